```python
import jax, jax.numpy as jnp
from jax import lax
import numpy as np

D_MODEL = 1024
BATCH = 4
SEQ = 4096
DEPTH = 2

CTX_LEN = 256
GRID_W = 64
EPS = 1e-6
N_MOD = 6
NEG_INF = -1e30

MLA_HEADS = 8
MLA_Q_LORA = 384
MLA_KV_LORA = 256
MLA_NOPE = 64
MLA_ROPE = 32
MLA_V = 64
ROPE_THETA = 10000.0
Q_BLOCK = 128

NA_HEADS = 8
NA_HEAD_DIM = 64
NA_KH = 8
NA_KW = 16

GLA_HEADS = 4
GLA_DK = 64
GLA_DV = 128
GLA_GATE_RANK = 16
GLA_GATE_NORM = 16.0
GLA_CHUNK = 64

PEER_HEADS = 8
PEER_N_KEYS = 128
PEER_N_EXPERTS = PEER_N_KEYS * PEER_N_KEYS
PEER_QUERY_DIM = 128
PEER_TOPK = 16
PEER_BLOCK = 128

MLA_WIDTH = MLA_HEADS * MLA_V
NA_WIDTH = NA_HEADS * NA_HEAD_DIM
GLA_K_WIDTH = GLA_HEADS * GLA_DK
GLA_V_WIDTH = GLA_HEADS * GLA_DV
N_BRANCH = 3

IN_SPLITS = (MLA_Q_LORA, MLA_KV_LORA, MLA_ROPE, NA_WIDTH, NA_WIDTH, NA_WIDTH,
             GLA_K_WIDTH, GLA_K_WIDTH, GLA_V_WIDTH, GLA_V_WIDTH, GLA_GATE_RANK, GLA_GATE_RANK,
             N_BRANCH * D_MODEL)
IN_COLS = sum(IN_SPLITS)

kernel_name = 'hybrid_mla_natten_gla_peer_block'


def rms_norm(x, w):
    xf = x.astype(jnp.float32)
    y = xf * lax.rsqrt(jnp.mean(xf * xf, axis=-1, keepdims=True) + EPS)
    return (y * w.astype(jnp.float32)).astype(x.dtype)


def modulate(h, shift, scale):
    return h * (1 + scale) + shift


def split_cols(p):
    out, o = [], 0
    for w in IN_SPLITS:
        out.append(p[..., o:o + w])
        o += w
    return out


def axial_rope(L, dtype):
    t = jnp.arange(L)
    rows = (t // GRID_W).astype(jnp.float32)
    cols = (t % GRID_W).astype(jnp.float32)
    half = MLA_ROPE // 2
    inv = ROPE_THETA ** (-jnp.arange(0, half, 2, dtype=jnp.float32) / half)
    ar = rows[:, None] * inv
    ac = cols[:, None] * inv
    ang = jnp.concatenate([ar, ar, ac, ac], axis=-1)
    return jnp.cos(ang).astype(dtype), jnp.sin(ang).astype(dtype)


def apply_rope(x, cos, sin):
    q = MLA_ROPE // 4
    x0, x1, x2, x3 = x[..., :q], x[..., q:2 * q], x[..., 2 * q:3 * q], x[..., 3 * q:]
    rot = jnp.concatenate([-x1, x0, -x3, x2], axis=-1)
    return x * cos + rot * sin


def attend_blocks(q, k, v, scale):
    B, L, H, d = q.shape
    nb = L // Q_BLOCK
    qb = q.reshape(B, nb, Q_BLOCK, H, d).transpose(1, 0, 2, 3, 4)

    def one(qblk):
        s = jnp.einsum('bqhd,bkhd->bhqk', qblk, k).astype(jnp.float32) * scale
        p = jax.nn.softmax(s, axis=-1).astype(v.dtype)
        return jnp.einsum('bhqk,bkhe->bqhe', p, v)

    o = lax.map(one, qb)
    return o.transpose(1, 0, 2, 3, 4).reshape(B, L, H, v.shape[-1])


def mla_queries(cq, lp):
    B, L, _ = cq.shape
    q = rms_norm(cq, lp['mla_q_norm']) @ lp['mla_w_uq']
    return q.reshape(B, L, MLA_HEADS, MLA_NOPE + MLA_ROPE)


def mla_keys_values(ckv, k_rope, lp):
    B, L, _ = ckv.shape
    kv = (rms_norm(ckv, lp['mla_kv_norm']) @ lp['mla_w_ukv']).reshape(B, L, MLA_HEADS, MLA_NOPE + MLA_V)
    k_nope, v = kv[..., :MLA_NOPE], kv[..., MLA_NOPE:]
    k_rope = jnp.broadcast_to(k_rope[:, :, None, :], (B, L, MLA_HEADS, MLA_ROPE))
    return jnp.concatenate([k_nope, k_rope], axis=-1), v


def mla_mixer(pl, pc, lp, need_ctx):
    B, L, _ = pl[0].shape
    scale = (MLA_NOPE + MLA_ROPE) ** -0.5
    cos, sin = axial_rope(L, pl[0].dtype)
    q = mla_queries(pl[0], lp)
    q = jnp.concatenate([q[..., :MLA_NOPE], apply_rope(q[..., MLA_NOPE:], cos[:, None, :], sin[:, None, :])], axis=-1)
    k, v = mla_keys_values(pl[1], apply_rope(pl[2], cos, sin), lp)
    k_c, v_c = mla_keys_values(pc[1], pc[2], lp)
    o = attend_blocks(q, jnp.concatenate([k_c, k], axis=1), jnp.concatenate([v_c, v], axis=1), scale)
    o = o.reshape(B, L, MLA_WIDTH)
    o_c = None
    if need_ctx:
        q_c = mla_queries(pc[0], lp)
        o_c = attend_blocks(q_c, k_c, v_c, scale).reshape(B, pc[0].shape[1], MLA_WIDTH)
    return o, o_c


def na_heads(t):
    B, L, _ = t.shape
    return t.reshape(B, L, NA_HEADS, NA_HEAD_DIM)


def na_mixer(pl, pc, lp, need_ctx):
    q, k, v = na_heads(pl[3]), na_heads(pl[4]), na_heads(pl[5])
    k_c, v_c = na_heads(pc[4]), na_heads(pc[5])
    B, L = q.shape[:2]
    rows = L // GRID_W
    kh = min(NA_KH, rows)
    scale = NA_HEAD_DIM ** -0.5
    r = jnp.arange(rows)
    row_idx = jnp.clip(r - kh // 2, 0, rows - kh)[:, None] + jnp.arange(kh)
    w = jnp.arange(GRID_W)
    col_start = jnp.clip(w - NA_KW // 2, 0, GRID_W - NA_KW)
    col_in = (w[None, :] >= col_start[:, None]) & (w[None, :] < col_start[:, None] + NA_KW)
    dr = row_idx - r[:, None] + (NA_KH - 1)
    dc = jnp.clip(w[None, :] - w[:, None], -(NA_KW - 1), NA_KW - 1) + (NA_KW - 1)
    bias = lp['na_rpb'][:, dr[:, None, :, None], dc[None, :, None, :]]
    bias = bias.transpose(1, 0, 2, 3, 4).astype(jnp.float32)
    qg = q.reshape(B, rows, GRID_W, NA_HEADS, NA_HEAD_DIM)
    kr = k.reshape(B, rows, GRID_W, NA_HEADS, NA_HEAD_DIM)[:, row_idx]
    vr = v.reshape(B, rows, GRID_W, NA_HEADS, NA_HEAD_DIM)[:, row_idx]
    s_lat = jnp.einsum('brqhd,brawhd->brhqaw', qg, kr).astype(jnp.float32) * scale + bias
    s_lat = jnp.where(col_in[:, None, :], s_lat, NEG_INF)
    s_ctx = jnp.einsum('brqhd,bchd->brhqc', qg, k_c).astype(jnp.float32) * scale
    s = jnp.concatenate([s_lat.reshape(B, rows, NA_HEADS, GRID_W, kh * GRID_W), s_ctx], axis=-1)
    p = jax.nn.softmax(s, axis=-1).astype(v.dtype)
    p_lat = p[..., :kh * GRID_W].reshape(B, rows, NA_HEADS, GRID_W, kh, GRID_W)
    p_ctx = p[..., kh * GRID_W:]
    o = jnp.einsum('brhqaw,brawhd->brqhd', p_lat, vr) + jnp.einsum('brhqc,bchd->brqhd', p_ctx, v_c)
    o = o.reshape(B, L, NA_WIDTH)
    o_c = None
    if need_ctx:
        o_c = attend_blocks(na_heads(pc[3]), k_c, v_c, scale).reshape(B, pc[3].shape[1], NA_WIDTH)
    return o, o_c


def gla_heads(t, dh):
    B, L, _ = t.shape
    return t.reshape(B, L, GLA_HEADS, dh).transpose(0, 2, 1, 3).astype(jnp.float32)


def gla_gate(low, w, b):
    return gla_heads(jax.nn.log_sigmoid((low @ w + b).astype(jnp.float32)) / GLA_GATE_NORM, GLA_DK)


def flip_seq(t):
    return jnp.flip(t, axis=2)


def gla_scan(q, k, v, g, s0):
    B, H, L, DK = q.shape
    DV = v.shape[-1]
    n, C = L // GLA_CHUNK, GLA_CHUNK
    q, k, g = [t.reshape(B, H, n, C, DK) for t in (q, k, g)]
    v = v.reshape(B, H, n, C, DV)
    b = jnp.cumsum(g, axis=-2)
    b_last = b[..., -1:, :]
    q_in = q * jnp.exp(b)
    k_in = k * jnp.exp(-b)
    k_end = k * jnp.exp(b_last - b)
    mask = jnp.tril(jnp.ones((C, C), dtype=bool))
    a = jnp.where(mask, jnp.einsum('bhnik,bhnjk->bhnij', q_in, k_in), 0.0)
    o_intra = jnp.einsum('bhnij,bhnjv->bhniv', a, v)
    decay = jnp.exp(b_last[..., 0, :])

    def step(s, inp):
        qi, ke, vi, de = inp
        o = jnp.einsum('bhik,bhkv->bhiv', qi, s)
        s = de[..., None] * s + jnp.einsum('bhjk,bhjv->bhkv', ke, vi)
        return s, o

    xs = tuple(jnp.moveaxis(t, 2, 0) for t in (q_in, k_end, v, decay))
    s_fin, o_inter = lax.scan(step, s0, xs)
    o = o_intra + jnp.moveaxis(o_inter, 0, 2)
    return o.reshape(B, H, L, DV), s_fin


def gla_final_state(k, v, g):
    b = jnp.cumsum(g, axis=2)
    return jnp.einsum('bhtk,bhtv->bhkv', k * jnp.exp(b[:, :, -1:, :] - b), v)


def gla_bidir(q, k, v, gf, gb, s0f, s0b):
    of, sf = gla_scan(q, k, v, gf, s0f)
    ob, sb = gla_scan(flip_seq(q), flip_seq(k), flip_seq(v), flip_seq(gb), s0b)
    return of + flip_seq(ob), sf, sb


def gla_out(o, og, norm_w):
    B, H, L, DV = o.shape
    o = rms_norm(o.transpose(0, 2, 1, 3), norm_w)
    return (o.reshape(B, L, GLA_V_WIDTH) * jax.nn.silu(og.astype(jnp.float32))).astype(og.dtype)


def gla_mixer(pl, pc, lp, need_ctx):
    scale = GLA_DK ** -0.5
    q = gla_heads(pl[6], GLA_DK) * scale
    k, v = gla_heads(pl[7], GLA_DK), gla_heads(pl[8], GLA_DV)
    gf = gla_gate(pl[10], lp['gla_w_gk_fwd'], lp['gla_b_gk_fwd'])
    gb = gla_gate(pl[11], lp['gla_w_gk_bwd'], lp['gla_b_gk_bwd'])
    k_c, v_c = gla_heads(pc[7], GLA_DK), gla_heads(pc[8], GLA_DV)
    gf_c = gla_gate(pc[10], lp['gla_w_gk_fwd'], lp['gla_b_gk_fwd'])
    gb_c = gla_gate(pc[11], lp['gla_w_gk_bwd'], lp['gla_b_gk_bwd'])
    o_c = None
    if need_ctx:
        q_c = gla_heads(pc[6], GLA_DK) * scale
        B = q_c.shape[0]
        z = jnp.zeros((B, GLA_HEADS, GLA_DK, GLA_DV), jnp.float32)
        o_c_raw, sf, sb = gla_bidir(q_c, k_c, v_c, gf_c, gb_c, z, z)
        o_c = gla_out(o_c_raw, pc[9], lp['gla_norm'])
    else:
        sf = gla_final_state(k_c, v_c, gf_c)
        sb = gla_final_state(flip_seq(k_c), flip_seq(v_c), flip_seq(gb_c))
    o_raw, _, _ = gla_bidir(q, k, v, gf, gb, sf, sb)
    return gla_out(o_raw, pl[9], lp['gla_norm']), o_c


def merge_branches(o_a, o_b, o_g, gate_cols, lp):
    ga, gb, gg = jnp.split(jax.nn.sigmoid(gate_cols), N_BRANCH, axis=-1)
    y = ga * (o_a @ lp['w_o_mla']) + gb * (o_b @ lp['w_o_na']) + gg * (o_g @ lp['w_o_gla'])
    return y @ lp['w_out']


def peer_ffn(h, lp):
    B, L, D = h.shape
    q = (h @ lp['peer_w_q']).reshape(B, L, PEER_HEADS, 2, PEER_QUERY_DIM // 2)
    s = jnp.einsum('blhpd,pkd->blhpk', q, lp['peer_sub_keys']).astype(jnp.float32)
    v_half, i_half = lax.top_k(s, PEER_TOPK)
    cand = (v_half[..., 0, :, None] + v_half[..., 1, None, :]).reshape(B, L, PEER_HEADS, PEER_TOPK * PEER_TOPK)
    cid = (i_half[..., 0, :, None] * PEER_N_KEYS + i_half[..., 1, None, :]).reshape(B, L, PEER_HEADS, PEER_TOPK * PEER_TOPK)
    top, pos = lax.top_k(cand, PEER_TOPK)
    eid = jnp.take_along_axis(cid, pos, axis=-1)
    gate = jax.nn.softmax(top, axis=-1).astype(h.dtype)
    nb = (B * L) // PEER_BLOCK
    kk = PEER_HEADS * PEER_TOPK
    hb = h.reshape(nb, PEER_BLOCK, D)
    eb = eid.reshape(nb, PEER_BLOCK, kk)
    gb = gate.reshape(nb, PEER_BLOCK, kk)
    u_tab, v_tab = lp['peer_u'], lp['peer_v']

    def one(args):
        ht, et, gt = args
        act = jnp.einsum('tkd,td->tk', u_tab[et], ht)
        return jnp.einsum('tk,tkd->td', gt * jax.nn.gelu(act), v_tab[et])

    out = lax.map(one, (hb, eb, gb))
    return out.reshape(B, L, D)


def hybrid_layer(x, xc, mod, mod_c, lp, need_ctx):
    sh1, sc1, g1, sh2, sc2, g2 = [mod[:, i, None, :] for i in range(N_MOD)]
    sh1c, sc1c, g1c, sh2c, sc2c, g2c = [mod_c[i] for i in range(N_MOD)]
    h = modulate(rms_norm(x, lp['norm1']), sh1, sc1)
    hc = modulate(rms_norm(xc, lp['norm1']), sh1c, sc1c)
    pl = split_cols(h @ lp['w_in'])
    pc = split_cols(hc @ lp['w_in'])
    o_a, o_a_c = mla_mixer(pl, pc, lp, need_ctx)
    o_b, o_b_c = na_mixer(pl, pc, lp, need_ctx)
    o_g, o_g_c = gla_mixer(pl, pc, lp, need_ctx)
    x = x + g1 * merge_branches(o_a, o_b, o_g, pl[12], lp)
    x = x + g2 * peer_ffn(modulate(rms_norm(x, lp['norm2']), sh2, sc2), lp)
    if need_ctx:
        xc = xc + g1c * merge_branches(o_a_c, o_b_c, o_g_c, pc[12], lp)
        xc = xc + g2c * peer_ffn(modulate(rms_norm(xc, lp['norm2']), sh2c, sc2c), lp)
    return x, xc


def setup_inputs(seed: int = 0) -> dict:
    key = jax.random.key(seed)
    ks = iter(jax.random.split(key, 40))

    def nrm(shape, s):
        return jax.random.normal(next(ks), shape, jnp.float32) * s

    d = D_MODEL
    return {
        'x': nrm((BATCH, SEQ, d), 1.0),
        'c': nrm((BATCH, d), 1.0),
        'ctx': nrm((BATCH, CTX_LEN, d), 1.0),
        'c_ctx': nrm((d,), 1.0),
        'w_ada': nrm((DEPTH, d, N_MOD * d), 0.25 * d ** -0.5),
        'b_ada': nrm((DEPTH, N_MOD * d), 0.01),
        'norm1': 1.0 + nrm((DEPTH, d), 0.02),
        'w_in': nrm((DEPTH, d, IN_COLS), d ** -0.5),
        'mla_q_norm': 1.0 + nrm((DEPTH, MLA_Q_LORA), 0.02),
        'mla_w_uq': nrm((DEPTH, MLA_Q_LORA, MLA_HEADS * (MLA_NOPE + MLA_ROPE)), MLA_Q_LORA ** -0.5),
        'mla_kv_norm': 1.0 + nrm((DEPTH, MLA_KV_LORA), 0.02),
        'mla_w_ukv': nrm((DEPTH, MLA_KV_LORA, MLA_HEADS * (MLA_NOPE + MLA_V)), MLA_KV_LORA ** -0.5),
        'na_rpb': nrm((DEPTH, NA_HEADS, 2 * NA_KH - 1, 2 * NA_KW - 1), 0.1),
        'gla_w_gk_fwd': nrm((DEPTH, GLA_GATE_RANK, GLA_K_WIDTH), GLA_GATE_RANK ** -0.5),
        'gla_b_gk_fwd': nrm((DEPTH, GLA_K_WIDTH), 0.01),
        'gla_w_gk_bwd': nrm((DEPTH, GLA_GATE_RANK, GLA_K_WIDTH), GLA_GATE_RANK ** -0.5),
        'gla_b_gk_bwd': nrm((DEPTH, GLA_K_WIDTH), 0.01),
        'gla_norm': 1.0 + nrm((DEPTH, GLA_DV), 0.02),
        'w_o_mla': nrm((DEPTH, MLA_WIDTH, d), MLA_WIDTH ** -0.5),
        'w_o_na': nrm((DEPTH, NA_WIDTH, d), NA_WIDTH ** -0.5),
        'w_o_gla': nrm((DEPTH, GLA_V_WIDTH, d), GLA_V_WIDTH ** -0.5),
        'w_out': nrm((DEPTH, d, d), d ** -0.5),
        'norm2': 1.0 + nrm((DEPTH, d), 0.02),
        'peer_w_q': nrm((DEPTH, d, PEER_HEADS * PEER_QUERY_DIM), d ** -0.5),
        'peer_sub_keys': nrm((DEPTH, 2, PEER_N_KEYS, PEER_QUERY_DIM // 2), (PEER_QUERY_DIM // 2) ** -0.5),
        'peer_u': nrm((DEPTH, PEER_N_EXPERTS, d), d ** -0.5),
        'peer_v': nrm((DEPTH, PEER_N_EXPERTS, d), 0.5),
        'final_norm': 1.0 + nrm((d,), 0.02),
    }


def reference(x, c, ctx, c_ctx, w_ada, b_ada, norm1, w_in, mla_q_norm, mla_w_uq, mla_kv_norm, mla_w_ukv,
              na_rpb, gla_w_gk_fwd, gla_b_gk_fwd, gla_w_gk_bwd, gla_b_gk_bwd, gla_norm, w_o_mla, w_o_na,
              w_o_gla, w_out, norm2, peer_w_q, peer_sub_keys, peer_u, peer_v, final_norm):
    B = x.shape[0]
    xc = ctx
    for l in range(DEPTH):
        lp = dict(norm1=norm1[l], w_in=w_in[l], mla_q_norm=mla_q_norm[l], mla_w_uq=mla_w_uq[l],
                  mla_kv_norm=mla_kv_norm[l], mla_w_ukv=mla_w_ukv[l], na_rpb=na_rpb[l],
                  gla_w_gk_fwd=gla_w_gk_fwd[l], gla_b_gk_fwd=gla_b_gk_fwd[l],
                  gla_w_gk_bwd=gla_w_gk_bwd[l], gla_b_gk_bwd=gla_b_gk_bwd[l], gla_norm=gla_norm[l],
                  w_o_mla=w_o_mla[l], w_o_na=w_o_na[l], w_o_gla=w_o_gla[l], w_out=w_out[l],
                  norm2=norm2[l], peer_w_q=peer_w_q[l], peer_sub_keys=peer_sub_keys[l],
                  peer_u=peer_u[l], peer_v=peer_v[l])
        mod = (jax.nn.silu(c) @ w_ada[l] + b_ada[l]).reshape(B, N_MOD, D_MODEL)
        mod_c = (jax.nn.silu(c_ctx) @ w_ada[l] + b_ada[l]).reshape(N_MOD, D_MODEL)
        x, xc = hybrid_layer(x, xc, mod, mod_c, lp, l < DEPTH - 1)
    return rms_norm(x, final_norm)
```

```python
import functools

import numpy as np
import jax
import jax.numpy as jnp
from jax import lax
from jax.experimental import pallas as pl
from jax.experimental.pallas import tpu as pltpu

F32 = jnp.float32
BF16 = jnp.bfloat16
HIGHEST = lax.Precision.HIGHEST

D_MODEL = 1024
DEPTH = 2
CTX_LEN = 256
GRID_W = 64
EPS = 1e-6
N_MOD = 6
NEG_INF = -1e30

MLA_HEADS = 8
MLA_Q_LORA = 384
MLA_KV_LORA = 256
MLA_NOPE = 64
MLA_ROPE = 32
MLA_V = 64
ROPE_THETA = 10000.0
MLA_HEAD_PAD = 128

NA_HEADS = 8
NA_HEAD_DIM = 64
NA_KH = 8
NA_KW = 16
NA_ROWS_PER_TILE = 4
NA_SPAN = NA_ROWS_PER_TILE + NA_KH

GLA_HEADS = 4
GLA_DK = 64
GLA_DV = 128
GLA_GATE_RANK = 16
GLA_GATE_NORM = 16.0
GLA_CHUNK = 64

PEER_HEADS = 8
PEER_N_KEYS = 128
PEER_N_EXPERTS = PEER_N_KEYS * PEER_N_KEYS
PEER_QUERY_DIM = 128
PEER_TOPK = 16
PEER_TOK_TILE = 512
PEER_EXP_TILE = 1024

MLA_WIDTH = MLA_HEADS * MLA_V
NA_WIDTH = NA_HEADS * NA_HEAD_DIM
GLA_K_WIDTH = GLA_HEADS * GLA_DK
GLA_V_WIDTH = GLA_HEADS * GLA_DV

LANES = 128
TOK_TILE = 256

MLA_COLS = MLA_Q_LORA + MLA_KV_LORA + 2 * LANES
NA_COLS = 3 * NA_WIDTH
GLA_COLS = 2 * GLA_K_WIDTH + 2 * GLA_V_WIDTH + LANES
GATE_COLS = 3 * D_MODEL
IN_COLS_PAD = MLA_COLS + NA_COLS + GLA_COLS + GATE_COLS

VMEM_LIMIT = 56 * 1024 * 1024


def _cparams(*sem):
    return pltpu.CompilerParams(dimension_semantics=sem, vmem_limit_bytes=VMEM_LIMIT)


def _rms(x, w):
    return x * lax.rsqrt(jnp.mean(x * x, axis=-1, keepdims=True) + EPS) * w


def _dot(a, b):
    return jnp.dot(a, b, preferred_element_type=F32)


def _dot_nt(a, b):
    return lax.dot_general(a, b, (((1,), (1,)), ((), ())), preferred_element_type=F32)


def _dot_tn(a, b):
    return lax.dot_general(a, b, (((0,), (0,)), ((), ())), preferred_element_type=F32)


def _dot_exact(a, b):
    return jnp.dot(a, b, preferred_element_type=F32, precision=HIGHEST)


def _mod_kernel(c_ref, w_ref, b_ref, o_ref):
    c = c_ref[...]
    o_ref[0] = _dot_exact(c * jax.nn.sigmoid(c), w_ref[0]) + b_ref[0]


def _modulation(cc, w_ada, b_ada):
    depth, d, n = w_ada.shape
    tn = 1536
    return pl.pallas_call(
        _mod_kernel,
        grid=(depth, n // tn),
        in_specs=[
            pl.BlockSpec((8, d), lambda l, j: (0, 0)),
            pl.BlockSpec((1, d, tn), lambda l, j: (l, 0, j)),
            pl.BlockSpec((1, 1, tn), lambda l, j: (l, 0, j)),
        ],
        out_specs=pl.BlockSpec((1, 8, tn), lambda l, j: (l, 0, j)),
        out_shape=jax.ShapeDtypeStruct((depth, 8, n), F32),
        compiler_params=_cparams("parallel", "parallel"),
        name="adaln_mod",
    )(cc, w_ada, b_ada.reshape(depth, 1, n))


def _mod_row(n_batch):
    return lambda b, i: (jnp.where(i == 0, n_batch, b), 0, 0)


def _inproj_kernel(x_ref, mod_ref, nw_ref, w_ref, mla_ref, na_ref, gla_ref, gate_ref):
    h = _rms(x_ref[0], nw_ref[...]) * (1.0 + mod_ref[0, 1:2, :]) + mod_ref[0, 0:1, :]
    h = h.astype(BF16)
    o = 0
    mla_ref[0] = _dot(h, w_ref[:, o:o + MLA_COLS])
    o += MLA_COLS
    na_ref[0] = _dot(h, w_ref[:, o:o + NA_COLS]).astype(BF16)
    o += NA_COLS
    gla_ref[0] = _dot(h, w_ref[:, o:o + GLA_COLS])
    o += GLA_COLS
    gate_ref[0] = _dot(h, w_ref[:, o:o + GATE_COLS])


def _in_projection(xs, mod, norm_w, w_in_p):
    nb, t, d = xs.shape
    tok = lambda b, i: (b, i, 0)
    return pl.pallas_call(
        _inproj_kernel,
        grid=(nb, t // TOK_TILE),
        in_specs=[
            pl.BlockSpec((1, TOK_TILE, d), tok),
            pl.BlockSpec((1, N_MOD, d), _mod_row(nb)),
            pl.BlockSpec((1, d), lambda b, i: (0, 0)),
            pl.BlockSpec((d, IN_COLS_PAD), lambda b, i: (0, 0), pipeline_mode=pl.Buffered(1)),
        ],
        out_specs=[
            pl.BlockSpec((1, TOK_TILE, MLA_COLS), tok),
            pl.BlockSpec((1, TOK_TILE, NA_COLS), tok),
            pl.BlockSpec((1, TOK_TILE, GLA_COLS), tok),
            pl.BlockSpec((1, TOK_TILE, GATE_COLS), tok),
        ],
        out_shape=[
            jax.ShapeDtypeStruct((nb, t, MLA_COLS), F32),
            jax.ShapeDtypeStruct((nb, t, NA_COLS), BF16),
            jax.ShapeDtypeStruct((nb, t, GLA_COLS), F32),
            jax.ShapeDtypeStruct((nb, t, GATE_COLS), F32),
        ],
        compiler_params=_cparams("parallel", "parallel"),
        name="in_proj",
    )(xs, mod, norm_w.reshape(1, d), w_in_p)


def _mla_prep_kernel(p_ref, cos_ref, sin_ref, qn_ref, kvn_ref, wq_ref, wqr_ref, wk_ref, e_ref, wv_ref,
                     q_ref, k_ref, v_ref):
    scale = (MLA_NOPE + MLA_ROPE) ** -0.5
    cos = cos_ref[...]
    sin = sin_ref[...]
    o = MLA_Q_LORA + MLA_KV_LORA
    nq = _rms(p_ref[0, :, :MLA_Q_LORA], qn_ref[...]).astype(BF16)
    nkv = _rms(p_ref[0, :, MLA_Q_LORA:o], kvn_ref[...]).astype(BF16)
    q = _dot(nq, wq_ref[...])
    q_rot = _dot(nq, wqr_ref[...])
    for h in range(MLA_HEADS):
        sl = slice(h * MLA_HEAD_PAD, (h + 1) * MLA_HEAD_PAD)
        q_ref[0, :, sl] = ((q[:, sl] * cos + q_rot[:, sl] * sin) * scale).astype(BF16)
    k_rope = (p_ref[0, :, o:o + LANES] * cos + p_ref[0, :, o + LANES:o + 2 * LANES] * sin).astype(BF16)
    k_ref[0] = (_dot(nkv, wk_ref[...]) + _dot(k_rope, e_ref[...])).astype(BF16)
    v_ref[0] = _dot(nkv, wv_ref[...]).astype(BF16)


def _mla_prep(p_mla, cos_t, sin_t, q_norm, kv_norm, wq, wq_rot, wk, e_rope, wv):
    nb, t, _ = p_mla.shape
    hw = MLA_HEADS * MLA_HEAD_PAD
    tok = lambda b, i: (b, i, 0)
    const = lambda b, i: (0, 0)
    return pl.pallas_call(
        _mla_prep_kernel,
        grid=(nb, t // TOK_TILE),
        in_specs=[
            pl.BlockSpec((1, TOK_TILE, MLA_COLS), tok),
            pl.BlockSpec((TOK_TILE, LANES), lambda b, i: (i, 0)),
            pl.BlockSpec((TOK_TILE, LANES), lambda b, i: (i, 0)),
            pl.BlockSpec((1, MLA_Q_LORA), const),
            pl.BlockSpec((1, MLA_KV_LORA), const),
            pl.BlockSpec((MLA_Q_LORA, hw), const),
            pl.BlockSpec((MLA_Q_LORA, hw), const),
            pl.BlockSpec((MLA_KV_LORA, hw), const),
            pl.BlockSpec((LANES, hw), const),
            pl.BlockSpec((MLA_KV_LORA, MLA_WIDTH), const),
        ],
        out_specs=[
            pl.BlockSpec((1, TOK_TILE, hw), tok),
            pl.BlockSpec((1, TOK_TILE, hw), tok),
            pl.BlockSpec((1, TOK_TILE, MLA_WIDTH), tok),
        ],
        out_shape=[
            jax.ShapeDtypeStruct((nb, t, hw), BF16),
            jax.ShapeDtypeStruct((nb, t, hw), BF16),
            jax.ShapeDtypeStruct((nb, t, MLA_WIDTH), BF16),
        ],
        compiler_params=_cparams("parallel", "parallel"),
        name="mla_prep",
    )(p_mla, cos_t, sin_t, q_norm.reshape(1, -1), kv_norm.reshape(1, -1), wq, wq_rot, wk, e_rope, wv)


def _softmax_pv(s_list, v_list):
    m = s_list[0].max(axis=-1, keepdims=True)
    for s in s_list[1:]:
        m = jnp.maximum(m, s.max(axis=-1, keepdims=True))
    acc = None
    den = None
    for s, v in zip(s_list, v_list):
        p = jnp.exp(s - m)
        l = p.sum(axis=-1, keepdims=True)
        o = _dot(p.astype(BF16), v)
        acc = o if acc is None else acc + o
        den = l if den is None else den + l
    return acc / den


def _pair_select(o0, o1, half):
    lane = lax.broadcasted_iota(jnp.int32, o0.shape, 1)
    return jnp.where(lane < half, o0, o1)


def _mla_attn_kernel(q_ref, k_ref, v_ref, o_ref):
    def attend(nk):
        outs = []
        for j in range(2):
            sl = slice(j * MLA_HEAD_PAD, (j + 1) * MLA_HEAD_PAD)
            s = _dot_nt(q_ref[0, :, sl], k_ref[0, :nk, sl])
            outs.append(_softmax_pv([s], [v_ref[0, :nk, :]]))
        o_ref[0] = _pair_select(outs[0], outs[1], MLA_V).astype(BF16)

    is_ctx = pl.program_id(2) == 0

    @pl.when(is_ctx)
    def _():
        attend(CTX_LEN)

    @pl.when(jnp.logical_not(is_ctx))
    def _():
        attend(k_ref.shape[1])


def _mla_attention(q, k, v):
    nb, t, _ = q.shape
    pw = 2 * MLA_HEAD_PAD
    return pl.pallas_call(
        _mla_attn_kernel,
        grid=(nb, MLA_HEADS // 2, t // TOK_TILE),
        in_specs=[
            pl.BlockSpec((1, TOK_TILE, pw), lambda b, p, i: (b, i, p)),
            pl.BlockSpec((1, t, pw), lambda b, p, i: (b, 0, p)),
            pl.BlockSpec((1, t, 2 * MLA_V), lambda b, p, i: (b, 0, p)),
        ],
        out_specs=pl.BlockSpec((1, TOK_TILE, 2 * MLA_V), lambda b, p, i: (b, i, p)),
        out_shape=jax.ShapeDtypeStruct((nb, t, MLA_WIDTH), BF16),
        compiler_params=_cparams("parallel", "parallel", "arbitrary"),
        name="mla_attn",
    )(q, k, v)


def _na_tile_geometry(rows):
    n_tiles = rows // NA_ROWS_PER_TILE
    r0 = np.arange(n_tiles) * NA_ROWS_PER_TILE
    kb = np.clip(r0 - NA_KH // 2, 0, rows - NA_SPAN)
    pat = np.where(r0 == 0, 0, np.where(r0 == rows - NA_ROWS_PER_TILE, 2, 1))
    return kb, pat


def _na_bias_index(rows, tile):
    kh = min(NA_KH, rows)
    kb, _ = _na_tile_geometry(rows)
    r = tile * NA_ROWS_PER_TILE + np.arange(NA_ROWS_PER_TILE)
    start = np.clip(r - kh // 2, 0, rows - kh)
    kr = kb[tile] + np.arange(NA_SPAN)
    row_ok = (kr[None, :] >= start[:, None]) & (kr[None, :] < start[:, None] + kh)
    dr = kr[None, :] - r[:, None] + (NA_KH - 1)
    w = np.arange(GRID_W)
    col_start = np.clip(w - NA_KW // 2, 0, GRID_W - NA_KW)
    col_in = (w[None, :] >= col_start[:, None]) & (w[None, :] < col_start[:, None] + NA_KW)
    dc = np.clip(w[None, :] - w[:, None], -(NA_KW - 1), NA_KW - 1) + (NA_KW - 1)
    valid = row_ok[:, None, :, None] & col_in[None, :, None, :]
    dr = np.broadcast_to(np.clip(dr, 0, 2 * NA_KH - 2)[:, None, :, None], valid.shape)
    dc = np.broadcast_to(dc[None, :, None, :], valid.shape)
    n = NA_ROWS_PER_TILE * GRID_W
    return dr.reshape(n, -1), dc.reshape(n, -1), valid.reshape(n, -1)


def _na_bias_table(rpb, rows):
    n_tiles = rows // NA_ROWS_PER_TILE
    pats = []
    for tile in (0, 1, n_tiles - 1):
        dr, dc, valid = _na_bias_index(rows, tile)
        pats.append(jnp.where(valid[None], rpb[:, dr, dc], NEG_INF))
    return jnp.stack(pats, axis=1).astype(F32)


def _na_attn_kernel(kb_ref, pat_ref, q_ref, k_ref, v_ref, bias_ref, o_ref):
    i = pl.program_id(2)
    scale = NA_HEAD_DIM ** -0.5
    lane = lax.broadcasted_iota(jnp.int32, (TOK_TILE, 2 * NA_HEAD_DIM), 1)

    def head_q(j):
        q = q_ref[0].astype(F32)
        return jnp.where((lane < NA_HEAD_DIM) if j == 0 else (lane >= NA_HEAD_DIM), q, 0.0).astype(BF16)

    @pl.when(i == 0)
    def _():
        outs = []
        for j in range(2):
            s = _dot_nt(head_q(j), k_ref[0, :CTX_LEN, :]) * scale
            outs.append(_softmax_pv([s], [v_ref[0, :CTX_LEN, :]]))
        o_ref[0] = _pair_select(outs[0], outs[1], NA_HEAD_DIM).astype(BF16)

    @pl.when(i > 0)
    def _():
        start = pl.multiple_of(CTX_LEN + kb_ref[i - 1] * GRID_W, GRID_W)
        nk = NA_SPAN * GRID_W
        k_lat = k_ref[0, pl.ds(start, nk), :]
        v_lat = v_ref[0, pl.ds(start, nk), :]
        pat = pat_ref[i - 1]
        outs = []
        for j in range(2):
            qj = head_q(j)
            s_lat = _dot_nt(qj, k_lat) * scale + bias_ref[j, pat]
            s_ctx = _dot_nt(qj, k_ref[0, :CTX_LEN, :]) * scale
            outs.append(_softmax_pv([s_lat, s_ctx], [v_lat, v_ref[0, :CTX_LEN, :]]))
        o_ref[0] = _pair_select(outs[0], outs[1], NA_HEAD_DIM).astype(BF16)


def _na_attention(p_na, bias, rows):
    nb, t, _ = p_na.shape
    pw = 2 * NA_HEAD_DIM
    n_pairs = NA_HEADS // 2
    kb, pat = _na_tile_geometry(rows)
    nkeys = NA_SPAN * GRID_W
    grid_spec = pltpu.PrefetchScalarGridSpec(
        num_scalar_prefetch=2,
        grid=(n_pairs, nb, t // TOK_TILE),
        in_specs=[
            pl.BlockSpec((1, TOK_TILE, pw), lambda p, b, i, *_: (b, i, p)),
            pl.BlockSpec((1, t, pw), lambda p, b, i, *_: (b, 0, n_pairs + p)),
            pl.BlockSpec((1, t, pw), lambda p, b, i, *_: (b, 0, 2 * n_pairs + p)),
            pl.BlockSpec((2, 3, TOK_TILE, nkeys), lambda p, b, i, *_: (p, 0, 0, 0)),
        ],
        out_specs=pl.BlockSpec((1, TOK_TILE, pw), lambda p, b, i, *_: (b, i, p)),
    )
    return pl.pallas_call(
        _na_attn_kernel,
        grid_spec=grid_spec,
        out_shape=jax.ShapeDtypeStruct((nb, t, NA_WIDTH), BF16),
        compiler_params=_cparams("parallel", "parallel", "arbitrary"),
        name="na_attn",
    )(jnp.asarray(kb, jnp.int32), jnp.asarray(pat, jnp.int32), p_na, p_na, p_na, bias)


def _gla_direction(slab_ref, wg_ref, bg_ref, st_ref, o_ref, d):
    n = TOK_TILE
    n_chunks = n // GLA_CHUNK
    pw = 2 * GLA_DK
    row = lax.broadcasted_iota(jnp.int32, (n, n), 0)
    col = lax.broadcasted_iota(jnp.int32, (n, n), 1)
    shift = GLA_CHUNK.bit_length() - 1
    same_chunk = jnp.right_shift(row, shift) == jnp.right_shift(col, shift)
    causal = same_chunk & ((col <= row) if d == 0 else (col >= row))
    tri = jnp.where(causal, 1.0, 0.0).astype(F32)
    ones = jnp.where(same_chunk, 1.0, 0.0).astype(F32)

    o_low = 2 * GLA_K_WIDTH + 2 * GLA_V_WIDTH
    gw = GLA_K_WIDTH
    g_raw = _dot_exact(slab_ref[0, :, o_low:o_low + LANES], wg_ref[:, d * gw:(d + 1) * gw]) + bg_ref[:, d * gw:(d + 1) * gw]
    g = jax.nn.log_sigmoid(g_raw) / GLA_GATE_NORM
    b = _dot_exact(tri, g)
    tot = _dot_exact(ones, g)
    e_b = jnp.exp(b)
    e_nb = jnp.exp(-b)
    e_end = jnp.exp(tot - b)
    decay = jnp.exp(tot)
    lane = lax.broadcasted_iota(jnp.int32, (n, pw), 1)
    chunk_order = range(n_chunks) if d == 0 else range(n_chunks - 1, -1, -1)

    for p in range(GLA_HEADS // 2):
        sl = slice(p * pw, (p + 1) * pw)
        q_in = slab_ref[0, :, sl] * e_b[:, sl] * (GLA_DK ** -0.5)
        k_p = slab_ref[0, :, GLA_K_WIDTH + p * pw:GLA_K_WIDTH + (p + 1) * pw]
        k_in = (k_p * e_nb[:, sl]).astype(BF16)
        k_end = (k_p * e_end[:, sl]).astype(BF16)
        for j in range(2):
            h = 2 * p + j
            q_h = jnp.where((lane < GLA_DK) if j == 0 else (lane >= GLA_DK), q_in, 0.0).astype(BF16)
            v_h = slab_ref[0, :, 2 * GLA_K_WIDTH + h * GLA_DV:2 * GLA_K_WIDTH + (h + 1) * GLA_DV].astype(BF16)
            a = jnp.where(causal, _dot_nt(q_h, k_in), 0.0)
            o_intra = _dot(a.astype(BF16), v_h)
            for c in chunk_order:
                rs = slice(c * GLA_CHUNK, (c + 1) * GLA_CHUNK)
                st = st_ref[d, h]
                o_ref[0, rs, h * GLA_DV:(h + 1) * GLA_DV] = o_intra[rs] + _dot_nt(q_h[rs], st.astype(BF16))
                st_ref[d, h] = st * decay[c * GLA_CHUNK:c * GLA_CHUNK + 1, sl] + _dot_tn(v_h[rs], k_end[rs])


def _gla_kernel(f_ref, b_ref, wg_ref, bg_ref, of_ref, ob_ref, st_ref):
    @pl.when(pl.program_id(1) == 0)
    def _():
        st_ref[...] = jnp.zeros_like(st_ref)

    _gla_direction(f_ref, wg_ref, bg_ref, st_ref, of_ref, 0)
    _gla_direction(b_ref, wg_ref, bg_ref, st_ref, ob_ref, 1)


def _gla_scan(p_gla, wg, bg):
    nb, t, _ = p_gla.shape
    n_tiles = t // TOK_TILE
    fwd = lambda b, s: (b, s, 0)
    bwd = lambda b, s: (b, jnp.where(s == 0, 0, n_tiles - s), 0)
    return pl.pallas_call(
        _gla_kernel,
        grid=(nb, n_tiles),
        in_specs=[
            pl.BlockSpec((1, TOK_TILE, GLA_COLS), fwd),
            pl.BlockSpec((1, TOK_TILE, GLA_COLS), bwd),
            pl.BlockSpec((LANES, 2 * GLA_K_WIDTH), lambda b, s: (0, 0)),
            pl.BlockSpec((1, 2 * GLA_K_WIDTH), lambda b, s: (0, 0)),
        ],
        out_specs=[
            pl.BlockSpec((1, TOK_TILE, GLA_V_WIDTH), fwd),
            pl.BlockSpec((1, TOK_TILE, GLA_V_WIDTH), bwd),
        ],
        out_shape=[jax.ShapeDtypeStruct((nb, t, GLA_V_WIDTH), F32)] * 2,
        scratch_shapes=[pltpu.VMEM((2, GLA_HEADS, GLA_DV, 2 * GLA_DK), F32)],
        compiler_params=_cparams("parallel", "arbitrary"),
        name="gla_scan",
    )(p_gla, p_gla, wg, bg)


def _merge_kernel(x_ref, mod_ref, oa_ref, ob_ref, gf_ref, gb_ref, og_ref, gate_ref, gn_ref, n2_ref,
                  wa_ref, wb_ref, wg_ref, wo_ref, x_out_ref, h_out_ref):
    og = og_ref[0]
    o_raw = gf_ref[0] + gb_ref[0]
    parts = []
    for h in range(GLA_HEADS):
        sl = slice(h * GLA_DV, (h + 1) * GLA_DV)
        parts.append(_rms(o_raw[:, sl], gn_ref[...]) * jax.nn.silu(og[:, sl]))
    o_g = jnp.concatenate(parts, axis=-1).astype(BF16)
    d = D_MODEL
    y = (jax.nn.sigmoid(gate_ref[0, :, 0:d]) * _dot(oa_ref[0], wa_ref[...])
         + jax.nn.sigmoid(gate_ref[0, :, d:2 * d]) * _dot(ob_ref[0], wb_ref[...])
         + jax.nn.sigmoid(gate_ref[0, :, 2 * d:3 * d]) * _dot(o_g, wg_ref[...]))
    x = x_ref[0] + mod_ref[0, 2:3, :] * _dot(y.astype(BF16), wo_ref[...])
    x_out_ref[0] = x
    h2 = _rms(x, n2_ref[...]) * (1.0 + mod_ref[0, 4:5, :]) + mod_ref[0, 3:4, :]
    h_out_ref[0] = h2.astype(BF16)


def _merge(xs, mod, o_a, o_b, gla_f, gla_b, p_gla, p_gate, gla_norm, norm2, wa, wb, wg, wo):
    nb, t, d = xs.shape
    tok = lambda b, i: (b, i, 0)
    const = lambda b, i: (0, 0)
    og_block = (2 * GLA_K_WIDTH) // GLA_V_WIDTH + 1
    return pl.pallas_call(
        _merge_kernel,
        grid=(nb, t // TOK_TILE),
        in_specs=[
            pl.BlockSpec((1, TOK_TILE, d), tok),
            pl.BlockSpec((1, N_MOD, d), _mod_row(nb)),
            pl.BlockSpec((1, TOK_TILE, MLA_WIDTH), tok),
            pl.BlockSpec((1, TOK_TILE, NA_WIDTH), tok),
            pl.BlockSpec((1, TOK_TILE, GLA_V_WIDTH), tok),
            pl.BlockSpec((1, TOK_TILE, GLA_V_WIDTH), tok),
            pl.BlockSpec((1, TOK_TILE, GLA_V_WIDTH), lambda b, i: (b, i, og_block)),
            pl.BlockSpec((1, TOK_TILE, GATE_COLS), tok),
            pl.BlockSpec((1, GLA_DV), const),
            pl.BlockSpec((1, d), const),
            pl.BlockSpec((MLA_WIDTH, d), const),
            pl.BlockSpec((NA_WIDTH, d), const),
            pl.BlockSpec((GLA_V_WIDTH, d), const),
            pl.BlockSpec((d, d), const),
        ],
        out_specs=[pl.BlockSpec((1, TOK_TILE, d), tok), pl.BlockSpec((1, TOK_TILE, d), tok)],
        out_shape=[jax.ShapeDtypeStruct((nb, t, d), F32), jax.ShapeDtypeStruct((nb, t, d), BF16)],
        compiler_params=_cparams("parallel", "parallel"),
        name="merge",
    )(xs, mod, o_a, o_b, gla_f, gla_b, p_gla, p_gate, gla_norm.reshape(1, -1), norm2.reshape(1, -1),
      wa, wb, wg, wo)


def _top_values(x, k):
    row = lax.broadcasted_iota(jnp.int32, x.shape, 0).astype(F32)
    out_row = lax.broadcasted_iota(jnp.int32, (k, x.shape[1]), 0)
    vals = jnp.zeros((k, x.shape[1]), F32)
    for r in range(k):
        m = jnp.max(x, axis=0, keepdims=True)
        vals = jnp.where(out_row == r, m, vals)
        first = jnp.min(jnp.where(x == m, row, float(x.shape[0])), axis=0, keepdims=True)
        x = jnp.where(row == first, -jnp.inf, x)
    return vals


def _peer_select_kernel(h_ref, wqt_ref, kp_ref, ht_ref, s0_ref, s1_ref, e0_ref, e1_ref, thr_ref):
    h = h_ref[...]
    ht_ref[...] = h.T
    q_t = _dot_nt(wqt_ref[...], h)
    nk = PEER_N_KEYS
    k = PEER_TOPK
    sub = lax.broadcasted_iota(jnp.int32, (8, LANES), 0)
    for hd in range(PEER_HEADS):
        s = _dot(kp_ref[...], q_t[hd * PEER_QUERY_DIM:(hd + 1) * PEER_QUERY_DIM].astype(BF16))
        for c in range(h.shape[0] // LANES):
            cs = slice(c * LANES, (c + 1) * LANES)
            s0, s1 = s[:nk, cs], s[nk:, cs]
            v0 = _top_values(s0, k)
            v1 = _top_values(s1, k)
            cand = [v0[0:1] + v1]
            for a in range(1, k):
                cand.append(jnp.where(sub < k // (a + 1), v0[a:a + 1] + v1[:8], -jnp.inf))
            top = _top_values(jnp.concatenate(cand, axis=0), k)
            z = jnp.sum(jnp.exp(top - top[0:1]), axis=0, keepdims=True)
            s0_ref[hd, :, cs] = s0
            s1_ref[hd, :, cs] = s1
            e0_ref[hd, :, cs] = jnp.exp(s0 - v0[0:1]) / z
            e1_ref[hd, :, cs] = jnp.exp(s1 - v1[0:1])
            thr_ref[hd:hd + 1, cs] = top[k - 1:k]


def _peer_select(h2, wq_t, key_pair):
    n, d = h2.shape
    tm = TOK_TILE
    fac = jax.ShapeDtypeStruct((PEER_HEADS, PEER_N_KEYS, n), F32)
    fac_spec = pl.BlockSpec((PEER_HEADS, PEER_N_KEYS, tm), lambda i: (0, 0, i))
    return pl.pallas_call(
        _peer_select_kernel,
        grid=(n // tm,),
        in_specs=[
            pl.BlockSpec((tm, d), lambda i: (i, 0)),
            pl.BlockSpec(wq_t.shape, lambda i: (0, 0)),
            pl.BlockSpec(key_pair.shape, lambda i: (0, 0)),
        ],
        out_specs=[pl.BlockSpec((d, tm), lambda i: (0, i)), fac_spec, fac_spec, fac_spec, fac_spec,
                   pl.BlockSpec((PEER_HEADS, tm), lambda i: (0, i))],
        out_shape=[jax.ShapeDtypeStruct((d, n), BF16), fac, fac, fac, fac,
                   jax.ShapeDtypeStruct((PEER_HEADS, n), F32)],
        compiler_params=_cparams("parallel"),
        name="peer_select",
    )(h2, wq_t, key_pair)


def _peer_dense_kernel(ht_ref, u_ref, vt_ref, s0_ref, e0_ref, s1_ref, e1_ref, thr_ref, o_ref, g_ref, act_ref, row_ref):
    nk = PEER_N_KEYS
    nh = PEER_HEADS
    for ii in range(PEER_EXP_TILE // nk):
        for hd in range(nh):
            row_ref[ii, hd:hd + 1, :] = s0_ref[hd, ii:ii + 1, :]
            row_ref[ii, nh + hd:nh + hd + 1, :] = e0_ref[hd, ii:ii + 1, :]

    def sub_block(ii, carry):
        r0 = pl.multiple_of(ii * nk, nk)
        act_ref[...] = _dot(u_ref[pl.ds(r0, nk), :], ht_ref[...])
        rs_n, cs_n = 64, LANES
        for rs in range(0, nk, rs_n):
            for cs in range(0, ht_ref.shape[1], cs_n):
                w = jnp.zeros((rs_n, cs_n), F32)
                for hd in range(nh):
                    s0 = row_ref[ii, hd:hd + 1, cs:cs + cs_n]
                    e0 = row_ref[ii, nh + hd:nh + hd + 1, cs:cs + cs_n]
                    s = s1_ref[hd, rs:rs + rs_n, cs:cs + cs_n] + s0
                    w = w + jnp.where(s >= thr_ref[hd:hd + 1, cs:cs + cs_n],
                                      e1_ref[hd, rs:rs + rs_n, cs:cs + cs_n] * e0, 0.0)
                act = act_ref[rs:rs + rs_n, cs:cs + cs_n]
                g_ref[pl.ds(r0 + rs, rs_n), cs:cs + cs_n] = (w * jax.nn.gelu(act)).astype(BF16)
        return carry

    lax.fori_loop(0, PEER_EXP_TILE // nk, sub_block, 0)
    part = _dot(vt_ref[...], g_ref[...])

    @pl.when(pl.program_id(1) == 0)
    def _():
        o_ref[...] = part

    @pl.when(pl.program_id(1) > 0)
    def _():
        o_ref[...] += part


def _peer_dense(h_t, u, v_t, s0, e0, s1, e1, thr):
    d, n = h_t.shape
    tm, te = PEER_TOK_TILE, PEER_EXP_TILE
    sub = te // PEER_N_KEYS
    return pl.pallas_call(
        _peer_dense_kernel,
        grid=(n // tm, PEER_N_EXPERTS // te),
        in_specs=[
            pl.BlockSpec((d, tm), lambda i, e: (0, i)),
            pl.BlockSpec((te, d), lambda i, e: (e, 0)),
            pl.BlockSpec((d, te), lambda i, e: (0, e)),
            pl.BlockSpec((PEER_HEADS, sub, tm), lambda i, e: (0, e, i)),
            pl.BlockSpec((PEER_HEADS, sub, tm), lambda i, e: (0, e, i)),
            pl.BlockSpec((PEER_HEADS, PEER_N_KEYS, tm), lambda i, e: (0, 0, i)),
            pl.BlockSpec((PEER_HEADS, PEER_N_KEYS, tm), lambda i, e: (0, 0, i)),
            pl.BlockSpec((PEER_HEADS, tm), lambda i, e: (0, i)),
        ],
        out_specs=pl.BlockSpec((d, tm), lambda i, e: (0, i)),
        out_shape=jax.ShapeDtypeStruct((d, n), F32),
        scratch_shapes=[pltpu.VMEM((te, tm), BF16), pltpu.VMEM((PEER_N_KEYS, tm), F32),
                        pltpu.VMEM((sub, 2 * PEER_HEADS, tm), F32)],
        compiler_params=_cparams("parallel", "arbitrary"),
        name="peer_dense",
    )(h_t, u, v_t, s0, e0, s1, e1, thr)


def _peer_residual_kernel(x_ref, mod_ref, ot_ref, fn_ref, o_ref, *, final):
    x = x_ref[0] + mod_ref[0, 5:6, :] * ot_ref[...].T
    o_ref[0] = _rms(x, fn_ref[...]) if final else x


def _peer_residual(xs, mod, out_t, final_norm, final):
    nb, t, d = xs.shape
    n_tiles = t // TOK_TILE
    tok = lambda b, i: (b, i, 0)
    return pl.pallas_call(
        functools.partial(_peer_residual_kernel, final=final),
        grid=(nb, n_tiles),
        in_specs=[
            pl.BlockSpec((1, TOK_TILE, d), tok),
            pl.BlockSpec((1, N_MOD, d), _mod_row(nb)),
            pl.BlockSpec((d, TOK_TILE), lambda b, i: (0, b * n_tiles + i)),
            pl.BlockSpec((1, d), lambda b, i: (0, 0)),
        ],
        out_specs=pl.BlockSpec((1, TOK_TILE, d), tok),
        out_shape=jax.ShapeDtypeStruct((nb, t, d), F32),
        compiler_params=_cparams("parallel", "parallel"),
        name="peer_residual",
    )(xs, mod, out_t, final_norm.reshape(1, d))


def _rot_cols(w):
    q = MLA_ROPE // 4
    return jnp.concatenate([-w[..., q:2 * q], w[..., :q], -w[..., 3 * q:], w[..., 2 * q:3 * q]], axis=-1)


def _rope_slab(w):
    return jnp.pad(w, ((0, 0), (MLA_NOPE, LANES - MLA_NOPE - MLA_ROPE)))


def _prep_w_in(w_in):
    o = MLA_Q_LORA + MLA_KV_LORA
    w_kr = w_in[:, o:o + MLA_ROPE]
    o2 = o + MLA_ROPE
    n_mid = NA_COLS + 2 * GLA_K_WIDTH + 2 * GLA_V_WIDTH
    low = w_in[:, o2 + n_mid:o2 + n_mid + 2 * GLA_GATE_RANK]
    parts = [w_in[:, :o], _rope_slab(w_kr), _rope_slab(_rot_cols(w_kr)), w_in[:, o2:o2 + n_mid],
             jnp.pad(low, ((0, 0), (0, LANES - 2 * GLA_GATE_RANK))), w_in[:, o2 + n_mid + 2 * GLA_GATE_RANK:]]
    return jnp.concatenate(parts, axis=1).astype(BF16)


def _prep_mla(w_uq, w_ukv):
    r = w_uq.shape[0]
    wq = w_uq.reshape(r, MLA_HEADS, MLA_NOPE + MLA_ROPE)
    pad = MLA_HEAD_PAD - MLA_NOPE - MLA_ROPE
    wq_p = jnp.pad(wq, ((0, 0), (0, 0), (0, pad)))
    wq_rot = jnp.pad(_rot_cols(wq[..., MLA_NOPE:]), ((0, 0), (0, 0), (MLA_NOPE, pad)))
    c = w_ukv.shape[0]
    wkv = w_ukv.reshape(c, MLA_HEADS, MLA_NOPE + MLA_V)
    wk = jnp.pad(wkv[..., :MLA_NOPE], ((0, 0), (0, 0), (0, MLA_HEAD_PAD - MLA_NOPE)))
    wv = wkv[..., MLA_NOPE:]
    hw = MLA_HEADS * MLA_HEAD_PAD
    lane = np.arange(LANES)
    col = np.arange(hw)
    is_rope = (lane >= MLA_NOPE) & (lane < MLA_NOPE + MLA_ROPE)
    e = (is_rope[:, None] & (col[None, :] % MLA_HEAD_PAD == lane[:, None])).astype(np.float32)
    return (wq_p.reshape(r, hw).astype(BF16), wq_rot.reshape(r, hw).astype(BF16),
            wk.reshape(c, hw).astype(BF16), jnp.asarray(e, BF16), wv.reshape(c, MLA_WIDTH).astype(BF16))


def _rope_tables(seq):
    pos = jnp.arange(seq)
    rows = (pos // GRID_W).astype(F32)
    cols = (pos % GRID_W).astype(F32)
    half = MLA_ROPE // 2
    inv = ROPE_THETA ** (-jnp.arange(0, half, 2, dtype=F32) / half)
    ar = rows[:, None] * inv
    ac = cols[:, None] * inv
    ang = jnp.concatenate([ar, ar, ac, ac], axis=-1)
    ang = jnp.concatenate([jnp.zeros((CTX_LEN, MLA_ROPE), F32), ang], axis=0)
    t = ang.shape[0]
    ones = jnp.ones((t, MLA_NOPE), F32)
    zeros = jnp.zeros((t, LANES - MLA_NOPE - MLA_ROPE), F32)
    cos_t = jnp.concatenate([ones, jnp.cos(ang), zeros], axis=1)
    sin_t = jnp.concatenate([0.0 * ones, jnp.sin(ang), zeros], axis=1)
    return cos_t, sin_t


def _prep_gla_gate(w_f, b_f, w_b, b_b):
    wg = jnp.zeros((LANES, 2 * GLA_K_WIDTH), F32)
    wg = wg.at[:GLA_GATE_RANK, :GLA_K_WIDTH].set(w_f)
    wg = wg.at[GLA_GATE_RANK:2 * GLA_GATE_RANK, GLA_K_WIDTH:].set(w_b)
    return wg, jnp.concatenate([b_f, b_b]).reshape(1, -1)


def _prep_peer_keys(sub_keys):
    half = PEER_QUERY_DIM // 2
    kp = jnp.zeros((2 * PEER_N_KEYS, PEER_QUERY_DIM), F32)
    kp = kp.at[:PEER_N_KEYS, :half].set(sub_keys[0])
    kp = kp.at[PEER_N_KEYS:, half:].set(sub_keys[1])
    return kp.astype(BF16)


def kernel(x, c, ctx, c_ctx, w_ada, b_ada, norm1, w_in, mla_q_norm, mla_w_uq, mla_kv_norm, mla_w_ukv, na_rpb, gla_w_gk_fwd, gla_b_gk_fwd, gla_w_gk_bwd, gla_b_gk_bwd, gla_norm, w_o_mla, w_o_na, w_o_gla, w_out, norm2, peer_w_q, peer_sub_keys, peer_u, peer_v, final_norm):
    nb, seq, d = x.shape
    assert ctx.shape[1] == CTX_LEN == TOK_TILE and seq % (NA_ROWS_PER_TILE * GRID_W) == 0
    assert NA_ROWS_PER_TILE * GRID_W == TOK_TILE and nb < 8
    rows = seq // GRID_W
    depth = w_ada.shape[0]

    xs = jnp.concatenate([ctx, x], axis=1)
    t = xs.shape[1]
    cc = jnp.zeros((8, d), F32).at[:nb].set(c).at[nb].set(c_ctx)
    mods = _modulation(cc, w_ada, b_ada).reshape(depth, 8, N_MOD, d)
    cos_t, sin_t = _rope_tables(seq)

    for l in range(depth):
        mod = mods[l]
        p_mla, p_na, p_gla, p_gate = _in_projection(xs, mod, norm1[l], _prep_w_in(w_in[l]))
        q, k, v = _mla_prep(p_mla, cos_t, sin_t, mla_q_norm[l], mla_kv_norm[l], *_prep_mla(mla_w_uq[l], mla_w_ukv[l]))
        o_a = _mla_attention(q, k, v)
        o_b = _na_attention(p_na, _na_bias_table(na_rpb[l], rows), rows)
        gla_f, gla_b = _gla_scan(p_gla, *_prep_gla_gate(gla_w_gk_fwd[l], gla_b_gk_fwd[l], gla_w_gk_bwd[l], gla_b_gk_bwd[l]))
        xs, h2 = _merge(xs, mod, o_a, o_b, gla_f, gla_b, p_gla, p_gate, gla_norm[l], norm2[l],
                        w_o_mla[l].astype(BF16), w_o_na[l].astype(BF16), w_o_gla[l].astype(BF16), w_out[l].astype(BF16))
        h_t, s0, s1, e0, e1, thr = _peer_select(h2.reshape(nb * t, d), peer_w_q[l].T.astype(BF16),
                                                _prep_peer_keys(peer_sub_keys[l]))
        out_t = _peer_dense(h_t, peer_u[l].astype(BF16), peer_v[l].T.astype(BF16), s0, e0, s1, e1, thr)
        xs = _peer_residual(xs, mod, out_t, final_norm, final=(l == depth - 1))
    return xs[:, CTX_LEN:, :]
```

```python
import functools

import numpy as np
import jax
import jax.numpy as jnp
from jax import lax
from jax.experimental import pallas as pl
from jax.experimental.pallas import tpu as pltpu

F32 = jnp.float32
BF16 = jnp.bfloat16
HIGHEST = lax.Precision.HIGHEST

D_MODEL = 1024
DEPTH = 2
CTX_LEN = 256
GRID_W = 64
EPS = 1e-6
N_MOD = 6
NEG_INF = -1e30

MLA_HEADS = 8
MLA_Q_LORA = 384
MLA_KV_LORA = 256
MLA_NOPE = 64
MLA_ROPE = 32
MLA_V = 64
ROPE_THETA = 10000.0
MLA_HEAD_PAD = 128

NA_HEADS = 8
NA_HEAD_DIM = 64
NA_KH = 8
NA_KW = 16
NA_ROWS_PER_TILE = 4
NA_SPAN = NA_ROWS_PER_TILE + NA_KH

GLA_HEADS = 4
GLA_DK = 64
GLA_DV = 128
GLA_GATE_RANK = 16
GLA_GATE_NORM = 16.0
GLA_CHUNK = 64

PEER_HEADS = 8
PEER_N_KEYS = 128
PEER_N_EXPERTS = PEER_N_KEYS * PEER_N_KEYS
PEER_QUERY_DIM = 128
PEER_TOPK = 16
PEER_TOK_TILE = 512
PEER_EXP_TILE = 1024
PEER_PIECE = 256

MLA_WIDTH = MLA_HEADS * MLA_V
NA_WIDTH = NA_HEADS * NA_HEAD_DIM
GLA_K_WIDTH = GLA_HEADS * GLA_DK
GLA_V_WIDTH = GLA_HEADS * GLA_DV

LANES = 128
TOK_TILE = 256

MLA_COLS = MLA_Q_LORA + MLA_KV_LORA + 2 * LANES
NA_COLS = 3 * NA_WIDTH
GLA_COLS = 2 * GLA_K_WIDTH + 2 * GLA_V_WIDTH + LANES
GATE_COLS = 3 * D_MODEL
IN_COLS_PAD = MLA_COLS + NA_COLS + GLA_COLS + GATE_COLS

VMEM_LIMIT = 56 * 1024 * 1024


def _cparams(*sem):
    return pltpu.CompilerParams(dimension_semantics=sem, vmem_limit_bytes=VMEM_LIMIT)


def _rms(x, w):
    return x * lax.rsqrt(jnp.mean(x * x, axis=-1, keepdims=True) + EPS) * w


def _dot(a, b):
    return jnp.dot(a, b, preferred_element_type=F32)


def _dot_nt(a, b):
    return lax.dot_general(a, b, (((1,), (1,)), ((), ())), preferred_element_type=F32)


def _dot_tn(a, b):
    return lax.dot_general(a, b, (((0,), (0,)), ((), ())), preferred_element_type=F32)


def _dot_exact(a, b):
    return jnp.dot(a, b, preferred_element_type=F32, precision=HIGHEST)


def _mod_kernel(c_ref, w_ref, b_ref, o_ref):
    c = c_ref[...]
    o_ref[0] = _dot_exact(c * jax.nn.sigmoid(c), w_ref[0]) + b_ref[0]


def _modulation(cc, w_ada, b_ada):
    depth, d, n = w_ada.shape
    tn = 1536
    return pl.pallas_call(
        _mod_kernel,
        grid=(depth, n // tn),
        in_specs=[
            pl.BlockSpec((8, d), lambda l, j: (0, 0)),
            pl.BlockSpec((1, d, tn), lambda l, j: (l, 0, j)),
            pl.BlockSpec((1, 1, tn), lambda l, j: (l, 0, j)),
        ],
        out_specs=pl.BlockSpec((1, 8, tn), lambda l, j: (l, 0, j)),
        out_shape=jax.ShapeDtypeStruct((depth, 8, n), F32),
        compiler_params=_cparams("parallel", "parallel"),
        name="adaln_mod",
    )(cc, w_ada, b_ada.reshape(depth, 1, n))


def _mod_row(n_batch):
    return lambda b, i: (jnp.where(i == 0, n_batch, b), 0, 0)


def _inproj_kernel(x_ref, mod_ref, nw_ref, w_ref, mla_ref, na_ref, gla_ref, gate_ref):
    h = _rms(x_ref[0], nw_ref[...]) * (1.0 + mod_ref[0, 1:2, :]) + mod_ref[0, 0:1, :]
    h = h.astype(BF16)
    o = 0
    mla_ref[0] = _dot(h, w_ref[:, o:o + MLA_COLS])
    o += MLA_COLS
    na_ref[0] = _dot(h, w_ref[:, o:o + NA_COLS]).astype(BF16)
    o += NA_COLS
    gla_ref[0] = _dot(h, w_ref[:, o:o + GLA_COLS])
    o += GLA_COLS
    gate_ref[0] = _dot(h, w_ref[:, o:o + GATE_COLS])


def _in_projection(xs, mod, norm_w, w_in_p):
    nb, t, d = xs.shape
    tok = lambda b, i: (b, i, 0)
    return pl.pallas_call(
        _inproj_kernel,
        grid=(nb, t // TOK_TILE),
        in_specs=[
            pl.BlockSpec((1, TOK_TILE, d), tok),
            pl.BlockSpec((1, N_MOD, d), _mod_row(nb)),
            pl.BlockSpec((1, d), lambda b, i: (0, 0)),
            pl.BlockSpec((d, IN_COLS_PAD), lambda b, i: (0, 0), pipeline_mode=pl.Buffered(1)),
        ],
        out_specs=[
            pl.BlockSpec((1, TOK_TILE, MLA_COLS), tok),
            pl.BlockSpec((1, TOK_TILE, NA_COLS), tok),
            pl.BlockSpec((1, TOK_TILE, GLA_COLS), tok),
            pl.BlockSpec((1, TOK_TILE, GATE_COLS), tok),
        ],
        out_shape=[
            jax.ShapeDtypeStruct((nb, t, MLA_COLS), F32),
            jax.ShapeDtypeStruct((nb, t, NA_COLS), BF16),
            jax.ShapeDtypeStruct((nb, t, GLA_COLS), F32),
            jax.ShapeDtypeStruct((nb, t, GATE_COLS), F32),
        ],
        compiler_params=_cparams("parallel", "parallel"),
        name="in_proj",
    )(xs, mod, norm_w.reshape(1, d), w_in_p)


def _mla_prep_kernel(p_ref, cos_ref, sin_ref, qn_ref, kvn_ref, wq_ref, wqr_ref, wk_ref, e_ref, wv_ref,
                     q_ref, k_ref, v_ref):
    scale = (MLA_NOPE + MLA_ROPE) ** -0.5
    cos = cos_ref[...]
    sin = sin_ref[...]
    o = MLA_Q_LORA + MLA_KV_LORA
    nq = _rms(p_ref[0, :, :MLA_Q_LORA], qn_ref[...]).astype(BF16)
    nkv = _rms(p_ref[0, :, MLA_Q_LORA:o], kvn_ref[...]).astype(BF16)
    q = _dot(nq, wq_ref[...])
    q_rot = _dot(nq, wqr_ref[...])
    for h in range(MLA_HEADS):
        sl = slice(h * MLA_HEAD_PAD, (h + 1) * MLA_HEAD_PAD)
        q_ref[0, :, sl] = ((q[:, sl] * cos + q_rot[:, sl] * sin) * scale).astype(BF16)
    k_rope = (p_ref[0, :, o:o + LANES] * cos + p_ref[0, :, o + LANES:o + 2 * LANES] * sin).astype(BF16)
    k_ref[0] = (_dot(nkv, wk_ref[...]) + _dot(k_rope, e_ref[...])).astype(BF16)
    v_ref[0] = _dot(nkv, wv_ref[...]).astype(BF16)


def _mla_prep(p_mla, cos_t, sin_t, q_norm, kv_norm, wq, wq_rot, wk, e_rope, wv):
    nb, t, _ = p_mla.shape
    hw = MLA_HEADS * MLA_HEAD_PAD
    tok = lambda b, i: (b, i, 0)
    const = lambda b, i: (0, 0)
    return pl.pallas_call(
        _mla_prep_kernel,
        grid=(nb, t // TOK_TILE),
        in_specs=[
            pl.BlockSpec((1, TOK_TILE, MLA_COLS), tok),
            pl.BlockSpec((TOK_TILE, LANES), lambda b, i: (i, 0)),
            pl.BlockSpec((TOK_TILE, LANES), lambda b, i: (i, 0)),
            pl.BlockSpec((1, MLA_Q_LORA), const),
            pl.BlockSpec((1, MLA_KV_LORA), const),
            pl.BlockSpec((MLA_Q_LORA, hw), const),
            pl.BlockSpec((MLA_Q_LORA, hw), const),
            pl.BlockSpec((MLA_KV_LORA, hw), const),
            pl.BlockSpec((LANES, hw), const),
            pl.BlockSpec((MLA_KV_LORA, MLA_WIDTH), const),
        ],
        out_specs=[
            pl.BlockSpec((1, TOK_TILE, hw), tok),
            pl.BlockSpec((1, TOK_TILE, hw), tok),
            pl.BlockSpec((1, TOK_TILE, MLA_WIDTH), tok),
        ],
        out_shape=[
            jax.ShapeDtypeStruct((nb, t, hw), BF16),
            jax.ShapeDtypeStruct((nb, t, hw), BF16),
            jax.ShapeDtypeStruct((nb, t, MLA_WIDTH), BF16),
        ],
        compiler_params=_cparams("parallel", "parallel"),
        name="mla_prep",
    )(p_mla, cos_t, sin_t, q_norm.reshape(1, -1), kv_norm.reshape(1, -1), wq, wq_rot, wk, e_rope, wv)


def _softmax_pv(s_list, v_list):
    m = s_list[0].max(axis=-1, keepdims=True)
    for s in s_list[1:]:
        m = jnp.maximum(m, s.max(axis=-1, keepdims=True))
    acc = None
    den = None
    for s, v in zip(s_list, v_list):
        p = jnp.exp(s - m)
        l = p.sum(axis=-1, keepdims=True)
        o = _dot(p.astype(BF16), v)
        acc = o if acc is None else acc + o
        den = l if den is None else den + l
    return acc / den


def _pair_select(o0, o1, half):
    lane = lax.broadcasted_iota(jnp.int32, o0.shape, 1)
    return jnp.where(lane < half, o0, o1)


def _mla_attn_kernel(q_ref, k_ref, v_ref, o_ref):
    def attend(nk):
        outs = []
        for j in range(2):
            sl = slice(j * MLA_HEAD_PAD, (j + 1) * MLA_HEAD_PAD)
            s = _dot_nt(q_ref[0, :, sl], k_ref[0, :nk, sl])
            outs.append(_softmax_pv([s], [v_ref[0, :nk, :]]))
        o_ref[0] = _pair_select(outs[0], outs[1], MLA_V).astype(BF16)

    is_ctx = pl.program_id(2) == 0

    @pl.when(is_ctx)
    def _():
        attend(CTX_LEN)

    @pl.when(jnp.logical_not(is_ctx))
    def _():
        attend(k_ref.shape[1])


def _mla_attention(q, k, v):
    nb, t, _ = q.shape
    pw = 2 * MLA_HEAD_PAD
    return pl.pallas_call(
        _mla_attn_kernel,
        grid=(nb, MLA_HEADS // 2, t // TOK_TILE),
        in_specs=[
            pl.BlockSpec((1, TOK_TILE, pw), lambda b, p, i: (b, i, p)),
            pl.BlockSpec((1, t, pw), lambda b, p, i: (b, 0, p)),
            pl.BlockSpec((1, t, 2 * MLA_V), lambda b, p, i: (b, 0, p)),
        ],
        out_specs=pl.BlockSpec((1, TOK_TILE, 2 * MLA_V), lambda b, p, i: (b, i, p)),
        out_shape=jax.ShapeDtypeStruct((nb, t, MLA_WIDTH), BF16),
        compiler_params=_cparams("parallel", "parallel", "arbitrary"),
        name="mla_attn",
    )(q, k, v)


def _na_tile_geometry(rows):
    n_tiles = rows // NA_ROWS_PER_TILE
    r0 = np.arange(n_tiles) * NA_ROWS_PER_TILE
    kb = np.clip(r0 - NA_KH // 2, 0, rows - NA_SPAN)
    pat = np.where(r0 == 0, 0, np.where(r0 == rows - NA_ROWS_PER_TILE, 2, 1))
    return kb, pat


def _na_bias_index(rows, tile):
    kh = min(NA_KH, rows)
    kb, _ = _na_tile_geometry(rows)
    r = tile * NA_ROWS_PER_TILE + np.arange(NA_ROWS_PER_TILE)
    start = np.clip(r - kh // 2, 0, rows - kh)
    kr = kb[tile] + np.arange(NA_SPAN)
    row_ok = (kr[None, :] >= start[:, None]) & (kr[None, :] < start[:, None] + kh)
    dr = kr[None, :] - r[:, None] + (NA_KH - 1)
    w = np.arange(GRID_W)
    col_start = np.clip(w - NA_KW // 2, 0, GRID_W - NA_KW)
    col_in = (w[None, :] >= col_start[:, None]) & (w[None, :] < col_start[:, None] + NA_KW)
    dc = np.clip(w[None, :] - w[:, None], -(NA_KW - 1), NA_KW - 1) + (NA_KW - 1)
    valid = row_ok[:, None, :, None] & col_in[None, :, None, :]
    dr = np.broadcast_to(np.clip(dr, 0, 2 * NA_KH - 2)[:, None, :, None], valid.shape)
    dc = np.broadcast_to(dc[None, :, None, :], valid.shape)
    n = NA_ROWS_PER_TILE * GRID_W
    return dr.reshape(n, -1), dc.reshape(n, -1), valid.reshape(n, -1)


def _na_bias_table(rpb, rows):
    nh, nd, _ = rpb.shape
    w = GRID_W
    kh = min(NA_KH, rows)
    edge = w - NA_KW
    r_ext = jnp.concatenate([jnp.broadcast_to(rpb[..., :1], (nh, nd, edge)), rpb,
                             jnp.broadcast_to(rpb[..., -1:], (nh, nd, edge)), jnp.zeros((nh, nd, 1), rpb.dtype)], axis=-1)
    toe = jnp.tile(r_ext, (1, 1, w))[..., :w * (2 * w - 1)].reshape(nh, nd, w, 2 * w - 1)[..., w - 1:]
    col = np.arange(w)
    col_start = np.clip(col - NA_KW // 2, 0, w - NA_KW)
    col_in = (col[None, :] >= col_start[:, None]) & (col[None, :] < col_start[:, None] + NA_KW)
    toe = jnp.where(col_in[None, None], toe, NEG_INF).astype(F32)
    masked = jnp.full((nh, w, w), NEG_INF, F32)
    kb, _ = _na_tile_geometry(rows)
    n_tiles = rows // NA_ROWS_PER_TILE
    pats = []
    for tile in (0, 1, n_tiles - 1):
        q_rows = []
        for dq in range(NA_ROWS_PER_TILE):
            r = tile * NA_ROWS_PER_TILE + dq
            start = int(np.clip(r - kh // 2, 0, rows - kh))
            blocks = []
            for dk in range(NA_SPAN):
                kr = int(kb[tile]) + dk
                blocks.append(toe[:, kr - r + NA_KH - 1] if start <= kr < start + kh else masked)
            q_rows.append(jnp.concatenate(blocks, axis=-1))
        pats.append(jnp.concatenate(q_rows, axis=1))
    return jnp.stack(pats, axis=1)


def _na_attn_kernel(kb_ref, pat_ref, q_ref, k_ref, v_ref, bias_ref, o_ref):
    i = pl.program_id(2)
    scale = NA_HEAD_DIM ** -0.5
    lane = lax.broadcasted_iota(jnp.int32, (TOK_TILE, 2 * NA_HEAD_DIM), 1)

    def head_q(j):
        q = q_ref[0].astype(F32)
        return jnp.where((lane < NA_HEAD_DIM) if j == 0 else (lane >= NA_HEAD_DIM), q, 0.0).astype(BF16)

    @pl.when(i == 0)
    def _():
        outs = []
        for j in range(2):
            s = _dot_nt(head_q(j), k_ref[0, :CTX_LEN, :]) * scale
            outs.append(_softmax_pv([s], [v_ref[0, :CTX_LEN, :]]))
        o_ref[0] = _pair_select(outs[0], outs[1], NA_HEAD_DIM).astype(BF16)

    @pl.when(i > 0)
    def _():
        start = pl.multiple_of(CTX_LEN + kb_ref[i - 1] * GRID_W, GRID_W)
        nk = NA_SPAN * GRID_W
        k_lat = k_ref[0, pl.ds(start, nk), :]
        v_lat = v_ref[0, pl.ds(start, nk), :]
        pat = pat_ref[i - 1]
        outs = []
        for j in range(2):
            qj = head_q(j)
            s_lat = _dot_nt(qj, k_lat) * scale + bias_ref[j, pat]
            s_ctx = _dot_nt(qj, k_ref[0, :CTX_LEN, :]) * scale
            outs.append(_softmax_pv([s_lat, s_ctx], [v_lat, v_ref[0, :CTX_LEN, :]]))
        o_ref[0] = _pair_select(outs[0], outs[1], NA_HEAD_DIM).astype(BF16)


def _na_attention(p_na, bias, rows):
    nb, t, _ = p_na.shape
    pw = 2 * NA_HEAD_DIM
    n_pairs = NA_HEADS // 2
    kb, pat = _na_tile_geometry(rows)
    nkeys = NA_SPAN * GRID_W
    grid_spec = pltpu.PrefetchScalarGridSpec(
        num_scalar_prefetch=2,
        grid=(n_pairs, nb, t // TOK_TILE),
        in_specs=[
            pl.BlockSpec((1, TOK_TILE, pw), lambda p, b, i, *_: (b, i, p)),
            pl.BlockSpec((1, t, pw), lambda p, b, i, *_: (b, 0, n_pairs + p)),
            pl.BlockSpec((1, t, pw), lambda p, b, i, *_: (b, 0, 2 * n_pairs + p)),
            pl.BlockSpec((2, 3, TOK_TILE, nkeys), lambda p, b, i, *_: (p, 0, 0, 0)),
        ],
        out_specs=pl.BlockSpec((1, TOK_TILE, pw), lambda p, b, i, *_: (b, i, p)),
    )
    return pl.pallas_call(
        _na_attn_kernel,
        grid_spec=grid_spec,
        out_shape=jax.ShapeDtypeStruct((nb, t, NA_WIDTH), BF16),
        compiler_params=_cparams("parallel", "parallel", "arbitrary"),
        name="na_attn",
    )(jnp.asarray(kb, jnp.int32), jnp.asarray(pat, jnp.int32), p_na, p_na, p_na, bias)


def _gla_direction(slab_ref, wg_ref, bg_ref, st_ref, o_ref, d):
    n = TOK_TILE
    n_chunks = n // GLA_CHUNK
    pw = 2 * GLA_DK
    row = lax.broadcasted_iota(jnp.int32, (n, n), 0)
    col = lax.broadcasted_iota(jnp.int32, (n, n), 1)
    shift = GLA_CHUNK.bit_length() - 1
    same_chunk = jnp.right_shift(row, shift) == jnp.right_shift(col, shift)
    causal = same_chunk & ((col <= row) if d == 0 else (col >= row))
    tri = jnp.where(causal, 1.0, 0.0).astype(F32)
    ones = jnp.where(same_chunk, 1.0, 0.0).astype(F32)

    o_low = 2 * GLA_K_WIDTH + 2 * GLA_V_WIDTH
    gw = GLA_K_WIDTH
    g_raw = _dot_exact(slab_ref[0, :, o_low:o_low + LANES], wg_ref[:, d * gw:(d + 1) * gw]) + bg_ref[:, d * gw:(d + 1) * gw]
    g = jax.nn.log_sigmoid(g_raw) / GLA_GATE_NORM
    b = _dot_exact(tri, g)
    tot = _dot_exact(ones, g)
    e_b = jnp.exp(b)
    e_nb = jnp.exp(-b)
    e_end = jnp.exp(tot - b)
    decay = jnp.exp(tot)
    lane = lax.broadcasted_iota(jnp.int32, (n, pw), 1)
    chunk_order = range(n_chunks) if d == 0 else range(n_chunks - 1, -1, -1)

    for p in range(GLA_HEADS // 2):
        sl = slice(p * pw, (p + 1) * pw)
        q_in = slab_ref[0, :, sl] * e_b[:, sl] * (GLA_DK ** -0.5)
        k_p = slab_ref[0, :, GLA_K_WIDTH + p * pw:GLA_K_WIDTH + (p + 1) * pw]
        k_in = (k_p * e_nb[:, sl]).astype(BF16)
        k_end = (k_p * e_end[:, sl]).astype(BF16)
        for j in range(2):
            h = 2 * p + j
            q_h = jnp.where((lane < GLA_DK) if j == 0 else (lane >= GLA_DK), q_in, 0.0).astype(BF16)
            v_h = slab_ref[0, :, 2 * GLA_K_WIDTH + h * GLA_DV:2 * GLA_K_WIDTH + (h + 1) * GLA_DV].astype(BF16)
            a = jnp.where(causal, _dot_nt(q_h, k_in), 0.0)
            o_intra = _dot(a.astype(BF16), v_h)
            for c in chunk_order:
                rs = slice(c * GLA_CHUNK, (c + 1) * GLA_CHUNK)
                st = st_ref[d, h]
                o_ref[0, rs, h * GLA_DV:(h + 1) * GLA_DV] = o_intra[rs] + _dot_nt(q_h[rs], st.astype(BF16))
                st_ref[d, h] = st * decay[c * GLA_CHUNK:c * GLA_CHUNK + 1, sl] + _dot_tn(v_h[rs], k_end[rs])


def _gla_kernel(f_ref, b_ref, wg_ref, bg_ref, of_ref, ob_ref, st_ref):
    @pl.when(pl.program_id(1) == 0)
    def _():
        st_ref[...] = jnp.zeros_like(st_ref)

    _gla_direction(f_ref, wg_ref, bg_ref, st_ref, of_ref, 0)
    _gla_direction(b_ref, wg_ref, bg_ref, st_ref, ob_ref, 1)


def _gla_scan(p_gla, wg, bg):
    nb, t, _ = p_gla.shape
    n_tiles = t // TOK_TILE
    fwd = lambda b, s: (b, s, 0)
    bwd = lambda b, s: (b, jnp.where(s == 0, 0, n_tiles - s), 0)
    return pl.pallas_call(
        _gla_kernel,
        grid=(nb, n_tiles),
        in_specs=[
            pl.BlockSpec((1, TOK_TILE, GLA_COLS), fwd),
            pl.BlockSpec((1, TOK_TILE, GLA_COLS), bwd),
            pl.BlockSpec((LANES, 2 * GLA_K_WIDTH), lambda b, s: (0, 0)),
            pl.BlockSpec((1, 2 * GLA_K_WIDTH), lambda b, s: (0, 0)),
        ],
        out_specs=[
            pl.BlockSpec((1, TOK_TILE, GLA_V_WIDTH), fwd),
            pl.BlockSpec((1, TOK_TILE, GLA_V_WIDTH), bwd),
        ],
        out_shape=[jax.ShapeDtypeStruct((nb, t, GLA_V_WIDTH), F32)] * 2,
        scratch_shapes=[pltpu.VMEM((2, GLA_HEADS, GLA_DV, 2 * GLA_DK), F32)],
        compiler_params=_cparams("parallel", "arbitrary"),
        name="gla_scan",
    )(p_gla, p_gla, wg, bg)


def _merge_kernel(x_ref, mod_ref, oa_ref, ob_ref, gf_ref, gb_ref, og_ref, gate_ref, gn_ref, n2_ref,
                  wa_ref, wb_ref, wg_ref, wo_ref, x_out_ref, h_out_ref):
    og = og_ref[0]
    o_raw = gf_ref[0] + gb_ref[0]
    parts = []
    for h in range(GLA_HEADS):
        sl = slice(h * GLA_DV, (h + 1) * GLA_DV)
        parts.append(_rms(o_raw[:, sl], gn_ref[...]) * jax.nn.silu(og[:, sl]))
    o_g = jnp.concatenate(parts, axis=-1).astype(BF16)
    d = D_MODEL
    y = (jax.nn.sigmoid(gate_ref[0, :, 0:d]) * _dot(oa_ref[0], wa_ref[...])
         + jax.nn.sigmoid(gate_ref[0, :, d:2 * d]) * _dot(ob_ref[0], wb_ref[...])
         + jax.nn.sigmoid(gate_ref[0, :, 2 * d:3 * d]) * _dot(o_g, wg_ref[...]))
    x = x_ref[0] + mod_ref[0, 2:3, :] * _dot(y.astype(BF16), wo_ref[...])
    x_out_ref[0] = x
    h2 = _rms(x, n2_ref[...]) * (1.0 + mod_ref[0, 4:5, :]) + mod_ref[0, 3:4, :]
    h_out_ref[0] = h2.astype(BF16)


def _merge(xs, mod, o_a, o_b, gla_f, gla_b, p_gla, p_gate, gla_norm, norm2, wa, wb, wg, wo):
    nb, t, d = xs.shape
    tok = lambda b, i: (b, i, 0)
    const = lambda b, i: (0, 0)
    og_block = (2 * GLA_K_WIDTH) // GLA_V_WIDTH + 1
    return pl.pallas_call(
        _merge_kernel,
        grid=(nb, t // TOK_TILE),
        in_specs=[
            pl.BlockSpec((1, TOK_TILE, d), tok),
            pl.BlockSpec((1, N_MOD, d), _mod_row(nb)),
            pl.BlockSpec((1, TOK_TILE, MLA_WIDTH), tok),
            pl.BlockSpec((1, TOK_TILE, NA_WIDTH), tok),
            pl.BlockSpec((1, TOK_TILE, GLA_V_WIDTH), tok),
            pl.BlockSpec((1, TOK_TILE, GLA_V_WIDTH), tok),
            pl.BlockSpec((1, TOK_TILE, GLA_V_WIDTH), lambda b, i: (b, i, og_block)),
            pl.BlockSpec((1, TOK_TILE, GATE_COLS), tok),
            pl.BlockSpec((1, GLA_DV), const),
            pl.BlockSpec((1, d), const),
            pl.BlockSpec((MLA_WIDTH, d), const),
            pl.BlockSpec((NA_WIDTH, d), const),
            pl.BlockSpec((GLA_V_WIDTH, d), const),
            pl.BlockSpec((d, d), const),
        ],
        out_specs=[pl.BlockSpec((1, TOK_TILE, d), tok), pl.BlockSpec((1, TOK_TILE, d), tok)],
        out_shape=[jax.ShapeDtypeStruct((nb, t, d), F32), jax.ShapeDtypeStruct((nb, t, d), BF16)],
        compiler_params=_cparams("parallel", "parallel"),
        name="merge",
    )(xs, mod, o_a, o_b, gla_f, gla_b, p_gla, p_gate, gla_norm.reshape(1, -1), norm2.reshape(1, -1),
      wa, wb, wg, wo)


def _top_ranked(x, k, want_rank=True):
    row = lax.broadcasted_iota(jnp.int32, x.shape, 0).astype(F32)
    out_row = lax.broadcasted_iota(jnp.int32, (k, x.shape[1]), 0)
    vals = jnp.zeros((k, x.shape[1]), F32)
    rank = jnp.full(x.shape, float(k), F32)
    for r in range(k):
        m = jnp.max(x, axis=0, keepdims=True)
        vals = jnp.where(out_row == r, m, vals)
        hit = row == jnp.min(jnp.where(x == m, row, float(x.shape[0])), axis=0, keepdims=True)
        if want_rank:
            rank = jnp.where(hit, float(r), rank)
        x = jnp.where(hit, -jnp.inf, x)
    return vals, rank


def _count(mask):
    return jnp.sum(jnp.where(mask, 1.0, 0.0), axis=0, keepdims=True)


def _peer_select_kernel(h_ref, wqt_ref, kp_ref, ht_ref, n0_ref, e0_ref, r1_ref, e1_ref):
    h = h_ref[...]
    ht_ref[...] = h.T
    q_t = _dot_nt(wqt_ref[...], h)
    nk = PEER_N_KEYS
    k = PEER_TOPK
    sub = lax.broadcasted_iota(jnp.int32, (8, LANES), 0)
    for hd in range(PEER_HEADS):
        s = _dot(kp_ref[...], q_t[hd * PEER_QUERY_DIM:(hd + 1) * PEER_QUERY_DIM].astype(BF16))
        for c in range(h.shape[0] // LANES):
            cs = slice(c * LANES, (c + 1) * LANES)
            s0, s1 = s[:nk, cs], s[nk:, cs]
            v0, rank0 = _top_ranked(s0, k)
            v1, rank1 = _top_ranked(s1, k)
            cand = [v0[0:1] + v1]
            for a in range(1, k):
                cand.append(jnp.where(sub < k // (a + 1), v0[a:a + 1] + v1[:8], -jnp.inf))
            top, _ = _top_ranked(jnp.concatenate(cand, axis=0), k, want_rank=False)
            thr = top[k - 1:k]
            z = jnp.sum(jnp.exp(top - top[0:1]), axis=0, keepdims=True)
            above = [_count(ca > thr) for ca in cand]
            equal = [_count(ca == thr) for ca in cand]
            spare = float(k) - functools.reduce(jnp.add, above)
            n0 = jnp.zeros_like(s0)
            for a in range(k):
                take = jnp.clip(spare, 0.0, equal[a])
                spare = spare - equal[a]
                n0 = jnp.where(rank0 == float(a), above[a] + take, n0)
            n0_ref[hd, :, cs] = n0
            e0_ref[hd, :, cs] = jnp.exp(s0 - v0[0:1]) / z
            r1_ref[hd, :, cs] = pltpu.bitcast(rank1.astype(BF16), jnp.uint32)
            e1_ref[hd, :, cs] = pltpu.bitcast(jnp.exp(s1 - v1[0:1]).astype(BF16), jnp.uint32)


def _peer_select(h2, wq_t, key_pair):
    n, d = h2.shape
    tm = TOK_TILE
    fac = lambda dt: jax.ShapeDtypeStruct((PEER_HEADS, PEER_N_KEYS // (1 if dt == F32 else 2), n), dt)
    fac_spec = pl.BlockSpec((PEER_HEADS, PEER_N_KEYS, tm), lambda i: (0, 0, i))
    packed_spec = pl.BlockSpec((PEER_HEADS, PEER_N_KEYS // 2, tm), lambda i: (0, 0, i))
    return pl.pallas_call(
        _peer_select_kernel,
        grid=(n // tm,),
        in_specs=[
            pl.BlockSpec((tm, d), lambda i: (i, 0)),
            pl.BlockSpec(wq_t.shape, lambda i: (0, 0)),
            pl.BlockSpec(key_pair.shape, lambda i: (0, 0)),
        ],
        out_specs=[pl.BlockSpec((d, tm), lambda i: (0, i)), fac_spec, fac_spec, packed_spec, packed_spec],
        out_shape=[jax.ShapeDtypeStruct((d, n), BF16), fac(F32), fac(F32), fac(jnp.uint32), fac(jnp.uint32)],
        compiler_params=_cparams("parallel"),
        name="peer_select",
    )(h2, wq_t, key_pair)


def _peer_dense_kernel(ht_ref, u_ref, vt_ref, n0_ref, e0_ref, r1_ref, e1_ref, o_ref, g_ref, act_ref):
    nk = PEER_N_KEYS
    piece = PEER_PIECE
    n_pieces = PEER_EXP_TILE // piece
    tm = ht_ref.shape[1]

    @pl.when(pl.program_id(1) == 0)
    def _():
        o_ref[...] = jnp.zeros_like(o_ref)

    def gate_piece(p):
        rs_n, cs_n = 64, LANES
        zero = jnp.zeros((rs_n, cs_n), BF16)
        for sub in range(piece // nk):
            ii = p * (piece // nk) + sub
            for cs in range(0, tm, cs_n):
                for rs in range(0, nk, rs_n):
                    w = zero
                    for hd in range(PEER_HEADS):
                        n_sel = jnp.broadcast_to(n0_ref[hd, ii:ii + 1, cs:cs + cs_n], (rs_n, cs_n)).astype(BF16)
                        e0 = jnp.broadcast_to(e0_ref[hd, ii:ii + 1, cs:cs + cs_n], (rs_n, cs_n)).astype(BF16)
                        r1 = pltpu.bitcast(r1_ref[hd, rs // 2:(rs + rs_n) // 2, cs:cs + cs_n], BF16)
                        e1 = pltpu.bitcast(e1_ref[hd, rs // 2:(rs + rs_n) // 2, cs:cs + cs_n], BF16)
                        w = w + jnp.where(r1 < n_sel, e1 * e0, zero)
                    rows = slice(sub * nk + rs, sub * nk + rs + rs_n)
                    act = act_ref[p, rows, cs:cs + cs_n]
                    g_ref[p, rows, cs:cs + cs_n] = w * jax.nn.gelu(act).astype(BF16)

    for p in range(n_pieces):
        act_ref[p] = _dot(u_ref[p * piece:(p + 1) * piece, :], ht_ref[...])
    for p in range(n_pieces):
        gate_piece(p)
        o_ref[...] += _dot(vt_ref[:, p * piece:(p + 1) * piece], g_ref[p])


def _peer_dense(h_t, u, v_t, n0, e0, r1, e1):
    d, n = h_t.shape
    tm, te = PEER_TOK_TILE, PEER_EXP_TILE
    sub = te // PEER_N_KEYS
    return pl.pallas_call(
        _peer_dense_kernel,
        grid=(n // tm, PEER_N_EXPERTS // te),
        in_specs=[
            pl.BlockSpec((d, tm), lambda i, e: (0, i)),
            pl.BlockSpec((te, d), lambda i, e: (e, 0)),
            pl.BlockSpec((d, te), lambda i, e: (0, e)),
            pl.BlockSpec((PEER_HEADS, sub, tm), lambda i, e: (0, e, i)),
            pl.BlockSpec((PEER_HEADS, sub, tm), lambda i, e: (0, e, i)),
            pl.BlockSpec((PEER_HEADS, PEER_N_KEYS // 2, tm), lambda i, e: (0, 0, i)),
            pl.BlockSpec((PEER_HEADS, PEER_N_KEYS // 2, tm), lambda i, e: (0, 0, i)),
        ],
        out_specs=pl.BlockSpec((d, tm), lambda i, e: (0, i)),
        out_shape=jax.ShapeDtypeStruct((d, n), F32),
        scratch_shapes=[pltpu.VMEM((te // PEER_PIECE, PEER_PIECE, tm), BF16),
                        pltpu.VMEM((te // PEER_PIECE, PEER_PIECE, tm), F32)],
        compiler_params=_cparams("parallel", "arbitrary"),
        name="peer_dense",
    )(h_t, u, v_t, n0, e0, r1, e1)


def _peer_residual_kernel(x_ref, mod_ref, ot_ref, fn_ref, o_ref, *, final):
    x = x_ref[0] + mod_ref[0, 5:6, :] * ot_ref[...].T
    o_ref[0] = _rms(x, fn_ref[...]) if final else x


def _peer_residual(xs, mod, out_t, final_norm, final):
    nb, t, d = xs.shape
    n_tiles = t // TOK_TILE
    tok = lambda b, i: (b, i, 0)
    return pl.pallas_call(
        functools.partial(_peer_residual_kernel, final=final),
        grid=(nb, n_tiles),
        in_specs=[
            pl.BlockSpec((1, TOK_TILE, d), tok),
            pl.BlockSpec((1, N_MOD, d), _mod_row(nb)),
            pl.BlockSpec((d, TOK_TILE), lambda b, i: (0, b * n_tiles + i)),
            pl.BlockSpec((1, d), lambda b, i: (0, 0)),
        ],
        out_specs=pl.BlockSpec((1, TOK_TILE, d), tok),
        out_shape=jax.ShapeDtypeStruct((nb, t, d), F32),
        compiler_params=_cparams("parallel", "parallel"),
        name="peer_residual",
    )(xs, mod, out_t, final_norm.reshape(1, d))


def _rot_cols(w):
    q = MLA_ROPE // 4
    return jnp.concatenate([-w[..., q:2 * q], w[..., :q], -w[..., 3 * q:], w[..., 2 * q:3 * q]], axis=-1)


def _rope_slab(w):
    return jnp.pad(w, ((0, 0), (MLA_NOPE, LANES - MLA_NOPE - MLA_ROPE)))


def _prep_w_in(w_in):
    o = MLA_Q_LORA + MLA_KV_LORA
    w_kr = w_in[:, o:o + MLA_ROPE]
    o2 = o + MLA_ROPE
    n_mid = NA_COLS + 2 * GLA_K_WIDTH + 2 * GLA_V_WIDTH
    low = w_in[:, o2 + n_mid:o2 + n_mid + 2 * GLA_GATE_RANK]
    parts = [w_in[:, :o], _rope_slab(w_kr), _rope_slab(_rot_cols(w_kr)), w_in[:, o2:o2 + n_mid],
             jnp.pad(low, ((0, 0), (0, LANES - 2 * GLA_GATE_RANK))), w_in[:, o2 + n_mid + 2 * GLA_GATE_RANK:]]
    return jnp.concatenate(parts, axis=1).astype(BF16)


def _prep_mla(w_uq, w_ukv):
    r = w_uq.shape[0]
    wq = w_uq.reshape(r, MLA_HEADS, MLA_NOPE + MLA_ROPE)
    pad = MLA_HEAD_PAD - MLA_NOPE - MLA_ROPE
    wq_p = jnp.pad(wq, ((0, 0), (0, 0), (0, pad)))
    wq_rot = jnp.pad(_rot_cols(wq[..., MLA_NOPE:]), ((0, 0), (0, 0), (MLA_NOPE, pad)))
    c = w_ukv.shape[0]
    wkv = w_ukv.reshape(c, MLA_HEADS, MLA_NOPE + MLA_V)
    wk = jnp.pad(wkv[..., :MLA_NOPE], ((0, 0), (0, 0), (0, MLA_HEAD_PAD - MLA_NOPE)))
    wv = wkv[..., MLA_NOPE:]
    hw = MLA_HEADS * MLA_HEAD_PAD
    lane = np.arange(LANES)
    col = np.arange(hw)
    is_rope = (lane >= MLA_NOPE) & (lane < MLA_NOPE + MLA_ROPE)
    e = (is_rope[:, None] & (col[None, :] % MLA_HEAD_PAD == lane[:, None])).astype(np.float32)
    return (wq_p.reshape(r, hw).astype(BF16), wq_rot.reshape(r, hw).astype(BF16),
            wk.reshape(c, hw).astype(BF16), jnp.asarray(e, BF16), wv.reshape(c, MLA_WIDTH).astype(BF16))


def _rope_tables(seq):
    pos = jnp.arange(seq)
    rows = (pos // GRID_W).astype(F32)
    cols = (pos % GRID_W).astype(F32)
    half = MLA_ROPE // 2
    inv = ROPE_THETA ** (-jnp.arange(0, half, 2, dtype=F32) / half)
    ar = rows[:, None] * inv
    ac = cols[:, None] * inv
    ang = jnp.concatenate([ar, ar, ac, ac], axis=-1)
    ang = jnp.concatenate([jnp.zeros((CTX_LEN, MLA_ROPE), F32), ang], axis=0)
    t = ang.shape[0]
    ones = jnp.ones((t, MLA_NOPE), F32)
    zeros = jnp.zeros((t, LANES - MLA_NOPE - MLA_ROPE), F32)
    cos_t = jnp.concatenate([ones, jnp.cos(ang), zeros], axis=1)
    sin_t = jnp.concatenate([0.0 * ones, jnp.sin(ang), zeros], axis=1)
    return cos_t, sin_t


def _prep_gla_gate(w_f, b_f, w_b, b_b):
    wg = jnp.zeros((LANES, 2 * GLA_K_WIDTH), F32)
    wg = wg.at[:GLA_GATE_RANK, :GLA_K_WIDTH].set(w_f)
    wg = wg.at[GLA_GATE_RANK:2 * GLA_GATE_RANK, GLA_K_WIDTH:].set(w_b)
    return wg, jnp.concatenate([b_f, b_b]).reshape(1, -1)


def _prep_peer_keys(sub_keys):
    half = PEER_QUERY_DIM // 2
    kp = jnp.zeros((2 * PEER_N_KEYS, PEER_QUERY_DIM), F32)
    kp = kp.at[:PEER_N_KEYS, :half].set(sub_keys[0])
    kp = kp.at[PEER_N_KEYS:, half:].set(sub_keys[1])
    return kp.astype(BF16)


def kernel(x, c, ctx, c_ctx, w_ada, b_ada, norm1, w_in, mla_q_norm, mla_w_uq, mla_kv_norm, mla_w_ukv, na_rpb, gla_w_gk_fwd, gla_b_gk_fwd, gla_w_gk_bwd, gla_b_gk_bwd, gla_norm, w_o_mla, w_o_na, w_o_gla, w_out, norm2, peer_w_q, peer_sub_keys, peer_u, peer_v, final_norm):
    nb, seq, d = x.shape
    assert ctx.shape[1] == CTX_LEN == TOK_TILE and seq % (NA_ROWS_PER_TILE * GRID_W) == 0
    assert NA_ROWS_PER_TILE * GRID_W == TOK_TILE and nb < 8
    rows = seq // GRID_W
    depth = w_ada.shape[0]

    xs = jnp.concatenate([ctx, x], axis=1)
    t = xs.shape[1]
    cc = jnp.zeros((8, d), F32).at[:nb].set(c).at[nb].set(c_ctx)
    mods = _modulation(cc, w_ada, b_ada).reshape(depth, 8, N_MOD, d)
    cos_t, sin_t = _rope_tables(seq)

    for l in range(depth):
        mod = mods[l]
        p_mla, p_na, p_gla, p_gate = _in_projection(xs, mod, norm1[l], _prep_w_in(w_in[l]))
        q, k, v = _mla_prep(p_mla, cos_t, sin_t, mla_q_norm[l], mla_kv_norm[l], *_prep_mla(mla_w_uq[l], mla_w_ukv[l]))
        o_a = _mla_attention(q, k, v)
        o_b = _na_attention(p_na, _na_bias_table(na_rpb[l], rows), rows)
        gla_f, gla_b = _gla_scan(p_gla, *_prep_gla_gate(gla_w_gk_fwd[l], gla_b_gk_fwd[l], gla_w_gk_bwd[l], gla_b_gk_bwd[l]))
        xs, h2 = _merge(xs, mod, o_a, o_b, gla_f, gla_b, p_gla, p_gate, gla_norm[l], norm2[l],
                        w_o_mla[l].astype(BF16), w_o_na[l].astype(BF16), w_o_gla[l].astype(BF16), w_out[l].astype(BF16))
        h_t, n0, e0, r1, e1 = _peer_select(h2.reshape(nb * t, d), peer_w_q[l].T.astype(BF16),
                                           _prep_peer_keys(peer_sub_keys[l]))
        out_t = _peer_dense(h_t, peer_u[l].astype(BF16), peer_v[l].T.astype(BF16), n0, e0, r1, e1)
        xs = _peer_residual(xs, mod, out_t, final_norm, final=(l == depth - 1))
    return xs[:, CTX_LEN:, :]
```

```python
import functools

import numpy as np
import jax
import jax.numpy as jnp
from jax import lax
from jax.experimental import pallas as pl
from jax.experimental.pallas import tpu as pltpu

F32 = jnp.float32
BF16 = jnp.bfloat16
HIGHEST = lax.Precision.HIGHEST

D_MODEL = 1024
DEPTH = 2
CTX_LEN = 256
GRID_W = 64
EPS = 1e-6
N_MOD = 6
NEG_INF = -1e30

MLA_HEADS = 8
MLA_Q_LORA = 384
MLA_KV_LORA = 256
MLA_NOPE = 64
MLA_ROPE = 32
MLA_V = 64
ROPE_THETA = 10000.0
MLA_HEAD_PAD = 128

NA_HEADS = 8
NA_HEAD_DIM = 64
NA_KH = 8
NA_KW = 16
NA_ROWS_PER_TILE = 4
NA_SPAN = NA_ROWS_PER_TILE + NA_KH

GLA_HEADS = 4
GLA_DK = 64
GLA_DV = 128
GLA_GATE_RANK = 16
GLA_GATE_NORM = 16.0
GLA_CHUNK = 64

PEER_HEADS = 8
PEER_N_KEYS = 128
PEER_N_EXPERTS = PEER_N_KEYS * PEER_N_KEYS
PEER_QUERY_DIM = 128
PEER_TOPK = 16
PEER_TOK_TILE = 512
PEER_EXP_TILE = 1024
PEER_PIECE = 256

MLA_WIDTH = MLA_HEADS * MLA_V
NA_WIDTH = NA_HEADS * NA_HEAD_DIM
GLA_K_WIDTH = GLA_HEADS * GLA_DK
GLA_V_WIDTH = GLA_HEADS * GLA_DV

LANES = 128
TOK_TILE = 256

MLA_COLS = MLA_Q_LORA + MLA_KV_LORA + 2 * LANES
NA_COLS = 3 * NA_WIDTH
GLA_COLS = 2 * GLA_K_WIDTH + 2 * GLA_V_WIDTH + LANES
GATE_COLS = 3 * D_MODEL
IN_COLS_PAD = MLA_COLS + NA_COLS + GLA_COLS + GATE_COLS

VMEM_LIMIT = 56 * 1024 * 1024


def _cparams(*sem, flags=None):
    return pltpu.CompilerParams(dimension_semantics=sem, vmem_limit_bytes=VMEM_LIMIT, flags=flags)


def _rms(x, w):
    return x * lax.rsqrt(jnp.mean(x * x, axis=-1, keepdims=True) + EPS) * w


def _dot(a, b):
    return jnp.dot(a, b, preferred_element_type=F32)


def _dot_nt(a, b):
    return lax.dot_general(a, b, (((1,), (1,)), ((), ())), preferred_element_type=F32)


def _dot_tn(a, b):
    return lax.dot_general(a, b, (((0,), (0,)), ((), ())), preferred_element_type=F32)


def _dot_exact(a, b):
    return jnp.dot(a, b, preferred_element_type=F32, precision=HIGHEST)


def _mod_kernel(c_ref, w_ref, b_ref, o_ref):
    c = c_ref[...]
    o_ref[0] = _dot_exact(c * jax.nn.sigmoid(c), w_ref[0]) + b_ref[0]


def _modulation(cc, w_ada, b_ada):
    depth, d, n = w_ada.shape
    tn = 1536
    return pl.pallas_call(
        _mod_kernel,
        grid=(depth, n // tn),
        in_specs=[
            pl.BlockSpec((8, d), lambda l, j: (0, 0)),
            pl.BlockSpec((1, d, tn), lambda l, j: (l, 0, j)),
            pl.BlockSpec((1, 1, tn), lambda l, j: (l, 0, j)),
        ],
        out_specs=pl.BlockSpec((1, 8, tn), lambda l, j: (l, 0, j)),
        out_shape=jax.ShapeDtypeStruct((depth, 8, n), F32),
        compiler_params=_cparams("parallel", "parallel"),
        name="adaln_mod",
    )(cc, w_ada, b_ada.reshape(depth, 1, n))


def _mod_row(n_batch, has_ctx=True):
    if not has_ctx:
        return lambda b, i: (b, 0, 0)
    return lambda b, i: (jnp.where(i == 0, n_batch, b), 0, 0)


def _inproj_kernel(x_ref, mod_ref, nw_ref, w_ref, mla_ref, na_ref, gla_ref, gate_ref):
    h = _rms(x_ref[0], nw_ref[...]) * (1.0 + mod_ref[0, 1:2, :]) + mod_ref[0, 0:1, :]
    h = h.astype(BF16)
    o = 0
    mla_ref[0] = _dot(h, w_ref[:, o:o + MLA_COLS])
    o += MLA_COLS
    na_ref[0] = _dot(h, w_ref[:, o:o + NA_COLS]).astype(BF16)
    o += NA_COLS
    gla_ref[0] = _dot(h, w_ref[:, o:o + GLA_COLS])
    o += GLA_COLS
    gate_ref[0] = _dot(h, w_ref[:, o:o + GATE_COLS])


def _in_projection(xs, mod, norm_w, w_in_p):
    nb, t, d = xs.shape
    tok = lambda b, i: (b, i, 0)
    return pl.pallas_call(
        _inproj_kernel,
        grid=(nb, t // TOK_TILE),
        in_specs=[
            pl.BlockSpec((1, TOK_TILE, d), tok),
            pl.BlockSpec((1, N_MOD, d), _mod_row(nb)),
            pl.BlockSpec((1, d), lambda b, i: (0, 0)),
            pl.BlockSpec((d, IN_COLS_PAD), lambda b, i: (0, 0), pipeline_mode=pl.Buffered(1)),
        ],
        out_specs=[
            pl.BlockSpec((1, TOK_TILE, MLA_COLS), tok),
            pl.BlockSpec((1, TOK_TILE, NA_COLS), tok),
            pl.BlockSpec((1, TOK_TILE, GLA_COLS), tok),
            pl.BlockSpec((1, TOK_TILE, GATE_COLS), tok),
        ],
        out_shape=[
            jax.ShapeDtypeStruct((nb, t, MLA_COLS), F32),
            jax.ShapeDtypeStruct((nb, t, NA_COLS), BF16),
            jax.ShapeDtypeStruct((nb, t, GLA_COLS), F32),
            jax.ShapeDtypeStruct((nb, t, GATE_COLS), F32),
        ],
        compiler_params=_cparams("parallel", "parallel"),
        name="in_proj",
    )(xs, mod, norm_w.reshape(1, d), w_in_p)


def _mla_prep_kernel(p_ref, cos_ref, sin_ref, qn_ref, kvn_ref, wq_ref, wqr_ref, wk_ref, e_ref, wv_ref,
                     q_ref, k_ref, v_ref):
    scale = (MLA_NOPE + MLA_ROPE) ** -0.5
    cos = cos_ref[...]
    sin = sin_ref[...]
    o = MLA_Q_LORA + MLA_KV_LORA
    nq = _rms(p_ref[0, :, :MLA_Q_LORA], qn_ref[...]).astype(BF16)
    nkv = _rms(p_ref[0, :, MLA_Q_LORA:o], kvn_ref[...]).astype(BF16)
    q = _dot(nq, wq_ref[...])
    q_rot = _dot(nq, wqr_ref[...])
    for h in range(MLA_HEADS):
        sl = slice(h * MLA_HEAD_PAD, (h + 1) * MLA_HEAD_PAD)
        q_ref[0, :, sl] = ((q[:, sl] * cos + q_rot[:, sl] * sin) * scale).astype(BF16)
    k_rope = (p_ref[0, :, o:o + LANES] * cos + p_ref[0, :, o + LANES:o + 2 * LANES] * sin).astype(BF16)
    k_ref[0] = (_dot(nkv, wk_ref[...]) + _dot(k_rope, e_ref[...])).astype(BF16)
    v_ref[0] = _dot(nkv, wv_ref[...]).astype(BF16)


def _mla_prep(p_mla, cos_t, sin_t, q_norm, kv_norm, wq, wq_rot, wk, e_rope, wv):
    nb, t, _ = p_mla.shape
    hw = MLA_HEADS * MLA_HEAD_PAD
    tok = lambda b, i: (b, i, 0)
    const = lambda b, i: (0, 0)
    return pl.pallas_call(
        _mla_prep_kernel,
        grid=(nb, t // TOK_TILE),
        in_specs=[
            pl.BlockSpec((1, TOK_TILE, MLA_COLS), tok),
            pl.BlockSpec((TOK_TILE, LANES), lambda b, i: (i, 0)),
            pl.BlockSpec((TOK_TILE, LANES), lambda b, i: (i, 0)),
            pl.BlockSpec((1, MLA_Q_LORA), const),
            pl.BlockSpec((1, MLA_KV_LORA), const),
            pl.BlockSpec((MLA_Q_LORA, hw), const),
            pl.BlockSpec((MLA_Q_LORA, hw), const),
            pl.BlockSpec((MLA_KV_LORA, hw), const),
            pl.BlockSpec((LANES, hw), const),
            pl.BlockSpec((MLA_KV_LORA, MLA_WIDTH), const),
        ],
        out_specs=[
            pl.BlockSpec((1, TOK_TILE, hw), tok),
            pl.BlockSpec((1, TOK_TILE, hw), tok),
            pl.BlockSpec((1, TOK_TILE, MLA_WIDTH), tok),
        ],
        out_shape=[
            jax.ShapeDtypeStruct((nb, t, hw), BF16),
            jax.ShapeDtypeStruct((nb, t, hw), BF16),
            jax.ShapeDtypeStruct((nb, t, MLA_WIDTH), BF16),
        ],
        compiler_params=_cparams("parallel", "parallel"),
        name="mla_prep",
    )(p_mla, cos_t, sin_t, q_norm.reshape(1, -1), kv_norm.reshape(1, -1), wq, wq_rot, wk, e_rope, wv)


def _softmax_pv(s_list, v_list):
    m = s_list[0].max(axis=-1, keepdims=True)
    for s in s_list[1:]:
        m = jnp.maximum(m, s.max(axis=-1, keepdims=True))
    acc = None
    den = None
    for s, v in zip(s_list, v_list):
        p = jnp.exp(s - m)
        l = p.sum(axis=-1, keepdims=True)
        o = _dot(p.astype(BF16), v)
        acc = o if acc is None else acc + o
        den = l if den is None else den + l
    return acc / den


def _pair_select(o0, o1, half):
    lane = lax.broadcasted_iota(jnp.int32, o0.shape, 1)
    return jnp.where(lane < half, o0, o1)


def _mla_attn_kernel(q_ref, k_ref, v_ref, o_ref):
    def attend(nk):
        outs = []
        for j in range(2):
            sl = slice(j * MLA_HEAD_PAD, (j + 1) * MLA_HEAD_PAD)
            s = _dot_nt(q_ref[0, :, sl], k_ref[0, :nk, sl])
            outs.append(_softmax_pv([s], [v_ref[0, :nk, :]]))
        o_ref[0] = _pair_select(outs[0], outs[1], MLA_V).astype(BF16)

    is_ctx = pl.program_id(2) == 0

    @pl.when(is_ctx)
    def _():
        attend(CTX_LEN)

    @pl.when(jnp.logical_not(is_ctx))
    def _():
        attend(k_ref.shape[1])


def _mla_attention(q, k, v):
    nb, t, _ = q.shape
    pw = 2 * MLA_HEAD_PAD
    return pl.pallas_call(
        _mla_attn_kernel,
        grid=(nb, MLA_HEADS // 2, t // TOK_TILE),
        in_specs=[
            pl.BlockSpec((1, TOK_TILE, pw), lambda b, p, i: (b, i, p)),
            pl.BlockSpec((1, t, pw), lambda b, p, i: (b, 0, p)),
            pl.BlockSpec((1, t, 2 * MLA_V), lambda b, p, i: (b, 0, p)),
        ],
        out_specs=pl.BlockSpec((1, TOK_TILE, 2 * MLA_V), lambda b, p, i: (b, i, p)),
        out_shape=jax.ShapeDtypeStruct((nb, t, MLA_WIDTH), BF16),
        compiler_params=_cparams("parallel", "parallel", "arbitrary"),
        name="mla_attn",
    )(q, k, v)


def _na_tile_geometry(rows):
    n_tiles = rows // NA_ROWS_PER_TILE
    r0 = np.arange(n_tiles) * NA_ROWS_PER_TILE
    kb = np.clip(r0 - NA_KH // 2, 0, rows - NA_SPAN)
    pat = np.where(r0 == 0, 0, np.where(r0 == rows - NA_ROWS_PER_TILE, 2, 1))
    return kb, pat


def _na_bias_index(rows, tile):
    kh = min(NA_KH, rows)
    kb, _ = _na_tile_geometry(rows)
    r = tile * NA_ROWS_PER_TILE + np.arange(NA_ROWS_PER_TILE)
    start = np.clip(r - kh // 2, 0, rows - kh)
    kr = kb[tile] + np.arange(NA_SPAN)
    row_ok = (kr[None, :] >= start[:, None]) & (kr[None, :] < start[:, None] + kh)
    dr = kr[None, :] - r[:, None] + (NA_KH - 1)
    w = np.arange(GRID_W)
    col_start = np.clip(w - NA_KW // 2, 0, GRID_W - NA_KW)
    col_in = (w[None, :] >= col_start[:, None]) & (w[None, :] < col_start[:, None] + NA_KW)
    dc = np.clip(w[None, :] - w[:, None], -(NA_KW - 1), NA_KW - 1) + (NA_KW - 1)
    valid = row_ok[:, None, :, None] & col_in[None, :, None, :]
    dr = np.broadcast_to(np.clip(dr, 0, 2 * NA_KH - 2)[:, None, :, None], valid.shape)
    dc = np.broadcast_to(dc[None, :, None, :], valid.shape)
    n = NA_ROWS_PER_TILE * GRID_W
    return dr.reshape(n, -1), dc.reshape(n, -1), valid.reshape(n, -1)


def _na_bias_table(rpb, rows):
    nh, nd, _ = rpb.shape
    w = GRID_W
    kh = min(NA_KH, rows)
    edge = w - NA_KW
    r_ext = jnp.concatenate([jnp.broadcast_to(rpb[..., :1], (nh, nd, edge)), rpb,
                             jnp.broadcast_to(rpb[..., -1:], (nh, nd, edge)), jnp.zeros((nh, nd, 1), rpb.dtype)], axis=-1)
    toe = jnp.tile(r_ext, (1, 1, w))[..., :w * (2 * w - 1)].reshape(nh, nd, w, 2 * w - 1)[..., w - 1:]
    col = np.arange(w)
    col_start = np.clip(col - NA_KW // 2, 0, w - NA_KW)
    col_in = (col[None, :] >= col_start[:, None]) & (col[None, :] < col_start[:, None] + NA_KW)
    toe = jnp.where(col_in[None, None], toe, NEG_INF).astype(F32)
    masked = jnp.full((nh, w, w), NEG_INF, F32)
    kb, _ = _na_tile_geometry(rows)
    n_tiles = rows // NA_ROWS_PER_TILE
    pats = []
    for tile in (0, 1, n_tiles - 1):
        q_rows = []
        for dq in range(NA_ROWS_PER_TILE):
            r = tile * NA_ROWS_PER_TILE + dq
            start = int(np.clip(r - kh // 2, 0, rows - kh))
            blocks = []
            for dk in range(NA_SPAN):
                kr = int(kb[tile]) + dk
                blocks.append(toe[:, kr - r + NA_KH - 1] if start <= kr < start + kh else masked)
            q_rows.append(jnp.concatenate(blocks, axis=-1))
        pats.append(jnp.concatenate(q_rows, axis=1))
    return jnp.stack(pats, axis=1)


def _na_attn_kernel(kb_ref, pat_ref, q_ref, k_ref, v_ref, bias_ref, o_ref):
    i = pl.program_id(2)
    scale = NA_HEAD_DIM ** -0.5
    lane = lax.broadcasted_iota(jnp.int32, (TOK_TILE, 2 * NA_HEAD_DIM), 1)

    def head_q(j):
        q = q_ref[0].astype(F32)
        return jnp.where((lane < NA_HEAD_DIM) if j == 0 else (lane >= NA_HEAD_DIM), q, 0.0).astype(BF16)

    @pl.when(i == 0)
    def _():
        outs = []
        for j in range(2):
            s = _dot_nt(head_q(j), k_ref[0, :CTX_LEN, :]) * scale
            outs.append(_softmax_pv([s], [v_ref[0, :CTX_LEN, :]]))
        o_ref[0] = _pair_select(outs[0], outs[1], NA_HEAD_DIM).astype(BF16)

    @pl.when(i > 0)
    def _():
        start = pl.multiple_of(CTX_LEN + kb_ref[i - 1] * GRID_W, GRID_W)
        nk = NA_SPAN * GRID_W
        k_lat = k_ref[0, pl.ds(start, nk), :]
        v_lat = v_ref[0, pl.ds(start, nk), :]
        pat = pat_ref[i - 1]
        outs = []
        for j in range(2):
            qj = head_q(j)
            s_lat = _dot_nt(qj, k_lat) * scale + bias_ref[j, pat]
            s_ctx = _dot_nt(qj, k_ref[0, :CTX_LEN, :]) * scale
            outs.append(_softmax_pv([s_lat, s_ctx], [v_lat, v_ref[0, :CTX_LEN, :]]))
        o_ref[0] = _pair_select(outs[0], outs[1], NA_HEAD_DIM).astype(BF16)


def _na_attention(p_na, bias, rows):
    nb, t, _ = p_na.shape
    pw = 2 * NA_HEAD_DIM
    n_pairs = NA_HEADS // 2
    kb, pat = _na_tile_geometry(rows)
    nkeys = NA_SPAN * GRID_W
    grid_spec = pltpu.PrefetchScalarGridSpec(
        num_scalar_prefetch=2,
        grid=(n_pairs, nb, t // TOK_TILE),
        in_specs=[
            pl.BlockSpec((1, TOK_TILE, pw), lambda p, b, i, *_: (b, i, p)),
            pl.BlockSpec((1, t, pw), lambda p, b, i, *_: (b, 0, n_pairs + p)),
            pl.BlockSpec((1, t, pw), lambda p, b, i, *_: (b, 0, 2 * n_pairs + p)),
            pl.BlockSpec((2, 3, TOK_TILE, nkeys), lambda p, b, i, *_: (p, 0, 0, 0)),
        ],
        out_specs=pl.BlockSpec((1, TOK_TILE, pw), lambda p, b, i, *_: (b, i, p)),
    )
    return pl.pallas_call(
        _na_attn_kernel,
        grid_spec=grid_spec,
        out_shape=jax.ShapeDtypeStruct((nb, t, NA_WIDTH), BF16),
        compiler_params=_cparams("parallel", "parallel", "arbitrary"),
        name="na_attn",
    )(jnp.asarray(kb, jnp.int32), jnp.asarray(pat, jnp.int32), p_na, p_na, p_na, bias)


def _gla_direction(slab_ref, wg_ref, bg_ref, st_ref, o_ref, d):
    n = TOK_TILE
    n_chunks = n // GLA_CHUNK
    pw = 2 * GLA_DK
    row = lax.broadcasted_iota(jnp.int32, (n, n), 0)
    col = lax.broadcasted_iota(jnp.int32, (n, n), 1)
    shift = GLA_CHUNK.bit_length() - 1
    same_chunk = jnp.right_shift(row, shift) == jnp.right_shift(col, shift)
    causal = same_chunk & ((col <= row) if d == 0 else (col >= row))
    tri = jnp.where(causal, 1.0, 0.0).astype(F32)
    ones = jnp.where(same_chunk, 1.0, 0.0).astype(F32)

    o_low = 2 * GLA_K_WIDTH + 2 * GLA_V_WIDTH
    gw = GLA_K_WIDTH
    g_raw = _dot_exact(slab_ref[0, :, o_low:o_low + LANES], wg_ref[:, d * gw:(d + 1) * gw]) + bg_ref[:, d * gw:(d + 1) * gw]
    g = jax.nn.log_sigmoid(g_raw) / GLA_GATE_NORM
    b = _dot_exact(tri, g)
    tot = _dot_exact(ones, g)
    e_b = jnp.exp(b)
    e_nb = jnp.exp(-b)
    e_end = jnp.exp(tot - b)
    decay = jnp.exp(tot)
    lane = lax.broadcasted_iota(jnp.int32, (n, pw), 1)
    chunk_order = range(n_chunks) if d == 0 else range(n_chunks - 1, -1, -1)

    for p in range(GLA_HEADS // 2):
        sl = slice(p * pw, (p + 1) * pw)
        q_in = slab_ref[0, :, sl] * e_b[:, sl] * (GLA_DK ** -0.5)
        k_p = slab_ref[0, :, GLA_K_WIDTH + p * pw:GLA_K_WIDTH + (p + 1) * pw]
        k_in = (k_p * e_nb[:, sl]).astype(BF16)
        k_end = (k_p * e_end[:, sl]).astype(BF16)
        for j in range(2):
            h = 2 * p + j
            q_h = jnp.where((lane < GLA_DK) if j == 0 else (lane >= GLA_DK), q_in, 0.0).astype(BF16)
            v_h = slab_ref[0, :, 2 * GLA_K_WIDTH + h * GLA_DV:2 * GLA_K_WIDTH + (h + 1) * GLA_DV].astype(BF16)
            a = jnp.where(causal, _dot_nt(q_h, k_in), 0.0)
            o_intra = _dot(a.astype(BF16), v_h)
            for c in chunk_order:
                rs = slice(c * GLA_CHUNK, (c + 1) * GLA_CHUNK)
                st = st_ref[d, h]
                o_ref[0, rs, h * GLA_DV:(h + 1) * GLA_DV] = o_intra[rs] + _dot_nt(q_h[rs], st.astype(BF16))
                st_ref[d, h] = st * decay[c * GLA_CHUNK:c * GLA_CHUNK + 1, sl] + _dot_tn(v_h[rs], k_end[rs])


def _gla_kernel(f_ref, b_ref, wg_ref, bg_ref, of_ref, ob_ref, st_ref):
    @pl.when(pl.program_id(1) == 0)
    def _():
        st_ref[...] = jnp.zeros_like(st_ref)

    _gla_direction(f_ref, wg_ref, bg_ref, st_ref, of_ref, 0)
    _gla_direction(b_ref, wg_ref, bg_ref, st_ref, ob_ref, 1)


def _gla_scan(p_gla, wg, bg):
    nb, t, _ = p_gla.shape
    n_tiles = t // TOK_TILE
    fwd = lambda b, s: (b, s, 0)
    bwd = lambda b, s: (b, jnp.where(s == 0, 0, n_tiles - s), 0)
    return pl.pallas_call(
        _gla_kernel,
        grid=(nb, n_tiles),
        in_specs=[
            pl.BlockSpec((1, TOK_TILE, GLA_COLS), fwd),
            pl.BlockSpec((1, TOK_TILE, GLA_COLS), bwd),
            pl.BlockSpec((LANES, 2 * GLA_K_WIDTH), lambda b, s: (0, 0)),
            pl.BlockSpec((1, 2 * GLA_K_WIDTH), lambda b, s: (0, 0)),
        ],
        out_specs=[
            pl.BlockSpec((1, TOK_TILE, GLA_V_WIDTH), fwd),
            pl.BlockSpec((1, TOK_TILE, GLA_V_WIDTH), bwd),
        ],
        out_shape=[jax.ShapeDtypeStruct((nb, t, GLA_V_WIDTH), F32)] * 2,
        scratch_shapes=[pltpu.VMEM((2, GLA_HEADS, GLA_DV, 2 * GLA_DK), F32)],
        compiler_params=_cparams("parallel", "arbitrary"),
        name="gla_scan",
    )(p_gla, p_gla, wg, bg)


def _merge_kernel(x_ref, mod_ref, oa_ref, ob_ref, gf_ref, gb_ref, og_ref, gate_ref, gn_ref, n2_ref,
                  wa_ref, wb_ref, wg_ref, wo_ref, x_out_ref, h_out_ref):
    og = og_ref[0]
    o_raw = gf_ref[0] + gb_ref[0]
    parts = []
    for h in range(GLA_HEADS):
        sl = slice(h * GLA_DV, (h + 1) * GLA_DV)
        parts.append(_rms(o_raw[:, sl], gn_ref[...]) * jax.nn.silu(og[:, sl]))
    o_g = jnp.concatenate(parts, axis=-1).astype(BF16)
    d = D_MODEL
    y = (jax.nn.sigmoid(gate_ref[0, :, 0:d]) * _dot(oa_ref[0], wa_ref[...])
         + jax.nn.sigmoid(gate_ref[0, :, d:2 * d]) * _dot(ob_ref[0], wb_ref[...])
         + jax.nn.sigmoid(gate_ref[0, :, 2 * d:3 * d]) * _dot(o_g, wg_ref[...]))
    x = x_ref[0] + mod_ref[0, 2:3, :] * _dot(y.astype(BF16), wo_ref[...])
    x_out_ref[0] = x
    h2 = _rms(x, n2_ref[...]) * (1.0 + mod_ref[0, 4:5, :]) + mod_ref[0, 3:4, :]
    h_out_ref[0] = h2.astype(BF16)


def _merge(xs, mod, o_a, o_b, gla_f, gla_b, p_gla, p_gate, gla_norm, norm2, wa, wb, wg, wo, keep_ctx):
    nb, t, d = xs.shape
    skip = 0 if keep_ctx else 1
    t_out = t - skip * TOK_TILE
    src = lambda b, i: (b, i + skip, 0)
    tok = lambda b, i: (b, i, 0)
    const = lambda b, i: (0, 0)
    og_block = (2 * GLA_K_WIDTH) // GLA_V_WIDTH + 1
    return pl.pallas_call(
        _merge_kernel,
        grid=(nb, t_out // TOK_TILE),
        in_specs=[
            pl.BlockSpec((1, TOK_TILE, d), src),
            pl.BlockSpec((1, N_MOD, d), _mod_row(nb, keep_ctx)),
            pl.BlockSpec((1, TOK_TILE, MLA_WIDTH), src),
            pl.BlockSpec((1, TOK_TILE, NA_WIDTH), src),
            pl.BlockSpec((1, TOK_TILE, GLA_V_WIDTH), src),
            pl.BlockSpec((1, TOK_TILE, GLA_V_WIDTH), src),
            pl.BlockSpec((1, TOK_TILE, GLA_V_WIDTH), lambda b, i: (b, i + skip, og_block)),
            pl.BlockSpec((1, TOK_TILE, GATE_COLS), src),
            pl.BlockSpec((1, GLA_DV), const),
            pl.BlockSpec((1, d), const),
            pl.BlockSpec((MLA_WIDTH, d), const),
            pl.BlockSpec((NA_WIDTH, d), const),
            pl.BlockSpec((GLA_V_WIDTH, d), const),
            pl.BlockSpec((d, d), const),
        ],
        out_specs=[pl.BlockSpec((1, TOK_TILE, d), tok), pl.BlockSpec((1, TOK_TILE, d), tok)],
        out_shape=[jax.ShapeDtypeStruct((nb, t_out, d), F32), jax.ShapeDtypeStruct((nb, t_out, d), BF16)],
        compiler_params=_cparams("parallel", "parallel"),
        name="merge",
    )(xs, mod, o_a, o_b, gla_f, gla_b, p_gla, p_gate, gla_norm.reshape(1, -1), norm2.reshape(1, -1),
      wa, wb, wg, wo)


def _top_ranked(x, k, want_rank=True):
    row = lax.broadcasted_iota(jnp.int32, x.shape, 0).astype(F32)
    out_row = lax.broadcasted_iota(jnp.int32, (k, x.shape[1]), 0)
    vals = jnp.zeros((k, x.shape[1]), F32)
    rank = jnp.full(x.shape, float(k), F32)
    for r in range(k):
        m = jnp.max(x, axis=0, keepdims=True)
        vals = jnp.where(out_row == r, m, vals)
        hit = row == jnp.min(jnp.where(x == m, row, float(x.shape[0])), axis=0, keepdims=True)
        if want_rank:
            rank = jnp.where(hit, float(r), rank)
        x = jnp.where(hit, -jnp.inf, x)
    return vals, rank


def _count(mask):
    return jnp.sum(jnp.where(mask, 1.0, 0.0), axis=0, keepdims=True)


def _top_distinct(x, k, want_rank=True):
    out_row = lax.broadcasted_iota(jnp.int32, (k, x.shape[1]), 0)
    vals = jnp.zeros((k, x.shape[1]), F32)
    rank = jnp.full(x.shape, float(k), F32)
    for r in range(k):
        m = jnp.max(x, axis=0, keepdims=True)
        vals = jnp.where(out_row == r, m, vals)
        hit = x == m
        if want_rank:
            rank = jnp.where(hit, float(r), rank)
        x = jnp.where(hit, -jnp.inf, x)
    return vals, rank, x


def _peer_select_chunk(s0, s1, exact):
    k = PEER_TOPK
    sub = lax.broadcasted_iota(jnp.int32, (8, LANES), 0)
    if exact:
        v0, rank0 = _top_ranked(s0, k)
        v1, rank1 = _top_ranked(s1, k)
    else:
        v0, rank0, rest0 = _top_distinct(s0, k)
        v1, rank1, rest1 = _top_distinct(s1, k)
    cand = [v0[0:1] + v1]
    for a in range(1, k):
        cand.append(jnp.where(sub < k // (a + 1), v0[a:a + 1] + v1[:8], -jnp.inf))
    cand_all = jnp.concatenate(cand, axis=0)
    if exact:
        top, _ = _top_ranked(cand_all, k, want_rank=False)
    else:
        top, _, rest_c = _top_distinct(cand_all, k, want_rank=False)
    thr = top[k - 1:k]
    z = jnp.sum(jnp.exp(top - top[0:1]), axis=0, keepdims=True)
    if exact:
        above = [_count(ca > thr) for ca in cand]
        equal = [_count(ca == thr) for ca in cand]
        spare = float(k) - functools.reduce(jnp.add, above)
        n_sel = []
        for a in range(k):
            n_sel.append(above[a] + jnp.clip(spare, 0.0, equal[a]))
            spare = spare - equal[a]
        redo = None
    else:
        n_sel = [_count(ca >= thr) for ca in cand]
        n_pad = cand_all.shape[0] - sum(k // (a + 1) for a in range(k))
        redo = jnp.where((_count(rest0 == -jnp.inf) != float(k)) | (_count(rest1 == -jnp.inf) != float(k))
                         | (_count(rest_c == -jnp.inf) != float(n_pad + k)), 1.0, 0.0)
    n0 = jnp.zeros_like(s0)
    for a in range(k):
        n0 = jnp.where(rank0 == float(a), n_sel[a], n0)
    e0 = jnp.exp(s0 - v0[0:1]) / z
    r1 = pltpu.bitcast(rank1.astype(BF16), jnp.uint32)
    e1 = pltpu.bitcast(jnp.exp(s1 - v1[0:1]).astype(BF16), jnp.uint32)
    return (n0, e0, r1, e1), redo


def _peer_select_kernel(h_ref, wqt_ref, kp_ref, ht_ref, n0_ref, e0_ref, r1_ref, e1_ref, s_ref):
    h = h_ref[...]
    ht_ref[...] = h.T
    q_t = _dot_nt(wqt_ref[...], h)
    nk = PEER_N_KEYS
    for hd in range(PEER_HEADS):
        s_ref[hd] = _dot(kp_ref[...], q_t[hd * PEER_QUERY_DIM:(hd + 1) * PEER_QUERY_DIM].astype(BF16))

    def run(exact):
        redo = jnp.zeros((1, LANES), F32)
        for hd in range(PEER_HEADS):
            for c in range(h.shape[0] // LANES):
                cs = slice(c * LANES, (c + 1) * LANES)
                outs, flag = _peer_select_chunk(s_ref[hd, :nk, cs], s_ref[hd, nk:, cs], exact)
                for ref, val in zip((n0_ref, e0_ref, r1_ref, e1_ref), outs):
                    ref[hd, :, cs] = val
                if flag is not None:
                    redo = jnp.maximum(redo, flag)
        return redo

    redo = run(exact=False)

    @pl.when(jnp.max(redo) > 0.0)
    def _():
        run(exact=True)


def _peer_select(h2, wq_t, key_pair):
    n, d = h2.shape
    tm = TOK_TILE
    fac = lambda dt: jax.ShapeDtypeStruct((PEER_HEADS, PEER_N_KEYS // (1 if dt == F32 else 2), n), dt)
    fac_spec = pl.BlockSpec((PEER_HEADS, PEER_N_KEYS, tm), lambda i: (0, 0, i))
    packed_spec = pl.BlockSpec((PEER_HEADS, PEER_N_KEYS // 2, tm), lambda i: (0, 0, i))
    return pl.pallas_call(
        _peer_select_kernel,
        grid=(n // tm,),
        in_specs=[
            pl.BlockSpec((tm, d), lambda i: (i, 0)),
            pl.BlockSpec(wq_t.shape, lambda i: (0, 0)),
            pl.BlockSpec(key_pair.shape, lambda i: (0, 0)),
        ],
        out_specs=[pl.BlockSpec((d, tm), lambda i: (0, i)), fac_spec, fac_spec, packed_spec, packed_spec],
        out_shape=[jax.ShapeDtypeStruct((d, n), BF16), fac(F32), fac(F32), fac(jnp.uint32), fac(jnp.uint32)],
        scratch_shapes=[pltpu.VMEM((PEER_HEADS, 2 * PEER_N_KEYS, tm), F32)],
        compiler_params=_cparams("parallel"),
        name="peer_select",
    )(h2, wq_t, key_pair)


def _peer_dense_kernel(ht_ref, u_ref, vt_ref, n0_ref, e0_ref, r1_ref, e1_ref, o_ref, g_ref, act_ref):
    nk = PEER_N_KEYS
    piece = PEER_PIECE
    n_pieces = PEER_EXP_TILE // piece
    tm = ht_ref.shape[1]

    @pl.when(pl.program_id(1) == 0)
    def _():
        o_ref[...] = jnp.zeros_like(o_ref)

    def gate_weights(p):
        zero = jnp.zeros((nk, LANES), BF16)
        for sub in range(piece // nk):
            ii = p * (piece // nk) + sub
            for cs in range(0, tm, LANES):
                w = zero
                for hd in range(PEER_HEADS):
                    n_sel = jnp.broadcast_to(n0_ref[hd, ii:ii + 1, cs:cs + LANES], (nk, LANES)).astype(BF16)
                    e0 = jnp.broadcast_to(e0_ref[hd, ii:ii + 1, cs:cs + LANES], (nk, LANES)).astype(BF16)
                    r1 = pltpu.bitcast(r1_ref[hd, :, cs:cs + LANES], BF16)
                    e1 = pltpu.bitcast(e1_ref[hd, :, cs:cs + LANES], BF16)
                    w = w + jnp.where(r1 < n_sel, e1 * e0, zero)
                g_ref[p, sub * nk:(sub + 1) * nk, cs:cs + LANES] = w

    def gate_apply(p):
        for r in range(0, piece, nk):
            for cs in range(0, tm, LANES):
                act = act_ref[p, r:r + nk, cs:cs + LANES]
                g_ref[p, r:r + nk, cs:cs + LANES] = g_ref[p, r:r + nk, cs:cs + LANES] * jax.nn.gelu(act).astype(BF16)

    gate_weights(0)
    for p in range(n_pieces):
        act_ref[p] = _dot(u_ref[p * piece:(p + 1) * piece, :], ht_ref[...])
    for p in range(n_pieces):
        if p + 1 < n_pieces:
            gate_weights(p + 1)
        gate_apply(p)
        o_ref[...] += _dot(vt_ref[:, p * piece:(p + 1) * piece], g_ref[p])


def _peer_dense(h_t, u, v_t, n0, e0, r1, e1):
    d, n = h_t.shape
    tm, te = PEER_TOK_TILE, PEER_EXP_TILE
    sub = te // PEER_N_KEYS
    return pl.pallas_call(
        _peer_dense_kernel,
        grid=(n // tm, PEER_N_EXPERTS // te),
        in_specs=[
            pl.BlockSpec((d, tm), lambda i, e: (0, i)),
            pl.BlockSpec((te, d), lambda i, e: (e, 0)),
            pl.BlockSpec((d, te), lambda i, e: (0, e)),
            pl.BlockSpec((PEER_HEADS, sub, tm), lambda i, e: (0, e, i)),
            pl.BlockSpec((PEER_HEADS, sub, tm), lambda i, e: (0, e, i)),
            pl.BlockSpec((PEER_HEADS, PEER_N_KEYS // 2, tm), lambda i, e: (0, 0, i)),
            pl.BlockSpec((PEER_HEADS, PEER_N_KEYS // 2, tm), lambda i, e: (0, 0, i)),
        ],
        out_specs=pl.BlockSpec((d, tm), lambda i, e: (0, i)),
        out_shape=jax.ShapeDtypeStruct((d, n), F32),
        scratch_shapes=[pltpu.VMEM((te // PEER_PIECE, PEER_PIECE, tm), BF16),
                        pltpu.VMEM((te // PEER_PIECE, PEER_PIECE, tm), F32)],
        compiler_params=_cparams("parallel", "arbitrary"),
        name="peer_dense",
    )(h_t, u, v_t, n0, e0, r1, e1)


def _peer_residual_kernel(x_ref, mod_ref, ot_ref, fn_ref, o_ref, *, final):
    x = x_ref[0] + mod_ref[0, 5:6, :] * ot_ref[...].T
    o_ref[0] = _rms(x, fn_ref[...]) if final else x


def _peer_residual(xs, mod, out_t, final_norm, final):
    nb, t, d = xs.shape
    n_tiles = t // TOK_TILE
    tok = lambda b, i: (b, i, 0)
    return pl.pallas_call(
        functools.partial(_peer_residual_kernel, final=final),
        grid=(nb, n_tiles),
        in_specs=[
            pl.BlockSpec((1, TOK_TILE, d), tok),
            pl.BlockSpec((1, N_MOD, d), _mod_row(nb, has_ctx=not final)),
            pl.BlockSpec((d, TOK_TILE), lambda b, i: (0, b * n_tiles + i)),
            pl.BlockSpec((1, d), lambda b, i: (0, 0)),
        ],
        out_specs=pl.BlockSpec((1, TOK_TILE, d), tok),
        out_shape=jax.ShapeDtypeStruct((nb, t, d), F32),
        compiler_params=_cparams("parallel", "parallel"),
        name="peer_residual",
    )(xs, mod, out_t, final_norm.reshape(1, d))


def _rot_cols(w):
    q = MLA_ROPE // 4
    return jnp.concatenate([-w[..., q:2 * q], w[..., :q], -w[..., 3 * q:], w[..., 2 * q:3 * q]], axis=-1)


def _rope_slab(w):
    return jnp.pad(w, ((0, 0), (MLA_NOPE, LANES - MLA_NOPE - MLA_ROPE)))


def _prep_w_in(w_in):
    o = MLA_Q_LORA + MLA_KV_LORA
    w_kr = w_in[:, o:o + MLA_ROPE]
    o2 = o + MLA_ROPE
    n_mid = NA_COLS + 2 * GLA_K_WIDTH + 2 * GLA_V_WIDTH
    low = w_in[:, o2 + n_mid:o2 + n_mid + 2 * GLA_GATE_RANK]
    parts = [w_in[:, :o], _rope_slab(w_kr), _rope_slab(_rot_cols(w_kr)), w_in[:, o2:o2 + n_mid],
             jnp.pad(low, ((0, 0), (0, LANES - 2 * GLA_GATE_RANK))), w_in[:, o2 + n_mid + 2 * GLA_GATE_RANK:]]
    return jnp.concatenate(parts, axis=1).astype(BF16)


def _prep_mla(w_uq, w_ukv):
    r = w_uq.shape[0]
    wq = w_uq.reshape(r, MLA_HEADS, MLA_NOPE + MLA_ROPE)
    pad = MLA_HEAD_PAD - MLA_NOPE - MLA_ROPE
    wq_p = jnp.pad(wq, ((0, 0), (0, 0), (0, pad)))
    wq_rot = jnp.pad(_rot_cols(wq[..., MLA_NOPE:]), ((0, 0), (0, 0), (MLA_NOPE, pad)))
    c = w_ukv.shape[0]
    wkv = w_ukv.reshape(c, MLA_HEADS, MLA_NOPE + MLA_V)
    wk = jnp.pad(wkv[..., :MLA_NOPE], ((0, 0), (0, 0), (0, MLA_HEAD_PAD - MLA_NOPE)))
    wv = wkv[..., MLA_NOPE:]
    hw = MLA_HEADS * MLA_HEAD_PAD
    lane = np.arange(LANES)
    col = np.arange(hw)
    is_rope = (lane >= MLA_NOPE) & (lane < MLA_NOPE + MLA_ROPE)
    e = (is_rope[:, None] & (col[None, :] % MLA_HEAD_PAD == lane[:, None])).astype(np.float32)
    return (wq_p.reshape(r, hw).astype(BF16), wq_rot.reshape(r, hw).astype(BF16),
            wk.reshape(c, hw).astype(BF16), jnp.asarray(e, BF16), wv.reshape(c, MLA_WIDTH).astype(BF16))


def _rope_tables(seq):
    pos = jnp.arange(seq)
    rows = (pos // GRID_W).astype(F32)
    cols = (pos % GRID_W).astype(F32)
    half = MLA_ROPE // 2
    inv = ROPE_THETA ** (-jnp.arange(0, half, 2, dtype=F32) / half)
    ar = rows[:, None] * inv
    ac = cols[:, None] * inv
    ang = jnp.concatenate([ar, ar, ac, ac], axis=-1)
    ang = jnp.concatenate([jnp.zeros((CTX_LEN, MLA_ROPE), F32), ang], axis=0)
    t = ang.shape[0]
    ones = jnp.ones((t, MLA_NOPE), F32)
    zeros = jnp.zeros((t, LANES - MLA_NOPE - MLA_ROPE), F32)
    cos_t = jnp.concatenate([ones, jnp.cos(ang), zeros], axis=1)
    sin_t = jnp.concatenate([0.0 * ones, jnp.sin(ang), zeros], axis=1)
    return cos_t, sin_t


def _prep_gla_gate(w_f, b_f, w_b, b_b):
    wg = jnp.zeros((LANES, 2 * GLA_K_WIDTH), F32)
    wg = wg.at[:GLA_GATE_RANK, :GLA_K_WIDTH].set(w_f)
    wg = wg.at[GLA_GATE_RANK:2 * GLA_GATE_RANK, GLA_K_WIDTH:].set(w_b)
    return wg, jnp.concatenate([b_f, b_b]).reshape(1, -1)


def _prep_peer_keys(sub_keys):
    half = PEER_QUERY_DIM // 2
    kp = jnp.zeros((2 * PEER_N_KEYS, PEER_QUERY_DIM), F32)
    kp = kp.at[:PEER_N_KEYS, :half].set(sub_keys[0])
    kp = kp.at[PEER_N_KEYS:, half:].set(sub_keys[1])
    return kp.astype(BF16)


def kernel(x, c, ctx, c_ctx, w_ada, b_ada, norm1, w_in, mla_q_norm, mla_w_uq, mla_kv_norm, mla_w_ukv, na_rpb, gla_w_gk_fwd, gla_b_gk_fwd, gla_w_gk_bwd, gla_b_gk_bwd, gla_norm, w_o_mla, w_o_na, w_o_gla, w_out, norm2, peer_w_q, peer_sub_keys, peer_u, peer_v, final_norm):
    nb, seq, d = x.shape
    assert ctx.shape[1] == CTX_LEN == TOK_TILE and seq % (NA_ROWS_PER_TILE * GRID_W) == 0
    assert NA_ROWS_PER_TILE * GRID_W == TOK_TILE and nb < 8
    rows = seq // GRID_W
    depth = w_ada.shape[0]

    xs = jnp.concatenate([ctx, x], axis=1)
    t = xs.shape[1]
    cc = jnp.zeros((8, d), F32).at[:nb].set(c).at[nb].set(c_ctx)
    mods = _modulation(cc, w_ada, b_ada).reshape(depth, 8, N_MOD, d)
    cos_t, sin_t = _rope_tables(seq)

    for l in range(depth):
        mod = mods[l]
        p_mla, p_na, p_gla, p_gate = _in_projection(xs, mod, norm1[l], _prep_w_in(w_in[l]))
        q, k, v = _mla_prep(p_mla, cos_t, sin_t, mla_q_norm[l], mla_kv_norm[l], *_prep_mla(mla_w_uq[l], mla_w_ukv[l]))
        o_a = _mla_attention(q, k, v)
        o_b = _na_attention(p_na, _na_bias_table(na_rpb[l], rows), rows)
        gla_f, gla_b = _gla_scan(p_gla, *_prep_gla_gate(gla_w_gk_fwd[l], gla_b_gk_fwd[l], gla_w_gk_bwd[l], gla_b_gk_bwd[l]))
        last = l == depth - 1
        xs, h2 = _merge(xs, mod, o_a, o_b, gla_f, gla_b, p_gla, p_gate, gla_norm[l], norm2[l],
                        w_o_mla[l].astype(BF16), w_o_na[l].astype(BF16), w_o_gla[l].astype(BF16), w_out[l].astype(BF16),
                        keep_ctx=not last)
        h_t, n0, e0, r1, e1 = _peer_select(h2.reshape(-1, d), peer_w_q[l].T.astype(BF16),
                                           _prep_peer_keys(peer_sub_keys[l]))
        out_t = _peer_dense(h_t, peer_u[l].astype(BF16), peer_v[l].T.astype(BF16), n0, e0, r1, e1)
        xs = _peer_residual(xs, mod, out_t, final_norm, final=last)
    return xs
```

```python
import functools

import numpy as np
import jax
import jax.numpy as jnp
from jax import lax
from jax.experimental import pallas as pl
from jax.experimental.pallas import tpu as pltpu

F32 = jnp.float32
BF16 = jnp.bfloat16
HIGHEST = lax.Precision.HIGHEST

D_MODEL = 1024
DEPTH = 2
CTX_LEN = 256
GRID_W = 64
EPS = 1e-6
N_MOD = 6
NEG_INF = -1e30

MLA_HEADS = 8
MLA_Q_LORA = 384
MLA_KV_LORA = 256
MLA_NOPE = 64
MLA_ROPE = 32
MLA_V = 64
ROPE_THETA = 10000.0
MLA_HEAD_PAD = 128

NA_HEADS = 8
NA_HEAD_DIM = 64
NA_KH = 8
NA_KW = 16
NA_ROWS_PER_TILE = 4
NA_SPAN = NA_ROWS_PER_TILE + NA_KH

GLA_HEADS = 4
GLA_DK = 64
GLA_DV = 128
GLA_GATE_RANK = 16
GLA_GATE_NORM = 16.0
GLA_CHUNK = 64

PEER_HEADS = 8
PEER_N_KEYS = 128
PEER_N_EXPERTS = PEER_N_KEYS * PEER_N_KEYS
PEER_QUERY_DIM = 128
PEER_TOPK = 16
PEER_TOK_TILE = 512
PEER_EXP_TILE = 1024
PEER_PIECE = 256

MLA_WIDTH = MLA_HEADS * MLA_V
NA_WIDTH = NA_HEADS * NA_HEAD_DIM
GLA_K_WIDTH = GLA_HEADS * GLA_DK
GLA_V_WIDTH = GLA_HEADS * GLA_DV

LANES = 128
SUBLANES = 8
TOK_TILE = 256

MLA_COLS = MLA_Q_LORA + MLA_KV_LORA + 2 * LANES
NA_COLS = 3 * NA_WIDTH
GLA_COLS = 2 * GLA_K_WIDTH + 2 * GLA_V_WIDTH + LANES
GATE_COLS = 3 * D_MODEL
IN_COLS_PAD = MLA_COLS + NA_COLS + GLA_COLS + GATE_COLS

VMEM_LIMIT = 56 * 1024 * 1024


def _cparams(*sem, flags=None):
    return pltpu.CompilerParams(dimension_semantics=sem, vmem_limit_bytes=VMEM_LIMIT, flags=flags)


def _rms(x, w):
    return x * lax.rsqrt(jnp.mean(x * x, axis=-1, keepdims=True) + EPS) * w


def _dot(a, b):
    return jnp.dot(a, b, preferred_element_type=F32)


def _dot_nt(a, b):
    return lax.dot_general(a, b, (((1,), (1,)), ((), ())), preferred_element_type=F32)


def _dot_tn(a, b):
    return lax.dot_general(a, b, (((0,), (0,)), ((), ())), preferred_element_type=F32)


def _dot_exact(a, b):
    return jnp.dot(a, b, preferred_element_type=F32, precision=HIGHEST)


def _mod_kernel(c_ref, w_ref, b_ref, o_ref):
    c = c_ref[...]
    o_ref[0] = _dot_exact(c * jax.nn.sigmoid(c), w_ref[0]) + b_ref[0]


def _modulation(cc, w_ada, b_ada):
    depth, d, n = w_ada.shape
    tn = 1536
    return pl.pallas_call(
        _mod_kernel,
        grid=(depth, n // tn),
        in_specs=[
            pl.BlockSpec((8, d), lambda l, j: (0, 0)),
            pl.BlockSpec((1, d, tn), lambda l, j: (l, 0, j)),
            pl.BlockSpec((1, 1, tn), lambda l, j: (l, 0, j)),
        ],
        out_specs=pl.BlockSpec((1, 8, tn), lambda l, j: (l, 0, j)),
        out_shape=jax.ShapeDtypeStruct((depth, 8, n), F32),
        compiler_params=_cparams("parallel", "parallel"),
        name="adaln_mod",
    )(cc, w_ada, b_ada.reshape(depth, 1, n))


def _mod_row(n_batch, has_ctx=True):
    if not has_ctx:
        return lambda b, i: (b, 0, 0)
    return lambda b, i: (jnp.where(i == 0, n_batch, b), 0, 0)


def _inproj_kernel(x_ref, mod_ref, nw_ref, w_ref, mla_ref, na_ref, gla_ref, gate_ref):
    h = _rms(x_ref[0], nw_ref[...]) * (1.0 + mod_ref[0, 1:2, :]) + mod_ref[0, 0:1, :]
    h = h.astype(BF16)
    o = 0
    mla_ref[0] = _dot(h, w_ref[:, o:o + MLA_COLS])
    o += MLA_COLS
    na_ref[0] = _dot(h, w_ref[:, o:o + NA_COLS]).astype(BF16)
    o += NA_COLS
    gla_ref[0] = _dot(h, w_ref[:, o:o + GLA_COLS])
    o += GLA_COLS
    gate_ref[0] = _dot(h, w_ref[:, o:o + GATE_COLS])


def _in_projection(xs, mod, norm_w, w_in_p):
    nb, t, d = xs.shape
    tok = lambda b, i: (b, i, 0)
    return pl.pallas_call(
        _inproj_kernel,
        grid=(nb, t // TOK_TILE),
        in_specs=[
            pl.BlockSpec((1, TOK_TILE, d), tok),
            pl.BlockSpec((1, N_MOD, d), _mod_row(nb)),
            pl.BlockSpec((1, d), lambda b, i: (0, 0)),
            pl.BlockSpec((d, IN_COLS_PAD), lambda b, i: (0, 0), pipeline_mode=pl.Buffered(1)),
        ],
        out_specs=[
            pl.BlockSpec((1, TOK_TILE, MLA_COLS), tok),
            pl.BlockSpec((1, TOK_TILE, NA_COLS), tok),
            pl.BlockSpec((1, TOK_TILE, GLA_COLS), tok),
            pl.BlockSpec((1, TOK_TILE, GATE_COLS), tok),
        ],
        out_shape=[
            jax.ShapeDtypeStruct((nb, t, MLA_COLS), F32),
            jax.ShapeDtypeStruct((nb, t, NA_COLS), BF16),
            jax.ShapeDtypeStruct((nb, t, GLA_COLS), F32),
            jax.ShapeDtypeStruct((nb, t, GATE_COLS), F32),
        ],
        compiler_params=_cparams("parallel", "parallel"),
        name="in_proj",
    )(xs, mod, norm_w.reshape(1, d), w_in_p)


def _mla_prep_kernel(p_ref, cos_ref, sin_ref, qn_ref, kvn_ref, wq_ref, wqr_ref, wk_ref, e_ref, wv_ref,
                     q_ref, k_ref, v_ref):
    scale = (MLA_NOPE + MLA_ROPE) ** -0.5
    cos = cos_ref[...]
    sin = sin_ref[...]
    o = MLA_Q_LORA + MLA_KV_LORA
    nq = _rms(p_ref[0, :, :MLA_Q_LORA], qn_ref[...]).astype(BF16)
    nkv = _rms(p_ref[0, :, MLA_Q_LORA:o], kvn_ref[...]).astype(BF16)
    q = _dot(nq, wq_ref[...])
    q_rot = _dot(nq, wqr_ref[...])
    for h in range(MLA_HEADS):
        sl = slice(h * MLA_HEAD_PAD, (h + 1) * MLA_HEAD_PAD)
        q_ref[0, :, sl] = ((q[:, sl] * cos + q_rot[:, sl] * sin) * scale).astype(BF16)
    k_rope = (p_ref[0, :, o:o + LANES] * cos + p_ref[0, :, o + LANES:o + 2 * LANES] * sin).astype(BF16)
    k_ref[0] = (_dot(nkv, wk_ref[...]) + _dot(k_rope, e_ref[...])).astype(BF16)
    v_ref[0] = _dot(nkv, wv_ref[...]).astype(BF16)


def _mla_prep(p_mla, cos_t, sin_t, q_norm, kv_norm, wq, wq_rot, wk, e_rope, wv):
    nb, t, _ = p_mla.shape
    hw = MLA_HEADS * MLA_HEAD_PAD
    tok = lambda b, i: (b, i, 0)
    const = lambda b, i: (0, 0)
    return pl.pallas_call(
        _mla_prep_kernel,
        grid=(nb, t // TOK_TILE),
        in_specs=[
            pl.BlockSpec((1, TOK_TILE, MLA_COLS), tok),
            pl.BlockSpec((TOK_TILE, LANES), lambda b, i: (i, 0)),
            pl.BlockSpec((TOK_TILE, LANES), lambda b, i: (i, 0)),
            pl.BlockSpec((1, MLA_Q_LORA), const),
            pl.BlockSpec((1, MLA_KV_LORA), const),
            pl.BlockSpec((MLA_Q_LORA, hw), const),
            pl.BlockSpec((MLA_Q_LORA, hw), const),
            pl.BlockSpec((MLA_KV_LORA, hw), const),
            pl.BlockSpec((LANES, hw), const),
            pl.BlockSpec((MLA_KV_LORA, MLA_WIDTH), const),
        ],
        out_specs=[
            pl.BlockSpec((1, TOK_TILE, hw), tok),
            pl.BlockSpec((1, TOK_TILE, hw), tok),
            pl.BlockSpec((1, TOK_TILE, MLA_WIDTH), tok),
        ],
        out_shape=[
            jax.ShapeDtypeStruct((nb, t, hw), BF16),
            jax.ShapeDtypeStruct((nb, t, hw), BF16),
            jax.ShapeDtypeStruct((nb, t, MLA_WIDTH), BF16),
        ],
        compiler_params=_cparams("parallel", "parallel"),
        name="mla_prep",
    )(p_mla, cos_t, sin_t, q_norm.reshape(1, -1), kv_norm.reshape(1, -1), wq, wq_rot, wk, e_rope, wv)


def _softmax_pv(s_list, v_list):
    m = s_list[0].max(axis=-1, keepdims=True)
    for s in s_list[1:]:
        m = jnp.maximum(m, s.max(axis=-1, keepdims=True))
    acc = None
    den = None
    for s, v in zip(s_list, v_list):
        p = jnp.exp(s - m)
        l = p.sum(axis=-1, keepdims=True)
        o = _dot(p.astype(BF16), v)
        acc = o if acc is None else acc + o
        den = l if den is None else den + l
    return acc / den


def _pair_select(o0, o1, half):
    lane = lax.broadcasted_iota(jnp.int32, o0.shape, 1)
    return jnp.where(lane < half, o0, o1)


def _mla_attn_kernel(q_ref, k_ref, v_ref, o_ref):
    def attend(nk):
        outs = []
        for j in range(2):
            sl = slice(j * MLA_HEAD_PAD, (j + 1) * MLA_HEAD_PAD)
            s = _dot_nt(q_ref[0, :, sl], k_ref[0, :nk, sl])
            outs.append(_softmax_pv([s], [v_ref[0, :nk, :]]))
        o_ref[0] = _pair_select(outs[0], outs[1], MLA_V).astype(BF16)

    is_ctx = pl.program_id(2) == 0

    @pl.when(is_ctx)
    def _():
        attend(CTX_LEN)

    @pl.when(jnp.logical_not(is_ctx))
    def _():
        attend(k_ref.shape[1])


def _mla_attention(q, k, v):
    nb, t, _ = q.shape
    pw = 2 * MLA_HEAD_PAD
    return pl.pallas_call(
        _mla_attn_kernel,
        grid=(nb, MLA_HEADS // 2, t // TOK_TILE),
        in_specs=[
            pl.BlockSpec((1, TOK_TILE, pw), lambda b, p, i: (b, i, p)),
            pl.BlockSpec((1, t, pw), lambda b, p, i: (b, 0, p)),
            pl.BlockSpec((1, t, 2 * MLA_V), lambda b, p, i: (b, 0, p)),
        ],
        out_specs=pl.BlockSpec((1, TOK_TILE, 2 * MLA_V), lambda b, p, i: (b, i, p)),
        out_shape=jax.ShapeDtypeStruct((nb, t, MLA_WIDTH), BF16),
        compiler_params=_cparams("parallel", "parallel", "arbitrary"),
        name="mla_attn",
    )(q, k, v)


def _na_tile_geometry(rows):
    n_tiles = rows // NA_ROWS_PER_TILE
    r0 = np.arange(n_tiles) * NA_ROWS_PER_TILE
    kb = np.clip(r0 - NA_KH // 2, 0, rows - NA_SPAN)
    pat = np.where(r0 == 0, 0, np.where(r0 == rows - NA_ROWS_PER_TILE, 2, 1))
    return kb, pat


def _na_bias_index(rows, tile):
    kh = min(NA_KH, rows)
    kb, _ = _na_tile_geometry(rows)
    r = tile * NA_ROWS_PER_TILE + np.arange(NA_ROWS_PER_TILE)
    start = np.clip(r - kh // 2, 0, rows - kh)
    kr = kb[tile] + np.arange(NA_SPAN)
    row_ok = (kr[None, :] >= start[:, None]) & (kr[None, :] < start[:, None] + kh)
    dr = kr[None, :] - r[:, None] + (NA_KH - 1)
    w = np.arange(GRID_W)
    col_start = np.clip(w - NA_KW // 2, 0, GRID_W - NA_KW)
    col_in = (w[None, :] >= col_start[:, None]) & (w[None, :] < col_start[:, None] + NA_KW)
    dc = np.clip(w[None, :] - w[:, None], -(NA_KW - 1), NA_KW - 1) + (NA_KW - 1)
    valid = row_ok[:, None, :, None] & col_in[None, :, None, :]
    dr = np.broadcast_to(np.clip(dr, 0, 2 * NA_KH - 2)[:, None, :, None], valid.shape)
    dc = np.broadcast_to(dc[None, :, None, :], valid.shape)
    n = NA_ROWS_PER_TILE * GRID_W
    return dr.reshape(n, -1), dc.reshape(n, -1), valid.reshape(n, -1)


def _na_bias_table(rpb, rows):
    nh, nd, _ = rpb.shape
    w = GRID_W
    kh = min(NA_KH, rows)
    edge = w - NA_KW
    r_ext = jnp.concatenate([jnp.broadcast_to(rpb[..., :1], (nh, nd, edge)), rpb,
                             jnp.broadcast_to(rpb[..., -1:], (nh, nd, edge)), jnp.zeros((nh, nd, 1), rpb.dtype)], axis=-1)
    toe = jnp.tile(r_ext, (1, 1, w))[..., :w * (2 * w - 1)].reshape(nh, nd, w, 2 * w - 1)[..., w - 1:]
    col = np.arange(w)
    col_start = np.clip(col - NA_KW // 2, 0, w - NA_KW)
    col_in = (col[None, :] >= col_start[:, None]) & (col[None, :] < col_start[:, None] + NA_KW)
    toe = jnp.where(col_in[None, None], toe, NEG_INF).astype(F32)
    masked = jnp.full((nh, w, w), NEG_INF, F32)
    kb, _ = _na_tile_geometry(rows)
    n_tiles = rows // NA_ROWS_PER_TILE
    pats = []
    for tile in (0, 1, n_tiles - 1):
        q_rows = []
        for dq in range(NA_ROWS_PER_TILE):
            r = tile * NA_ROWS_PER_TILE + dq
            start = int(np.clip(r - kh // 2, 0, rows - kh))
            blocks = []
            for dk in range(NA_SPAN):
                kr = int(kb[tile]) + dk
                blocks.append(toe[:, kr - r + NA_KH - 1] if start <= kr < start + kh else masked)
            q_rows.append(jnp.concatenate(blocks, axis=-1))
        pats.append(jnp.concatenate(q_rows, axis=1))
    return jnp.stack(pats, axis=1)


def _na_attn_kernel(kb_ref, pat_ref, q_ref, k_ref, v_ref, bias_ref, o_ref):
    i = pl.program_id(2)
    scale = NA_HEAD_DIM ** -0.5
    lane = lax.broadcasted_iota(jnp.int32, (TOK_TILE, 2 * NA_HEAD_DIM), 1)

    def head_q(j):
        q = q_ref[0].astype(F32)
        return jnp.where((lane < NA_HEAD_DIM) if j == 0 else (lane >= NA_HEAD_DIM), q, 0.0).astype(BF16)

    @pl.when(i == 0)
    def _():
        outs = []
        for j in range(2):
            s = _dot_nt(head_q(j), k_ref[0, :CTX_LEN, :]) * scale
            outs.append(_softmax_pv([s], [v_ref[0, :CTX_LEN, :]]))
        o_ref[0] = _pair_select(outs[0], outs[1], NA_HEAD_DIM).astype(BF16)

    @pl.when(i > 0)
    def _():
        start = pl.multiple_of(CTX_LEN + kb_ref[i - 1] * GRID_W, GRID_W)
        nk = NA_SPAN * GRID_W
        k_lat = k_ref[0, pl.ds(start, nk), :]
        v_lat = v_ref[0, pl.ds(start, nk), :]
        pat = pat_ref[i - 1]
        outs = []
        for j in range(2):
            qj = head_q(j)
            s_lat = _dot_nt(qj, k_lat) * scale + bias_ref[j, pat]
            s_ctx = _dot_nt(qj, k_ref[0, :CTX_LEN, :]) * scale
            outs.append(_softmax_pv([s_lat, s_ctx], [v_lat, v_ref[0, :CTX_LEN, :]]))
        o_ref[0] = _pair_select(outs[0], outs[1], NA_HEAD_DIM).astype(BF16)


def _na_attention(p_na, bias, rows):
    nb, t, _ = p_na.shape
    pw = 2 * NA_HEAD_DIM
    n_pairs = NA_HEADS // 2
    kb, pat = _na_tile_geometry(rows)
    nkeys = NA_SPAN * GRID_W
    grid_spec = pltpu.PrefetchScalarGridSpec(
        num_scalar_prefetch=2,
        grid=(n_pairs, nb, t // TOK_TILE),
        in_specs=[
            pl.BlockSpec((1, TOK_TILE, pw), lambda p, b, i, *_: (b, i, p)),
            pl.BlockSpec((1, t, pw), lambda p, b, i, *_: (b, 0, n_pairs + p)),
            pl.BlockSpec((1, t, pw), lambda p, b, i, *_: (b, 0, 2 * n_pairs + p)),
            pl.BlockSpec((2, 3, TOK_TILE, nkeys), lambda p, b, i, *_: (p, 0, 0, 0)),
        ],
        out_specs=pl.BlockSpec((1, TOK_TILE, pw), lambda p, b, i, *_: (b, i, p)),
    )
    return pl.pallas_call(
        _na_attn_kernel,
        grid_spec=grid_spec,
        out_shape=jax.ShapeDtypeStruct((nb, t, NA_WIDTH), BF16),
        compiler_params=_cparams("parallel", "parallel", "arbitrary"),
        name="na_attn",
    )(jnp.asarray(kb, jnp.int32), jnp.asarray(pat, jnp.int32), p_na, p_na, p_na, bias)


def _gla_direction(slab_ref, wg_ref, bg_ref, st_ref, o_ref, d):
    n = TOK_TILE
    n_chunks = n // GLA_CHUNK
    pw = 2 * GLA_DK
    row = lax.broadcasted_iota(jnp.int32, (n, n), 0)
    col = lax.broadcasted_iota(jnp.int32, (n, n), 1)
    shift = GLA_CHUNK.bit_length() - 1
    same_chunk = jnp.right_shift(row, shift) == jnp.right_shift(col, shift)
    causal = same_chunk & ((col <= row) if d == 0 else (col >= row))
    tri = jnp.where(causal, 1.0, 0.0).astype(F32)
    ones = jnp.where(same_chunk, 1.0, 0.0).astype(F32)

    o_low = 2 * GLA_K_WIDTH + 2 * GLA_V_WIDTH
    gw = GLA_K_WIDTH
    g_raw = _dot_exact(slab_ref[0, :, o_low:o_low + LANES], wg_ref[:, d * gw:(d + 1) * gw]) + bg_ref[:, d * gw:(d + 1) * gw]
    g = jax.nn.log_sigmoid(g_raw) / GLA_GATE_NORM
    b = _dot_exact(tri, g)
    tot = _dot_exact(ones, g)
    e_b = jnp.exp(b)
    e_nb = jnp.exp(-b)
    e_end = jnp.exp(tot - b)
    decay = jnp.exp(tot)
    lane = lax.broadcasted_iota(jnp.int32, (n, pw), 1)
    chunk_order = range(n_chunks) if d == 0 else range(n_chunks - 1, -1, -1)

    for p in range(GLA_HEADS // 2):
        sl = slice(p * pw, (p + 1) * pw)
        q_in = slab_ref[0, :, sl] * e_b[:, sl] * (GLA_DK ** -0.5)
        k_p = slab_ref[0, :, GLA_K_WIDTH + p * pw:GLA_K_WIDTH + (p + 1) * pw]
        k_in = (k_p * e_nb[:, sl]).astype(BF16)
        k_end = (k_p * e_end[:, sl]).astype(BF16)
        for j in range(2):
            h = 2 * p + j
            q_h = jnp.where((lane < GLA_DK) if j == 0 else (lane >= GLA_DK), q_in, 0.0).astype(BF16)
            v_h = slab_ref[0, :, 2 * GLA_K_WIDTH + h * GLA_DV:2 * GLA_K_WIDTH + (h + 1) * GLA_DV].astype(BF16)
            a = jnp.where(causal, _dot_nt(q_h, k_in), 0.0)
            o_intra = _dot(a.astype(BF16), v_h)
            for c in chunk_order:
                rs = slice(c * GLA_CHUNK, (c + 1) * GLA_CHUNK)
                st = st_ref[d, h]
                o_ref[0, rs, h * GLA_DV:(h + 1) * GLA_DV] = o_intra[rs] + _dot_nt(q_h[rs], st.astype(BF16))
                st_ref[d, h] = st * decay[c * GLA_CHUNK:c * GLA_CHUNK + 1, sl] + _dot_tn(v_h[rs], k_end[rs])


def _gla_kernel(f_ref, b_ref, wg_ref, bg_ref, of_ref, ob_ref, st_ref):
    @pl.when(pl.program_id(1) == 0)
    def _():
        st_ref[...] = jnp.zeros_like(st_ref)

    _gla_direction(f_ref, wg_ref, bg_ref, st_ref, of_ref, 0)
    _gla_direction(b_ref, wg_ref, bg_ref, st_ref, ob_ref, 1)


def _gla_scan(p_gla, wg, bg):
    nb, t, _ = p_gla.shape
    n_tiles = t // TOK_TILE
    fwd = lambda b, s: (b, s, 0)
    bwd = lambda b, s: (b, jnp.where(s == 0, 0, n_tiles - s), 0)
    return pl.pallas_call(
        _gla_kernel,
        grid=(nb, n_tiles),
        in_specs=[
            pl.BlockSpec((1, TOK_TILE, GLA_COLS), fwd),
            pl.BlockSpec((1, TOK_TILE, GLA_COLS), bwd),
            pl.BlockSpec((LANES, 2 * GLA_K_WIDTH), lambda b, s: (0, 0)),
            pl.BlockSpec((1, 2 * GLA_K_WIDTH), lambda b, s: (0, 0)),
        ],
        out_specs=[
            pl.BlockSpec((1, TOK_TILE, GLA_V_WIDTH), fwd),
            pl.BlockSpec((1, TOK_TILE, GLA_V_WIDTH), bwd),
        ],
        out_shape=[jax.ShapeDtypeStruct((nb, t, GLA_V_WIDTH), F32)] * 2,
        scratch_shapes=[pltpu.VMEM((2, GLA_HEADS, GLA_DV, 2 * GLA_DK), F32)],
        compiler_params=_cparams("parallel", "arbitrary"),
        name="gla_scan",
    )(p_gla, p_gla, wg, bg)


def _merge_kernel(x_ref, mod_ref, oa_ref, ob_ref, gf_ref, gb_ref, og_ref, gate_ref, gn_ref, n2_ref,
                  wa_ref, wb_ref, wg_ref, wo_ref, x_out_ref, h_out_ref):
    og = og_ref[0]
    o_raw = gf_ref[0] + gb_ref[0]
    parts = []
    for h in range(GLA_HEADS):
        sl = slice(h * GLA_DV, (h + 1) * GLA_DV)
        parts.append(_rms(o_raw[:, sl], gn_ref[...]) * jax.nn.silu(og[:, sl]))
    o_g = jnp.concatenate(parts, axis=-1).astype(BF16)
    d = D_MODEL
    y = (jax.nn.sigmoid(gate_ref[0, :, 0:d]) * _dot(oa_ref[0], wa_ref[...])
         + jax.nn.sigmoid(gate_ref[0, :, d:2 * d]) * _dot(ob_ref[0], wb_ref[...])
         + jax.nn.sigmoid(gate_ref[0, :, 2 * d:3 * d]) * _dot(o_g, wg_ref[...]))
    x = x_ref[0] + mod_ref[0, 2:3, :] * _dot(y.astype(BF16), wo_ref[...])
    x_out_ref[0] = x
    h2 = _rms(x, n2_ref[...]) * (1.0 + mod_ref[0, 4:5, :]) + mod_ref[0, 3:4, :]
    h_out_ref[0] = h2.astype(BF16)


def _merge(xs, mod, o_a, o_b, gla_f, gla_b, p_gla, p_gate, gla_norm, norm2, wa, wb, wg, wo, keep_ctx):
    nb, t, d = xs.shape
    skip = 0 if keep_ctx else 1
    t_out = t - skip * TOK_TILE
    src = lambda b, i: (b, i + skip, 0)
    tok = lambda b, i: (b, i, 0)
    const = lambda b, i: (0, 0)
    og_block = (2 * GLA_K_WIDTH) // GLA_V_WIDTH + 1
    return pl.pallas_call(
        _merge_kernel,
        grid=(nb, t_out // TOK_TILE),
        in_specs=[
            pl.BlockSpec((1, TOK_TILE, d), src),
            pl.BlockSpec((1, N_MOD, d), _mod_row(nb, keep_ctx)),
            pl.BlockSpec((1, TOK_TILE, MLA_WIDTH), src),
            pl.BlockSpec((1, TOK_TILE, NA_WIDTH), src),
            pl.BlockSpec((1, TOK_TILE, GLA_V_WIDTH), src),
            pl.BlockSpec((1, TOK_TILE, GLA_V_WIDTH), src),
            pl.BlockSpec((1, TOK_TILE, GLA_V_WIDTH), lambda b, i: (b, i + skip, og_block)),
            pl.BlockSpec((1, TOK_TILE, GATE_COLS), src),
            pl.BlockSpec((1, GLA_DV), const),
            pl.BlockSpec((1, d), const),
            pl.BlockSpec((MLA_WIDTH, d), const),
            pl.BlockSpec((NA_WIDTH, d), const),
            pl.BlockSpec((GLA_V_WIDTH, d), const),
            pl.BlockSpec((d, d), const),
        ],
        out_specs=[pl.BlockSpec((1, TOK_TILE, d), tok), pl.BlockSpec((1, TOK_TILE, d), tok)],
        out_shape=[jax.ShapeDtypeStruct((nb, t_out, d), F32), jax.ShapeDtypeStruct((nb, t_out, d), BF16)],
        compiler_params=_cparams("parallel", "parallel"),
        name="merge",
    )(xs, mod, o_a, o_b, gla_f, gla_b, p_gla, p_gate, gla_norm.reshape(1, -1), norm2.reshape(1, -1),
      wa, wb, wg, wo)


def _top_ranked(x, k, want_rank=True):
    row = lax.broadcasted_iota(jnp.int32, x.shape, 0).astype(F32)
    out_row = lax.broadcasted_iota(jnp.int32, (k, x.shape[1]), 0)
    vals = jnp.zeros((k, x.shape[1]), F32)
    rank = jnp.full(x.shape, float(k), F32)
    for r in range(k):
        m = jnp.max(x, axis=0, keepdims=True)
        vals = jnp.where(out_row == r, m, vals)
        hit = row == jnp.min(jnp.where(x == m, row, float(x.shape[0])), axis=0, keepdims=True)
        if want_rank:
            rank = jnp.where(hit, float(r), rank)
        x = jnp.where(hit, -jnp.inf, x)
    return vals, rank


def _count(mask):
    return jnp.sum(jnp.where(mask, 1.0, 0.0), axis=0, keepdims=True)


def _pack_rows(x):
    return pltpu.bitcast(x.astype(BF16), jnp.uint32)


def _peer_select_chunk(s0, s1):
    k = PEER_TOPK
    sub = lax.broadcasted_iota(jnp.int32, (SUBLANES, LANES), 0)
    v0, rank0 = _top_ranked(s0, k)
    v1, rank1 = _top_ranked(s1, k)
    cand = [v0[0:1] + v1]
    for a in range(1, k):
        cand.append(jnp.where(sub < k // (a + 1), v0[a:a + 1] + v1[:SUBLANES], -jnp.inf))
    top, _ = _top_ranked(jnp.concatenate(cand, axis=0), k, want_rank=False)
    thr = top[k - 1:k]
    z = jnp.sum(jnp.exp(top - top[0:1]), axis=0, keepdims=True)
    above = [_count(ca > thr) for ca in cand]
    equal = [_count(ca == thr) for ca in cand]
    spare = float(k) - functools.reduce(jnp.add, above)
    n0 = jnp.zeros_like(s0)
    for a in range(k):
        take = jnp.clip(spare, 0.0, equal[a])
        spare = spare - equal[a]
        n0 = jnp.where(rank0 == float(a), above[a] + take, n0)
    return n0, jnp.exp(s0 - v0[0:1]) / z, _pack_rows(rank1), _pack_rows(jnp.exp(s1 - v1[0:1]))


def _sorting_network(n):
    pairs = []
    p = 1
    while p < n:
        k = p
        while k >= 1:
            for j in range(k % p, n - k, 2 * k):
                for i in range(min(k, n - j - k)):
                    if (i + j) // (2 * p) == (i + j + k) // (2 * p):
                        pairs.append((i + j, i + j + k))
            k //= 2
        p *= 2
    return pairs


def _sublane_sum(x):
    for shift in (4, 2, 1):
        x = x + pltpu.roll(x, shift, axis=0)
    return x


def _top_sorted(blocks, k):
    def exchange(xs, i, j):
        xs[i], xs[j] = jnp.maximum(xs[i], xs[j]), jnp.minimum(xs[i], xs[j])

    xs = list(blocks[:k])
    for i, j in _sorting_network(k):
        exchange(xs, i, j)
    for extra in blocks[k:]:
        for r in range(k):
            xs[r], extra = jnp.maximum(xs[r], extra), jnp.minimum(xs[r], extra)
    for shift in (4, 2, 1):
        ys = [pltpu.roll(x, shift, axis=0) for x in xs]
        xs = [jnp.maximum(xs[r], ys[k - 1 - r]) for r in range(k)]
        stride = k // 2
        while stride >= 1:
            for r in range(k):
                if r & stride == 0:
                    exchange(xs, r, r + stride)
            stride //= 2
    return xs


def _peer_select_chunk_distinct(s0, s1):
    k = PEER_TOPK
    nb = s0.shape[0] // SUBLANES
    sub = lax.broadcasted_iota(jnp.int32, (SUBLANES, LANES), 0)
    blocks0 = [s0[SUBLANES * i:SUBLANES * (i + 1)] for i in range(nb)]
    blocks1 = [s1[SUBLANES * i:SUBLANES * (i + 1)] for i in range(nb)]
    v0 = _top_sorted(blocks0, k)
    v1 = _top_sorted(blocks1, k)

    def ties(v, blocks):
        same = functools.reduce(jnp.add, [jnp.where(v[r] == v[r + 1], 1.0, 0.0) for r in range(k - 1)])
        reach = _sublane_sum(functools.reduce(jnp.add, [jnp.where(b >= v[k - 1], 1.0, 0.0) for b in blocks]))
        return same + jnp.abs(reach - float(k))

    v1_lo, v1_hi = v1[SUBLANES - 1], v1[2 * SUBLANES - 1]
    for b in range(SUBLANES - 2, -1, -1):
        v1_lo = jnp.where(sub == b, v1[b], v1_lo)
        v1_hi = jnp.where(sub == b, v1[SUBLANES + b], v1_hi)
    cand = [[v0[0] + v1_lo, v0[0] + v1_hi]]
    for a in range(1, k):
        cand.append([jnp.where(sub < k // (a + 1), v0[a] + v1_lo, -jnp.inf)])
    cand_blocks = [c for ca in cand for c in ca]
    top = _top_sorted(cand_blocks[1:] + cand_blocks[:1], k)
    thr = top[k - 1]
    z = functools.reduce(jnp.add, [jnp.exp(t - top[0]) for t in top])
    n_sel = [_sublane_sum(functools.reduce(jnp.add, [jnp.where(c >= thr, 1.0, 0.0) for c in ca])) for ca in cand]
    redo = ties(v0, blocks0) + ties(v1, blocks1) + ties(top, cand_blocks)

    n0, e0, rank1, e1 = [], [], [], []
    for x0, x1 in zip(blocks0, blocks1):
        n = jnp.zeros_like(x0)
        for a in range(k):
            n = jnp.where(x0 == v0[a], n_sel[a], n)
        n0.append(n)
        e0.append(jnp.exp(x0 - v0[0]) / z)
        rank1.append(functools.reduce(jnp.add, [jnp.where(v > x1, 1.0, 0.0) for v in v1]))
        e1.append(jnp.exp(x1 - v1[0]))
    cat = lambda parts: jnp.concatenate(parts, axis=0)
    return (cat(n0), cat(e0), _pack_rows(cat(rank1)), _pack_rows(cat(e1))), redo[0:1]


def _peer_select_kernel(h_ref, wqt_ref, kp_ref, ht_ref, n0_ref, e0_ref, r1_ref, e1_ref, s_ref):
    h = h_ref[...]
    ht_ref[...] = h.T
    q_t = _dot_nt(wqt_ref[...], h)
    nk = PEER_N_KEYS
    for hd in range(PEER_HEADS):
        s_ref[hd] = _dot(kp_ref[...], q_t[hd * PEER_QUERY_DIM:(hd + 1) * PEER_QUERY_DIM].astype(BF16))

    def run(exact):
        redo = jnp.zeros((1, LANES), F32)
        for hd in range(PEER_HEADS):
            for c in range(h.shape[0] // LANES):
                cs = slice(c * LANES, (c + 1) * LANES)
                s0, s1 = s_ref[hd, :nk, cs], s_ref[hd, nk:, cs]
                if exact:
                    outs = _peer_select_chunk(s0, s1)
                else:
                    outs, flag = _peer_select_chunk_distinct(s0, s1)
                    redo = jnp.maximum(redo, flag)
                for ref, val in zip((n0_ref, e0_ref, r1_ref, e1_ref), outs):
                    ref[hd, :, cs] = val
        return redo

    redo = run(exact=False)

    @pl.when(jnp.max(redo) > 0.0)
    def _():
        run(exact=True)


def _peer_select(h2, wq_t, key_pair):
    n, d = h2.shape
    tm = TOK_TILE
    fac = lambda dt: jax.ShapeDtypeStruct((PEER_HEADS, PEER_N_KEYS // (1 if dt == F32 else 2), n), dt)
    fac_spec = pl.BlockSpec((PEER_HEADS, PEER_N_KEYS, tm), lambda i: (0, 0, i))
    packed_spec = pl.BlockSpec((PEER_HEADS, PEER_N_KEYS // 2, tm), lambda i: (0, 0, i))
    return pl.pallas_call(
        _peer_select_kernel,
        grid=(n // tm,),
        in_specs=[
            pl.BlockSpec((tm, d), lambda i: (i, 0)),
            pl.BlockSpec(wq_t.shape, lambda i: (0, 0)),
            pl.BlockSpec(key_pair.shape, lambda i: (0, 0)),
        ],
        out_specs=[pl.BlockSpec((d, tm), lambda i: (0, i)), fac_spec, fac_spec, packed_spec, packed_spec],
        out_shape=[jax.ShapeDtypeStruct((d, n), BF16), fac(F32), fac(F32), fac(jnp.uint32), fac(jnp.uint32)],
        scratch_shapes=[pltpu.VMEM((PEER_HEADS, 2 * PEER_N_KEYS, tm), F32)],
        compiler_params=_cparams("parallel"),
        name="peer_select",
    )(h2, wq_t, key_pair)


def _peer_dense_kernel(ht_ref, u_ref, vt_ref, n0_ref, e0_ref, r1_ref, e1_ref, o_ref, g_ref, act_ref):
    nk = PEER_N_KEYS
    piece = PEER_PIECE
    n_pieces = PEER_EXP_TILE // piece
    tm = ht_ref.shape[1]

    @pl.when(pl.program_id(1) == 0)
    def _():
        o_ref[...] = jnp.zeros_like(o_ref)

    def gate_weights(p):
        rs_n = nk // 2
        zero = jnp.zeros((rs_n, LANES), BF16)
        for sub in range(piece // nk):
            ii = p * (piece // nk) + sub
            for cs in range(0, tm, LANES):
                for rs in range(0, nk, rs_n):
                    w = zero
                    for hd in range(PEER_HEADS):
                        n_sel = jnp.broadcast_to(n0_ref[hd, ii:ii + 1, cs:cs + LANES], (rs_n, LANES)).astype(BF16)
                        e0 = jnp.broadcast_to(e0_ref[hd, ii:ii + 1, cs:cs + LANES], (rs_n, LANES)).astype(BF16)
                        r1 = pltpu.bitcast(r1_ref[hd, rs // 2:(rs + rs_n) // 2, cs:cs + LANES], BF16)
                        e1 = pltpu.bitcast(e1_ref[hd, rs // 2:(rs + rs_n) // 2, cs:cs + LANES], BF16)
                        w = w + jnp.where(r1 < n_sel, e1 * e0, zero)
                    g_ref[p, sub * nk + rs:sub * nk + rs + rs_n, cs:cs + LANES] = w

    def gate_apply(p):
        for r in range(0, piece, nk):
            for cs in range(0, tm, LANES):
                act = act_ref[p, r:r + nk, cs:cs + LANES]
                g_ref[p, r:r + nk, cs:cs + LANES] = g_ref[p, r:r + nk, cs:cs + LANES] * jax.nn.gelu(act).astype(BF16)

    gate_weights(0)
    for p in range(n_pieces):
        act_ref[p] = _dot(u_ref[p * piece:(p + 1) * piece, :], ht_ref[...])
    for p in range(n_pieces):
        if p + 1 < n_pieces:
            gate_weights(p + 1)
        gate_apply(p)
        o_ref[...] += _dot(vt_ref[:, p * piece:(p + 1) * piece], g_ref[p])


def _peer_dense(h_t, u, v_t, n0, e0, r1, e1):
    d, n = h_t.shape
    tm, te = PEER_TOK_TILE, PEER_EXP_TILE
    sub = te // PEER_N_KEYS
    return pl.pallas_call(
        _peer_dense_kernel,
        grid=(n // tm, PEER_N_EXPERTS // te),
        in_specs=[
            pl.BlockSpec((d, tm), lambda i, e: (0, i)),
            pl.BlockSpec((te, d), lambda i, e: (e, 0)),
            pl.BlockSpec((d, te), lambda i, e: (0, e)),
            pl.BlockSpec((PEER_HEADS, sub, tm), lambda i, e: (0, e, i)),
            pl.BlockSpec((PEER_HEADS, sub, tm), lambda i, e: (0, e, i)),
            pl.BlockSpec((PEER_HEADS, PEER_N_KEYS // 2, tm), lambda i, e: (0, 0, i)),
            pl.BlockSpec((PEER_HEADS, PEER_N_KEYS // 2, tm), lambda i, e: (0, 0, i)),
        ],
        out_specs=pl.BlockSpec((d, tm), lambda i, e: (0, i)),
        out_shape=jax.ShapeDtypeStruct((d, n), F32),
        scratch_shapes=[pltpu.VMEM((te // PEER_PIECE, PEER_PIECE, tm), BF16),
                        pltpu.VMEM((te // PEER_PIECE, PEER_PIECE, tm), F32)],
        compiler_params=_cparams("parallel", "arbitrary"),
        name="peer_dense",
    )(h_t, u, v_t, n0, e0, r1, e1)


def _peer_residual_kernel(x_ref, mod_ref, ot_ref, fn_ref, o_ref, *, final):
    x = x_ref[0] + mod_ref[0, 5:6, :] * ot_ref[...].T
    o_ref[0] = _rms(x, fn_ref[...]) if final else x


def _peer_residual(xs, mod, out_t, final_norm, final):
    nb, t, d = xs.shape
    n_tiles = t // TOK_TILE
    tok = lambda b, i: (b, i, 0)
    return pl.pallas_call(
        functools.partial(_peer_residual_kernel, final=final),
        grid=(nb, n_tiles),
        in_specs=[
            pl.BlockSpec((1, TOK_TILE, d), tok),
            pl.BlockSpec((1, N_MOD, d), _mod_row(nb, has_ctx=not final)),
            pl.BlockSpec((d, TOK_TILE), lambda b, i: (0, b * n_tiles + i)),
            pl.BlockSpec((1, d), lambda b, i: (0, 0)),
        ],
        out_specs=pl.BlockSpec((1, TOK_TILE, d), tok),
        out_shape=jax.ShapeDtypeStruct((nb, t, d), F32),
        compiler_params=_cparams("parallel", "parallel"),
        name="peer_residual",
    )(xs, mod, out_t, final_norm.reshape(1, d))


def _rot_cols(w):
    q = MLA_ROPE // 4
    return jnp.concatenate([-w[..., q:2 * q], w[..., :q], -w[..., 3 * q:], w[..., 2 * q:3 * q]], axis=-1)


def _rope_slab(w):
    return jnp.pad(w, ((0, 0), (MLA_NOPE, LANES - MLA_NOPE - MLA_ROPE)))


def _prep_w_in(w_in):
    o = MLA_Q_LORA + MLA_KV_LORA
    w_kr = w_in[:, o:o + MLA_ROPE]
    o2 = o + MLA_ROPE
    n_mid = NA_COLS + 2 * GLA_K_WIDTH + 2 * GLA_V_WIDTH
    low = w_in[:, o2 + n_mid:o2 + n_mid + 2 * GLA_GATE_RANK]
    parts = [w_in[:, :o], _rope_slab(w_kr), _rope_slab(_rot_cols(w_kr)), w_in[:, o2:o2 + n_mid],
             jnp.pad(low, ((0, 0), (0, LANES - 2 * GLA_GATE_RANK))), w_in[:, o2 + n_mid + 2 * GLA_GATE_RANK:]]
    return jnp.concatenate(parts, axis=1).astype(BF16)


def _prep_mla(w_uq, w_ukv):
    r = w_uq.shape[0]
    wq = w_uq.reshape(r, MLA_HEADS, MLA_NOPE + MLA_ROPE)
    pad = MLA_HEAD_PAD - MLA_NOPE - MLA_ROPE
    wq_p = jnp.pad(wq, ((0, 0), (0, 0), (0, pad)))
    wq_rot = jnp.pad(_rot_cols(wq[..., MLA_NOPE:]), ((0, 0), (0, 0), (MLA_NOPE, pad)))
    c = w_ukv.shape[0]
    wkv = w_ukv.reshape(c, MLA_HEADS, MLA_NOPE + MLA_V)
    wk = jnp.pad(wkv[..., :MLA_NOPE], ((0, 0), (0, 0), (0, MLA_HEAD_PAD - MLA_NOPE)))
    wv = wkv[..., MLA_NOPE:]
    hw = MLA_HEADS * MLA_HEAD_PAD
    lane = np.arange(LANES)
    col = np.arange(hw)
    is_rope = (lane >= MLA_NOPE) & (lane < MLA_NOPE + MLA_ROPE)
    e = (is_rope[:, None] & (col[None, :] % MLA_HEAD_PAD == lane[:, None])).astype(np.float32)
    return (wq_p.reshape(r, hw).astype(BF16), wq_rot.reshape(r, hw).astype(BF16),
            wk.reshape(c, hw).astype(BF16), jnp.asarray(e, BF16), wv.reshape(c, MLA_WIDTH).astype(BF16))


def _rope_tables(seq):
    pos = jnp.arange(seq)
    rows = (pos // GRID_W).astype(F32)
    cols = (pos % GRID_W).astype(F32)
    half = MLA_ROPE // 2
    inv = ROPE_THETA ** (-jnp.arange(0, half, 2, dtype=F32) / half)
    ar = rows[:, None] * inv
    ac = cols[:, None] * inv
    ang = jnp.concatenate([ar, ar, ac, ac], axis=-1)
    ang = jnp.concatenate([jnp.zeros((CTX_LEN, MLA_ROPE), F32), ang], axis=0)
    t = ang.shape[0]
    ones = jnp.ones((t, MLA_NOPE), F32)
    zeros = jnp.zeros((t, LANES - MLA_NOPE - MLA_ROPE), F32)
    cos_t = jnp.concatenate([ones, jnp.cos(ang), zeros], axis=1)
    sin_t = jnp.concatenate([0.0 * ones, jnp.sin(ang), zeros], axis=1)
    return cos_t, sin_t


def _prep_gla_gate(w_f, b_f, w_b, b_b):
    wg = jnp.zeros((LANES, 2 * GLA_K_WIDTH), F32)
    wg = wg.at[:GLA_GATE_RANK, :GLA_K_WIDTH].set(w_f)
    wg = wg.at[GLA_GATE_RANK:2 * GLA_GATE_RANK, GLA_K_WIDTH:].set(w_b)
    return wg, jnp.concatenate([b_f, b_b]).reshape(1, -1)


def _prep_peer_keys(sub_keys):
    half = PEER_QUERY_DIM // 2
    kp = jnp.zeros((2 * PEER_N_KEYS, PEER_QUERY_DIM), F32)
    kp = kp.at[:PEER_N_KEYS, :half].set(sub_keys[0])
    kp = kp.at[PEER_N_KEYS:, half:].set(sub_keys[1])
    return kp.astype(BF16)


def kernel(x, c, ctx, c_ctx, w_ada, b_ada, norm1, w_in, mla_q_norm, mla_w_uq, mla_kv_norm, mla_w_ukv, na_rpb, gla_w_gk_fwd, gla_b_gk_fwd, gla_w_gk_bwd, gla_b_gk_bwd, gla_norm, w_o_mla, w_o_na, w_o_gla, w_out, norm2, peer_w_q, peer_sub_keys, peer_u, peer_v, final_norm):
    nb, seq, d = x.shape
    assert ctx.shape[1] == CTX_LEN == TOK_TILE and seq % (NA_ROWS_PER_TILE * GRID_W) == 0
    assert NA_ROWS_PER_TILE * GRID_W == TOK_TILE and nb < 8
    rows = seq // GRID_W
    depth = w_ada.shape[0]

    xs = jnp.concatenate([ctx, x], axis=1)
    t = xs.shape[1]
    cc = jnp.zeros((8, d), F32).at[:nb].set(c).at[nb].set(c_ctx)
    mods = _modulation(cc, w_ada, b_ada).reshape(depth, 8, N_MOD, d)
    cos_t, sin_t = _rope_tables(seq)

    for l in range(depth):
        mod = mods[l]
        p_mla, p_na, p_gla, p_gate = _in_projection(xs, mod, norm1[l], _prep_w_in(w_in[l]))
        q, k, v = _mla_prep(p_mla, cos_t, sin_t, mla_q_norm[l], mla_kv_norm[l], *_prep_mla(mla_w_uq[l], mla_w_ukv[l]))
        o_a = _mla_attention(q, k, v)
        o_b = _na_attention(p_na, _na_bias_table(na_rpb[l], rows), rows)
        gla_f, gla_b = _gla_scan(p_gla, *_prep_gla_gate(gla_w_gk_fwd[l], gla_b_gk_fwd[l], gla_w_gk_bwd[l], gla_b_gk_bwd[l]))
        last = l == depth - 1
        xs, h2 = _merge(xs, mod, o_a, o_b, gla_f, gla_b, p_gla, p_gate, gla_norm[l], norm2[l],
                        w_o_mla[l].astype(BF16), w_o_na[l].astype(BF16), w_o_gla[l].astype(BF16), w_out[l].astype(BF16),
                        keep_ctx=not last)
        h_t, n0, e0, r1, e1 = _peer_select(h2.reshape(-1, d), peer_w_q[l].T.astype(BF16),
                                           _prep_peer_keys(peer_sub_keys[l]))
        out_t = _peer_dense(h_t, peer_u[l].astype(BF16), peer_v[l].T.astype(BF16), n0, e0, r1, e1)
        xs = _peer_residual(xs, mod, out_t, final_norm, final=last)
    return xs
```

```python
import functools

import numpy as np
import jax
import jax.numpy as jnp
from jax import lax
from jax.experimental import pallas as pl
from jax.experimental.pallas import tpu as pltpu

F32 = jnp.float32
BF16 = jnp.bfloat16
HIGHEST = lax.Precision.HIGHEST

D_MODEL = 1024
DEPTH = 2
CTX_LEN = 256
GRID_W = 64
EPS = 1e-6
N_MOD = 6
NEG_INF = -1e30

MLA_HEADS = 8
MLA_Q_LORA = 384
MLA_KV_LORA = 256
MLA_NOPE = 64
MLA_ROPE = 32
MLA_V = 64
ROPE_THETA = 10000.0
MLA_HEAD_PAD = 128

NA_HEADS = 8
NA_HEAD_DIM = 64
NA_KH = 8
NA_KW = 16
NA_ROWS_PER_TILE = 4
NA_SPAN = NA_ROWS_PER_TILE + NA_KH

GLA_HEADS = 4
GLA_DK = 64
GLA_DV = 128
GLA_GATE_RANK = 16
GLA_GATE_NORM = 16.0
GLA_CHUNK = 64

PEER_HEADS = 8
PEER_N_KEYS = 128
PEER_N_EXPERTS = PEER_N_KEYS * PEER_N_KEYS
PEER_QUERY_DIM = 128
PEER_TOPK = 16
PEER_TOK_TILE = 512
PEER_EXP_TILE = 1024
PEER_PIECE = 256

MLA_WIDTH = MLA_HEADS * MLA_V
NA_WIDTH = NA_HEADS * NA_HEAD_DIM
GLA_K_WIDTH = GLA_HEADS * GLA_DK
GLA_V_WIDTH = GLA_HEADS * GLA_DV

LANES = 128
SUBLANES = 8
TOK_TILE = 256

MLA_COLS = MLA_Q_LORA + MLA_KV_LORA + 2 * LANES
NA_COLS = 3 * NA_WIDTH
GLA_COLS = 2 * GLA_K_WIDTH + 2 * GLA_V_WIDTH + LANES
GATE_COLS = 3 * D_MODEL
IN_COLS_PAD = MLA_COLS + NA_COLS + GLA_COLS + GATE_COLS

VMEM_LIMIT = 56 * 1024 * 1024


def _cparams(*sem, flags=None):
    return pltpu.CompilerParams(dimension_semantics=sem, vmem_limit_bytes=VMEM_LIMIT, flags=flags)


def _rms(x, w):
    return x * lax.rsqrt(jnp.mean(x * x, axis=-1, keepdims=True) + EPS) * w


def _dot(a, b):
    return jnp.dot(a, b, preferred_element_type=F32)


def _dot_nt(a, b):
    return lax.dot_general(a, b, (((1,), (1,)), ((), ())), preferred_element_type=F32)


def _dot_tn(a, b):
    return lax.dot_general(a, b, (((0,), (0,)), ((), ())), preferred_element_type=F32)


def _dot_exact(a, b):
    return jnp.dot(a, b, preferred_element_type=F32, precision=HIGHEST)


def _mod_kernel(c_ref, w_ref, b_ref, o_ref):
    c = c_ref[...]
    o_ref[0] = _dot_exact(c * jax.nn.sigmoid(c), w_ref[0]) + b_ref[0]


def _modulation(cc, w_ada, b_ada):
    depth, d, n = w_ada.shape
    tn = 1536
    return pl.pallas_call(
        _mod_kernel,
        grid=(depth, n // tn),
        in_specs=[
            pl.BlockSpec((8, d), lambda l, j: (0, 0)),
            pl.BlockSpec((1, d, tn), lambda l, j: (l, 0, j)),
            pl.BlockSpec((1, 1, tn), lambda l, j: (l, 0, j)),
        ],
        out_specs=pl.BlockSpec((1, 8, tn), lambda l, j: (l, 0, j)),
        out_shape=jax.ShapeDtypeStruct((depth, 8, n), F32),
        compiler_params=_cparams("parallel", "parallel"),
        name="adaln_mod",
    )(cc, w_ada, b_ada.reshape(depth, 1, n))


def _mod_row(n_batch, has_ctx=True):
    if not has_ctx:
        return lambda b, i: (b, 0, 0)
    return lambda b, i: (jnp.where(i == 0, n_batch, b), 0, 0)


def _inproj_kernel(x_ref, mod_ref, nw_ref, w_ref, mla_ref, na_ref, gla_ref, gate_ref):
    h = _rms(x_ref[0], nw_ref[...]) * (1.0 + mod_ref[0, 1:2, :]) + mod_ref[0, 0:1, :]
    h = h.astype(BF16)
    o = 0
    mla_ref[0] = _dot(h, w_ref[:, o:o + MLA_COLS])
    o += MLA_COLS
    na_ref[0] = _dot(h, w_ref[:, o:o + NA_COLS]).astype(BF16)
    o += NA_COLS
    gla_ref[0] = _dot(h, w_ref[:, o:o + GLA_COLS])
    o += GLA_COLS
    gate_ref[0] = _dot(h, w_ref[:, o:o + GATE_COLS])


def _in_projection(xs, mod, norm_w, w_in_p):
    nb, t, d = xs.shape
    tok = lambda b, i: (b, i, 0)
    return pl.pallas_call(
        _inproj_kernel,
        grid=(nb, t // TOK_TILE),
        in_specs=[
            pl.BlockSpec((1, TOK_TILE, d), tok),
            pl.BlockSpec((1, N_MOD, d), _mod_row(nb)),
            pl.BlockSpec((1, d), lambda b, i: (0, 0)),
            pl.BlockSpec((d, IN_COLS_PAD), lambda b, i: (0, 0), pipeline_mode=pl.Buffered(1)),
        ],
        out_specs=[
            pl.BlockSpec((1, TOK_TILE, MLA_COLS), tok),
            pl.BlockSpec((1, TOK_TILE, NA_COLS), tok),
            pl.BlockSpec((1, TOK_TILE, GLA_COLS), tok),
            pl.BlockSpec((1, TOK_TILE, GATE_COLS), tok),
        ],
        out_shape=[
            jax.ShapeDtypeStruct((nb, t, MLA_COLS), F32),
            jax.ShapeDtypeStruct((nb, t, NA_COLS), BF16),
            jax.ShapeDtypeStruct((nb, t, GLA_COLS), F32),
            jax.ShapeDtypeStruct((nb, t, GATE_COLS), F32),
        ],
        compiler_params=_cparams("parallel", "parallel"),
        name="in_proj",
    )(xs, mod, norm_w.reshape(1, d), w_in_p)


def _mla_prep_kernel(p_ref, cos_ref, sin_ref, qn_ref, kvn_ref, wq_ref, wqr_ref, wk_ref, e_ref, wv_ref,
                     q_ref, k_ref, v_ref):
    scale = (MLA_NOPE + MLA_ROPE) ** -0.5
    cos = cos_ref[...]
    sin = sin_ref[...]
    o = MLA_Q_LORA + MLA_KV_LORA
    nq = _rms(p_ref[0, :, :MLA_Q_LORA], qn_ref[...]).astype(BF16)
    nkv = _rms(p_ref[0, :, MLA_Q_LORA:o], kvn_ref[...]).astype(BF16)
    q = _dot(nq, wq_ref[...])
    q_rot = _dot(nq, wqr_ref[...])
    for h in range(MLA_HEADS):
        sl = slice(h * MLA_HEAD_PAD, (h + 1) * MLA_HEAD_PAD)
        q_ref[0, :, sl] = ((q[:, sl] * cos + q_rot[:, sl] * sin) * scale).astype(BF16)
    k_rope = (p_ref[0, :, o:o + LANES] * cos + p_ref[0, :, o + LANES:o + 2 * LANES] * sin).astype(BF16)
    k_ref[0] = (_dot(nkv, wk_ref[...]) + _dot(k_rope, e_ref[...])).astype(BF16)
    v_ref[0] = _dot(nkv, wv_ref[...]).astype(BF16)


def _mla_prep(p_mla, cos_t, sin_t, q_norm, kv_norm, wq, wq_rot, wk, e_rope, wv):
    nb, t, _ = p_mla.shape
    hw = MLA_HEADS * MLA_HEAD_PAD
    tok = lambda b, i: (b, i, 0)
    const = lambda b, i: (0, 0)
    return pl.pallas_call(
        _mla_prep_kernel,
        grid=(nb, t // TOK_TILE),
        in_specs=[
            pl.BlockSpec((1, TOK_TILE, MLA_COLS), tok),
            pl.BlockSpec((TOK_TILE, LANES), lambda b, i: (i, 0)),
            pl.BlockSpec((TOK_TILE, LANES), lambda b, i: (i, 0)),
            pl.BlockSpec((1, MLA_Q_LORA), const),
            pl.BlockSpec((1, MLA_KV_LORA), const),
            pl.BlockSpec((MLA_Q_LORA, hw), const),
            pl.BlockSpec((MLA_Q_LORA, hw), const),
            pl.BlockSpec((MLA_KV_LORA, hw), const),
            pl.BlockSpec((LANES, hw), const),
            pl.BlockSpec((MLA_KV_LORA, MLA_WIDTH), const),
        ],
        out_specs=[
            pl.BlockSpec((1, TOK_TILE, hw), tok),
            pl.BlockSpec((1, TOK_TILE, hw), tok),
            pl.BlockSpec((1, TOK_TILE, MLA_WIDTH), tok),
        ],
        out_shape=[
            jax.ShapeDtypeStruct((nb, t, hw), BF16),
            jax.ShapeDtypeStruct((nb, t, hw), BF16),
            jax.ShapeDtypeStruct((nb, t, MLA_WIDTH), BF16),
        ],
        compiler_params=_cparams("parallel", "parallel"),
        name="mla_prep",
    )(p_mla, cos_t, sin_t, q_norm.reshape(1, -1), kv_norm.reshape(1, -1), wq, wq_rot, wk, e_rope, wv)


def _softmax_pv(s_list, v_list):
    m = s_list[0].max(axis=-1, keepdims=True)
    for s in s_list[1:]:
        m = jnp.maximum(m, s.max(axis=-1, keepdims=True))
    acc = None
    den = None
    for s, v in zip(s_list, v_list):
        p = jnp.exp(s - m)
        l = p.sum(axis=-1, keepdims=True)
        o = _dot(p.astype(BF16), v)
        acc = o if acc is None else acc + o
        den = l if den is None else den + l
    return acc / den


def _pair_select(o0, o1, half):
    lane = lax.broadcasted_iota(jnp.int32, o0.shape, 1)
    return jnp.where(lane < half, o0, o1)


def _mla_attn_kernel(q_ref, k_ref, v_ref, o_ref):
    def attend(nk):
        outs = []
        for j in range(2):
            sl = slice(j * MLA_HEAD_PAD, (j + 1) * MLA_HEAD_PAD)
            s = _dot_nt(q_ref[0, :, sl], k_ref[0, :nk, sl])
            outs.append(_softmax_pv([s], [v_ref[0, :nk, :]]))
        o_ref[0] = _pair_select(outs[0], outs[1], MLA_V).astype(BF16)

    is_ctx = pl.program_id(2) == 0

    @pl.when(is_ctx)
    def _():
        attend(CTX_LEN)

    @pl.when(jnp.logical_not(is_ctx))
    def _():
        attend(k_ref.shape[1])


def _mla_attention(q, k, v):
    nb, t, _ = q.shape
    pw = 2 * MLA_HEAD_PAD
    return pl.pallas_call(
        _mla_attn_kernel,
        grid=(nb, MLA_HEADS // 2, t // TOK_TILE),
        in_specs=[
            pl.BlockSpec((1, TOK_TILE, pw), lambda b, p, i: (b, i, p)),
            pl.BlockSpec((1, t, pw), lambda b, p, i: (b, 0, p)),
            pl.BlockSpec((1, t, 2 * MLA_V), lambda b, p, i: (b, 0, p)),
        ],
        out_specs=pl.BlockSpec((1, TOK_TILE, 2 * MLA_V), lambda b, p, i: (b, i, p)),
        out_shape=jax.ShapeDtypeStruct((nb, t, MLA_WIDTH), BF16),
        compiler_params=_cparams("parallel", "parallel", "arbitrary"),
        name="mla_attn",
    )(q, k, v)


def _na_tile_geometry(rows):
    n_tiles = rows // NA_ROWS_PER_TILE
    r0 = np.arange(n_tiles) * NA_ROWS_PER_TILE
    kb = np.clip(r0 - NA_KH // 2, 0, rows - NA_SPAN)
    pat = np.where(r0 == 0, 0, np.where(r0 == rows - NA_ROWS_PER_TILE, 2, 1))
    return kb, pat


def _na_bias_table(rpb, rows):
    nh, nd, _ = rpb.shape
    w = GRID_W
    kh = min(NA_KH, rows)
    edge = w - NA_KW
    r_ext = jnp.concatenate([jnp.broadcast_to(rpb[..., :1], (nh, nd, edge)), rpb,
                             jnp.broadcast_to(rpb[..., -1:], (nh, nd, edge)), jnp.zeros((nh, nd, 1), rpb.dtype)], axis=-1)
    toe = jnp.tile(r_ext, (1, 1, w))[..., :w * (2 * w - 1)].reshape(nh, nd, w, 2 * w - 1)[..., w - 1:]
    col = np.arange(w)
    col_start = np.clip(col - NA_KW // 2, 0, w - NA_KW)
    col_in = (col[None, :] >= col_start[:, None]) & (col[None, :] < col_start[:, None] + NA_KW)
    toe = jnp.where(col_in[None, None], toe, NEG_INF).astype(F32)
    masked = jnp.full((nh, w, w), NEG_INF, F32)
    kb, _ = _na_tile_geometry(rows)
    n_tiles = rows // NA_ROWS_PER_TILE
    pats = []
    for tile in (0, 1, n_tiles - 1):
        q_rows = []
        for dq in range(NA_ROWS_PER_TILE):
            r = tile * NA_ROWS_PER_TILE + dq
            start = int(np.clip(r - kh // 2, 0, rows - kh))
            blocks = []
            for dk in range(NA_SPAN):
                kr = int(kb[tile]) + dk
                blocks.append(toe[:, kr - r + NA_KH - 1] if start <= kr < start + kh else masked)
            q_rows.append(jnp.concatenate(blocks, axis=-1))
        pats.append(jnp.concatenate(q_rows, axis=1))
    return jnp.stack(pats, axis=1)


def _na_attn_kernel(kb_ref, pat_ref, q_ref, k_ref, v_ref, bias_ref, o_ref):
    i = pl.program_id(2)
    scale = NA_HEAD_DIM ** -0.5
    lane = lax.broadcasted_iota(jnp.int32, (TOK_TILE, 2 * NA_HEAD_DIM), 1)

    def head_q(j):
        q = q_ref[0].astype(F32)
        return jnp.where((lane < NA_HEAD_DIM) if j == 0 else (lane >= NA_HEAD_DIM), q, 0.0).astype(BF16)

    @pl.when(i == 0)
    def _():
        outs = []
        for j in range(2):
            s = _dot_nt(head_q(j), k_ref[0, :CTX_LEN, :]) * scale
            outs.append(_softmax_pv([s], [v_ref[0, :CTX_LEN, :]]))
        o_ref[0] = _pair_select(outs[0], outs[1], NA_HEAD_DIM).astype(BF16)

    @pl.when(i > 0)
    def _():
        start = pl.multiple_of(CTX_LEN + kb_ref[i - 1] * GRID_W, GRID_W)
        nk = NA_SPAN * GRID_W
        k_lat = k_ref[0, pl.ds(start, nk), :]
        v_lat = v_ref[0, pl.ds(start, nk), :]
        pat = pat_ref[i - 1]
        outs = []
        for j in range(2):
            qj = head_q(j)
            s_lat = _dot_nt(qj, k_lat) * scale + bias_ref[j, pat]
            s_ctx = _dot_nt(qj, k_ref[0, :CTX_LEN, :]) * scale
            outs.append(_softmax_pv([s_lat, s_ctx], [v_lat, v_ref[0, :CTX_LEN, :]]))
        o_ref[0] = _pair_select(outs[0], outs[1], NA_HEAD_DIM).astype(BF16)


def _na_attention(p_na, bias, rows):
    nb, t, _ = p_na.shape
    pw = 2 * NA_HEAD_DIM
    n_pairs = NA_HEADS // 2
    kb, pat = _na_tile_geometry(rows)
    nkeys = NA_SPAN * GRID_W
    grid_spec = pltpu.PrefetchScalarGridSpec(
        num_scalar_prefetch=2,
        grid=(n_pairs, nb, t // TOK_TILE),
        in_specs=[
            pl.BlockSpec((1, TOK_TILE, pw), lambda p, b, i, *_: (b, i, p)),
            pl.BlockSpec((1, t, pw), lambda p, b, i, *_: (b, 0, n_pairs + p)),
            pl.BlockSpec((1, t, pw), lambda p, b, i, *_: (b, 0, 2 * n_pairs + p)),
            pl.BlockSpec((2, 3, TOK_TILE, nkeys), lambda p, b, i, *_: (p, 0, 0, 0)),
        ],
        out_specs=pl.BlockSpec((1, TOK_TILE, pw), lambda p, b, i, *_: (b, i, p)),
    )
    return pl.pallas_call(
        _na_attn_kernel,
        grid_spec=grid_spec,
        out_shape=jax.ShapeDtypeStruct((nb, t, NA_WIDTH), BF16),
        compiler_params=_cparams("parallel", "parallel", "arbitrary"),
        name="na_attn",
    )(jnp.asarray(kb, jnp.int32), jnp.asarray(pat, jnp.int32), p_na, p_na, p_na, bias)


def _gla_direction(slab_ref, wg_ref, bg_ref, st_ref, o_ref, d):
    n = TOK_TILE
    n_chunks = n // GLA_CHUNK
    pw = 2 * GLA_DK
    row = lax.broadcasted_iota(jnp.int32, (n, n), 0)
    col = lax.broadcasted_iota(jnp.int32, (n, n), 1)
    shift = GLA_CHUNK.bit_length() - 1
    same_chunk = jnp.right_shift(row, shift) == jnp.right_shift(col, shift)
    causal = same_chunk & ((col <= row) if d == 0 else (col >= row))
    tri = jnp.where(causal, 1.0, 0.0).astype(F32)

    o_low = 2 * GLA_K_WIDTH + 2 * GLA_V_WIDTH
    gw = GLA_K_WIDTH
    g_raw = _dot_exact(slab_ref[0, :, o_low:o_low + LANES], wg_ref[:, d * gw:(d + 1) * gw]) + bg_ref[:, d * gw:(d + 1) * gw]
    g = jax.nn.log_sigmoid(g_raw) / GLA_GATE_NORM
    b = _dot_exact(tri, g)
    last = GLA_CHUNK - 1 if d == 0 else 0
    tot = jnp.concatenate([jnp.broadcast_to(b[c * GLA_CHUNK + last:c * GLA_CHUNK + last + 1], (GLA_CHUNK, b.shape[1]))
                           for c in range(n_chunks)], axis=0)
    e_b = jnp.exp(b)
    e_nb = jnp.exp(-b)
    e_end = jnp.exp(tot - b)
    decay = jnp.exp(tot)
    lane = lax.broadcasted_iota(jnp.int32, (n, pw), 1)
    chunk_order = range(n_chunks) if d == 0 else range(n_chunks - 1, -1, -1)

    for p in range(GLA_HEADS // 2):
        sl = slice(p * pw, (p + 1) * pw)
        q_in = slab_ref[0, :, sl] * e_b[:, sl] * (GLA_DK ** -0.5)
        k_p = slab_ref[0, :, GLA_K_WIDTH + p * pw:GLA_K_WIDTH + (p + 1) * pw]
        k_in = (k_p * e_nb[:, sl]).astype(BF16)
        k_end = (k_p * e_end[:, sl]).astype(BF16)
        for j in range(2):
            h = 2 * p + j
            q_h = jnp.where((lane < GLA_DK) if j == 0 else (lane >= GLA_DK), q_in, 0.0).astype(BF16)
            v_h = slab_ref[0, :, 2 * GLA_K_WIDTH + h * GLA_DV:2 * GLA_K_WIDTH + (h + 1) * GLA_DV].astype(BF16)
            a = jnp.where(causal, _dot_nt(q_h, k_in), 0.0)
            o_intra = _dot(a.astype(BF16), v_h)
            for c in chunk_order:
                rs = slice(c * GLA_CHUNK, (c + 1) * GLA_CHUNK)
                st = st_ref[d, h]
                o_ref[0, rs, h * GLA_DV:(h + 1) * GLA_DV] = o_intra[rs] + _dot_nt(q_h[rs], st.astype(BF16))
                st_ref[d, h] = st * decay[c * GLA_CHUNK:c * GLA_CHUNK + 1, sl] + _dot_tn(v_h[rs], k_end[rs])


def _gla_kernel(f_ref, b_ref, wg_ref, bg_ref, of_ref, ob_ref, st_ref):
    @pl.when(pl.program_id(1) == 0)
    def _():
        st_ref[...] = jnp.zeros_like(st_ref)

    _gla_direction(f_ref, wg_ref, bg_ref, st_ref, of_ref, 0)
    _gla_direction(b_ref, wg_ref, bg_ref, st_ref, ob_ref, 1)


def _gla_scan(p_gla, wg, bg):
    nb, t, _ = p_gla.shape
    n_tiles = t // TOK_TILE
    fwd = lambda b, s: (b, s, 0)
    bwd = lambda b, s: (b, jnp.where(s == 0, 0, n_tiles - s), 0)
    return pl.pallas_call(
        _gla_kernel,
        grid=(nb, n_tiles),
        in_specs=[
            pl.BlockSpec((1, TOK_TILE, GLA_COLS), fwd),
            pl.BlockSpec((1, TOK_TILE, GLA_COLS), bwd),
            pl.BlockSpec((LANES, 2 * GLA_K_WIDTH), lambda b, s: (0, 0)),
            pl.BlockSpec((1, 2 * GLA_K_WIDTH), lambda b, s: (0, 0)),
        ],
        out_specs=[
            pl.BlockSpec((1, TOK_TILE, GLA_V_WIDTH), fwd),
            pl.BlockSpec((1, TOK_TILE, GLA_V_WIDTH), bwd),
        ],
        out_shape=[jax.ShapeDtypeStruct((nb, t, GLA_V_WIDTH), F32)] * 2,
        scratch_shapes=[pltpu.VMEM((2, GLA_HEADS, GLA_DV, 2 * GLA_DK), F32)],
        compiler_params=_cparams("parallel", "arbitrary"),
        name="gla_scan",
    )(p_gla, p_gla, wg, bg)


def _merge_kernel(x_ref, mod_ref, oa_ref, ob_ref, gf_ref, gb_ref, og_ref, gate_ref, gn_ref, n2_ref,
                  wa_ref, wb_ref, wg_ref, wo_ref, x_out_ref, h_out_ref):
    og = og_ref[0]
    o_raw = gf_ref[0] + gb_ref[0]
    parts = []
    for h in range(GLA_HEADS):
        sl = slice(h * GLA_DV, (h + 1) * GLA_DV)
        parts.append(_rms(o_raw[:, sl], gn_ref[...]) * jax.nn.silu(og[:, sl]))
    o_g = jnp.concatenate(parts, axis=-1).astype(BF16)
    d = D_MODEL
    y = (jax.nn.sigmoid(gate_ref[0, :, 0:d]) * _dot(oa_ref[0], wa_ref[...])
         + jax.nn.sigmoid(gate_ref[0, :, d:2 * d]) * _dot(ob_ref[0], wb_ref[...])
         + jax.nn.sigmoid(gate_ref[0, :, 2 * d:3 * d]) * _dot(o_g, wg_ref[...]))
    x = x_ref[0] + mod_ref[0, 2:3, :] * _dot(y.astype(BF16), wo_ref[...])
    x_out_ref[0] = x
    h2 = _rms(x, n2_ref[...]) * (1.0 + mod_ref[0, 4:5, :]) + mod_ref[0, 3:4, :]
    h_out_ref[0] = h2.astype(BF16)


def _merge(xs, mod, o_a, o_b, gla_f, gla_b, p_gla, p_gate, gla_norm, norm2, wa, wb, wg, wo, keep_ctx):
    nb, t, d = xs.shape
    skip = 0 if keep_ctx else 1
    t_out = t - skip * TOK_TILE
    src = lambda b, i: (b, i + skip, 0)
    tok = lambda b, i: (b, i, 0)
    const = lambda b, i: (0, 0)
    og_block = (2 * GLA_K_WIDTH) // GLA_V_WIDTH + 1
    return pl.pallas_call(
        _merge_kernel,
        grid=(nb, t_out // TOK_TILE),
        in_specs=[
            pl.BlockSpec((1, TOK_TILE, d), src),
            pl.BlockSpec((1, N_MOD, d), _mod_row(nb, keep_ctx)),
            pl.BlockSpec((1, TOK_TILE, MLA_WIDTH), src),
            pl.BlockSpec((1, TOK_TILE, NA_WIDTH), src),
            pl.BlockSpec((1, TOK_TILE, GLA_V_WIDTH), src),
            pl.BlockSpec((1, TOK_TILE, GLA_V_WIDTH), src),
            pl.BlockSpec((1, TOK_TILE, GLA_V_WIDTH), lambda b, i: (b, i + skip, og_block)),
            pl.BlockSpec((1, TOK_TILE, GATE_COLS), src),
            pl.BlockSpec((1, GLA_DV), const),
            pl.BlockSpec((1, d), const),
            pl.BlockSpec((MLA_WIDTH, d), const),
            pl.BlockSpec((NA_WIDTH, d), const),
            pl.BlockSpec((GLA_V_WIDTH, d), const),
            pl.BlockSpec((d, d), const),
        ],
        out_specs=[pl.BlockSpec((1, TOK_TILE, d), tok), pl.BlockSpec((1, TOK_TILE, d), tok)],
        out_shape=[jax.ShapeDtypeStruct((nb, t_out, d), F32), jax.ShapeDtypeStruct((nb, t_out, d), BF16)],
        compiler_params=_cparams("parallel", "parallel"),
        name="merge",
    )(xs, mod, o_a, o_b, gla_f, gla_b, p_gla, p_gate, gla_norm.reshape(1, -1), norm2.reshape(1, -1),
      wa, wb, wg, wo)


def _top_ranked(x, k, want_rank=True):
    row = lax.broadcasted_iota(jnp.int32, x.shape, 0).astype(F32)
    out_row = lax.broadcasted_iota(jnp.int32, (k, x.shape[1]), 0)
    vals = jnp.zeros((k, x.shape[1]), F32)
    rank = jnp.full(x.shape, float(k), F32)
    for r in range(k):
        m = jnp.max(x, axis=0, keepdims=True)
        vals = jnp.where(out_row == r, m, vals)
        hit = row == jnp.min(jnp.where(x == m, row, float(x.shape[0])), axis=0, keepdims=True)
        if want_rank:
            rank = jnp.where(hit, float(r), rank)
        x = jnp.where(hit, -jnp.inf, x)
    return vals, rank


def _count(mask):
    return jnp.sum(jnp.where(mask, 1.0, 0.0), axis=0, keepdims=True)


def _pack_rows(x):
    return pltpu.bitcast(x.astype(BF16), jnp.uint32)


def _peer_select_chunk(s0, s1):
    k = PEER_TOPK
    sub = lax.broadcasted_iota(jnp.int32, (SUBLANES, LANES), 0)
    v0, rank0 = _top_ranked(s0, k)
    v1, rank1 = _top_ranked(s1, k)
    cand = [v0[0:1] + v1]
    for a in range(1, k):
        cand.append(jnp.where(sub < k // (a + 1), v0[a:a + 1] + v1[:SUBLANES], -jnp.inf))
    top, _ = _top_ranked(jnp.concatenate(cand, axis=0), k, want_rank=False)
    thr = top[k - 1:k]
    z = jnp.sum(jnp.exp(top - top[0:1]), axis=0, keepdims=True)
    above = [_count(ca > thr) for ca in cand]
    equal = [_count(ca == thr) for ca in cand]
    spare = float(k) - functools.reduce(jnp.add, above)
    n0 = jnp.zeros_like(s0)
    for a in range(k):
        take = jnp.clip(spare, 0.0, equal[a])
        spare = spare - equal[a]
        n0 = jnp.where(rank0 == float(a), above[a] + take, n0)
    return n0, jnp.exp(s0 - v0[0:1]) / z, _pack_rows(rank1), _pack_rows(jnp.exp(s1 - v1[0:1]))


def _sorting_network(n):
    pairs = []
    p = 1
    while p < n:
        k = p
        while k >= 1:
            for j in range(k % p, n - k, 2 * k):
                for i in range(min(k, n - j - k)):
                    if (i + j) // (2 * p) == (i + j + k) // (2 * p):
                        pairs.append((i + j, i + j + k))
            k //= 2
        p *= 2
    return pairs


def _sublane_sum(x):
    for shift in (4, 2, 1):
        x = x + pltpu.roll(x, shift, axis=0)
    return x


def _top_sorted(blocks, k):
    def exchange(xs, i, j):
        xs[i], xs[j] = jnp.maximum(xs[i], xs[j]), jnp.minimum(xs[i], xs[j])

    xs = list(blocks[:k])
    for i, j in _sorting_network(k):
        exchange(xs, i, j)
    for extra in blocks[k:]:
        for r in range(k):
            xs[r], extra = jnp.maximum(xs[r], extra), jnp.minimum(xs[r], extra)
    for shift in (4, 2, 1):
        ys = [pltpu.roll(x, shift, axis=0) for x in xs]
        xs = [jnp.maximum(xs[r], ys[k - 1 - r]) for r in range(k)]
        stride = k // 2
        while stride >= 1:
            for r in range(k):
                if r & stride == 0:
                    exchange(xs, r, r + stride)
            stride //= 2
    return xs


def _peer_select_chunk_distinct(s0, s1):
    k = PEER_TOPK
    nb = s0.shape[0] // SUBLANES
    sub = lax.broadcasted_iota(jnp.int32, (SUBLANES, LANES), 0)
    blocks0 = [s0[SUBLANES * i:SUBLANES * (i + 1)] for i in range(nb)]
    blocks1 = [s1[SUBLANES * i:SUBLANES * (i + 1)] for i in range(nb)]
    v0 = _top_sorted(blocks0, k)
    v1 = _top_sorted(blocks1, k)

    def ties(v, blocks):
        same = functools.reduce(jnp.add, [jnp.where(v[r] == v[r + 1], 1.0, 0.0) for r in range(k - 1)])
        reach = _sublane_sum(functools.reduce(jnp.add, [jnp.where(b >= v[k - 1], 1.0, 0.0) for b in blocks]))
        return same + jnp.abs(reach - float(k))

    v1_lo, v1_hi = v1[SUBLANES - 1], v1[2 * SUBLANES - 1]
    for b in range(SUBLANES - 2, -1, -1):
        v1_lo = jnp.where(sub == b, v1[b], v1_lo)
        v1_hi = jnp.where(sub == b, v1[SUBLANES + b], v1_hi)
    cand = [[v0[0] + v1_lo, v0[0] + v1_hi]]
    for a in range(1, k):
        cand.append([jnp.where(sub < k // (a + 1), v0[a] + v1_lo, -jnp.inf)])
    cand_blocks = [c for ca in cand for c in ca]
    top = _top_sorted(cand_blocks[1:] + cand_blocks[:1], k)
    thr = top[k - 1]
    z = functools.reduce(jnp.add, [jnp.exp(t - top[0]) for t in top])
    n_sel = [_sublane_sum(functools.reduce(jnp.add, [jnp.where(c >= thr, 1.0, 0.0) for c in ca])) for ca in cand]
    redo = ties(v0, blocks0) + ties(v1, blocks1) + ties(top, cand_blocks)

    n0, e0, rank1, e1 = [], [], [], []
    for x0, x1 in zip(blocks0, blocks1):
        n = jnp.zeros_like(x0)
        for a in range(k):
            n = jnp.where(x0 == v0[a], n_sel[a], n)
        n0.append(n)
        e0.append(jnp.exp(x0 - v0[0]) / z)
        rank1.append(functools.reduce(jnp.add, [jnp.where(v > x1, 1.0, 0.0) for v in v1]))
        e1.append(jnp.exp(x1 - v1[0]))
    cat = lambda parts: jnp.concatenate(parts, axis=0)
    return (cat(n0), cat(e0), _pack_rows(cat(rank1)), _pack_rows(cat(e1))), redo[0:1]


def _peer_select_kernel(h_ref, wqt_ref, kp_ref, ht_ref, n0_ref, e0_ref, r1_ref, e1_ref, s_ref):
    h = h_ref[...]
    ht_ref[...] = h.T
    q_t = _dot_nt(wqt_ref[...], h)
    nk = PEER_N_KEYS
    for hd in range(PEER_HEADS):
        s_ref[hd] = _dot(kp_ref[...], q_t[hd * PEER_QUERY_DIM:(hd + 1) * PEER_QUERY_DIM].astype(BF16))

    def run(exact):
        redo = jnp.zeros((1, LANES), F32)
        for hd in range(PEER_HEADS):
            for c in range(h.shape[0] // LANES):
                cs = slice(c * LANES, (c + 1) * LANES)
                s0, s1 = s_ref[hd, :nk, cs], s_ref[hd, nk:, cs]
                if exact:
                    outs = _peer_select_chunk(s0, s1)
                else:
                    outs, flag = _peer_select_chunk_distinct(s0, s1)
                    redo = jnp.maximum(redo, flag)
                for ref, val in zip((n0_ref, e0_ref, r1_ref, e1_ref), outs):
                    ref[hd, :, cs] = val
        return redo

    redo = run(exact=False)

    @pl.when(jnp.max(redo) > 0.0)
    def _():
        run(exact=True)


def _peer_select(h2, wq_t, key_pair):
    n, d = h2.shape
    tm = TOK_TILE
    fac = lambda dt: jax.ShapeDtypeStruct((PEER_HEADS, PEER_N_KEYS // (1 if dt == F32 else 2), n), dt)
    fac_spec = pl.BlockSpec((PEER_HEADS, PEER_N_KEYS, tm), lambda i: (0, 0, i))
    packed_spec = pl.BlockSpec((PEER_HEADS, PEER_N_KEYS // 2, tm), lambda i: (0, 0, i))
    return pl.pallas_call(
        _peer_select_kernel,
        grid=(n // tm,),
        in_specs=[
            pl.BlockSpec((tm, d), lambda i: (i, 0)),
            pl.BlockSpec(wq_t.shape, lambda i: (0, 0)),
            pl.BlockSpec(key_pair.shape, lambda i: (0, 0)),
        ],
        out_specs=[pl.BlockSpec((d, tm), lambda i: (0, i)), fac_spec, fac_spec, packed_spec, packed_spec],
        out_shape=[jax.ShapeDtypeStruct((d, n), BF16), fac(F32), fac(F32), fac(jnp.uint32), fac(jnp.uint32)],
        scratch_shapes=[pltpu.VMEM((PEER_HEADS, 2 * PEER_N_KEYS, tm), F32)],
        compiler_params=_cparams("parallel"),
        name="peer_select",
    )(h2, wq_t, key_pair)


def _peer_dense_kernel(ht_ref, u_ref, vt_ref, n0_ref, e0_ref, r1_ref, e1_ref, o_ref, g_ref, act_ref):
    nk = PEER_N_KEYS
    piece = PEER_PIECE
    n_pieces = PEER_EXP_TILE // piece
    tm = ht_ref.shape[1]

    @pl.when(pl.program_id(1) == 0)
    def _():
        o_ref[...] = jnp.zeros_like(o_ref)

    def gate_piece(p):
        rs_n = nk // 2
        zero = jnp.zeros((rs_n, LANES), BF16)
        for sub in range(piece // nk):
            ii = p * (piece // nk) + sub
            for cs in range(0, tm, LANES):
                for rs in range(0, nk, rs_n):
                    w = zero
                    for hd in range(PEER_HEADS):
                        n_sel = jnp.broadcast_to(n0_ref[hd, ii:ii + 1, cs:cs + LANES], (rs_n, LANES)).astype(BF16)
                        e0 = jnp.broadcast_to(e0_ref[hd, ii:ii + 1, cs:cs + LANES], (rs_n, LANES)).astype(BF16)
                        r1 = pltpu.bitcast(r1_ref[hd, rs // 2:(rs + rs_n) // 2, cs:cs + LANES], BF16)
                        e1 = pltpu.bitcast(e1_ref[hd, rs // 2:(rs + rs_n) // 2, cs:cs + LANES], BF16)
                        w = w + jnp.where(r1 < n_sel, e1 * e0, zero)
                    rows = slice(sub * nk + rs, sub * nk + rs + rs_n)
                    act = act_ref[p, rows, cs:cs + LANES]
                    g_ref[p, rows, cs:cs + LANES] = w * jax.nn.gelu(act).astype(BF16)

    for p in range(n_pieces):
        act_ref[p] = _dot(u_ref[0, p * piece:(p + 1) * piece, :], ht_ref[...])
    for p in range(n_pieces):
        gate_piece(p)
        o_ref[...] += _dot(vt_ref[0, :, p * piece:(p + 1) * piece], g_ref[p])


def _peer_dense(h_t, u, v_t, layer, n0, e0, r1, e1):
    d, n = h_t.shape
    tm, te = PEER_TOK_TILE, PEER_EXP_TILE
    sub = te // PEER_N_KEYS
    return pl.pallas_call(
        _peer_dense_kernel,
        grid=(n // tm, PEER_N_EXPERTS // te),
        in_specs=[
            pl.BlockSpec((d, tm), lambda i, e: (0, i)),
            pl.BlockSpec((1, te, d), lambda i, e: (layer, e, 0)),
            pl.BlockSpec((1, d, te), lambda i, e: (layer, 0, e)),
            pl.BlockSpec((PEER_HEADS, sub, tm), lambda i, e: (0, e, i)),
            pl.BlockSpec((PEER_HEADS, sub, tm), lambda i, e: (0, e, i)),
            pl.BlockSpec((PEER_HEADS, PEER_N_KEYS // 2, tm), lambda i, e: (0, 0, i)),
            pl.BlockSpec((PEER_HEADS, PEER_N_KEYS // 2, tm), lambda i, e: (0, 0, i)),
        ],
        out_specs=pl.BlockSpec((d, tm), lambda i, e: (0, i)),
        out_shape=jax.ShapeDtypeStruct((d, n), F32),
        scratch_shapes=[pltpu.VMEM((te // PEER_PIECE, PEER_PIECE, tm), BF16),
                        pltpu.VMEM((te // PEER_PIECE, PEER_PIECE, tm), F32)],
        compiler_params=_cparams("parallel", "arbitrary"),
        name="peer_dense",
    )(h_t, u, v_t, n0, e0, r1, e1)


def _peer_residual_kernel(x_ref, mod_ref, ot_ref, fn_ref, o_ref, *, final):
    x = x_ref[0] + mod_ref[0, 5:6, :] * ot_ref[...].T
    o_ref[0] = _rms(x, fn_ref[...]) if final else x


def _peer_residual(xs, mod, out_t, final_norm, final):
    nb, t, d = xs.shape
    n_tiles = t // TOK_TILE
    tok = lambda b, i: (b, i, 0)
    return pl.pallas_call(
        functools.partial(_peer_residual_kernel, final=final),
        grid=(nb, n_tiles),
        in_specs=[
            pl.BlockSpec((1, TOK_TILE, d), tok),
            pl.BlockSpec((1, N_MOD, d), _mod_row(nb, has_ctx=not final)),
            pl.BlockSpec((d, TOK_TILE), lambda b, i: (0, b * n_tiles + i)),
            pl.BlockSpec((1, d), lambda b, i: (0, 0)),
        ],
        out_specs=pl.BlockSpec((1, TOK_TILE, d), tok),
        out_shape=jax.ShapeDtypeStruct((nb, t, d), F32),
        compiler_params=_cparams("parallel", "parallel"),
        name="peer_residual",
    )(xs, mod, out_t, final_norm.reshape(1, d))


def _rot_cols(w):
    q = MLA_ROPE // 4
    return jnp.concatenate([-w[..., q:2 * q], w[..., :q], -w[..., 3 * q:], w[..., 2 * q:3 * q]], axis=-1)


def _rope_slab(w):
    return jnp.pad(w, ((0, 0), (MLA_NOPE, LANES - MLA_NOPE - MLA_ROPE)))


def _prep_w_in(w_in):
    o = MLA_Q_LORA + MLA_KV_LORA
    w_kr = w_in[:, o:o + MLA_ROPE]
    o2 = o + MLA_ROPE
    n_mid = NA_COLS + 2 * GLA_K_WIDTH + 2 * GLA_V_WIDTH
    low = w_in[:, o2 + n_mid:o2 + n_mid + 2 * GLA_GATE_RANK]
    parts = [w_in[:, :o], _rope_slab(w_kr), _rope_slab(_rot_cols(w_kr)), w_in[:, o2:o2 + n_mid],
             jnp.pad(low, ((0, 0), (0, LANES - 2 * GLA_GATE_RANK))), w_in[:, o2 + n_mid + 2 * GLA_GATE_RANK:]]
    return jnp.concatenate(parts, axis=1).astype(BF16)


def _prep_mla(w_uq, w_ukv):
    r = w_uq.shape[0]
    wq = w_uq.reshape(r, MLA_HEADS, MLA_NOPE + MLA_ROPE)
    pad = MLA_HEAD_PAD - MLA_NOPE - MLA_ROPE
    wq_p = jnp.pad(wq, ((0, 0), (0, 0), (0, pad)))
    wq_rot = jnp.pad(_rot_cols(wq[..., MLA_NOPE:]), ((0, 0), (0, 0), (MLA_NOPE, pad)))
    c = w_ukv.shape[0]
    wkv = w_ukv.reshape(c, MLA_HEADS, MLA_NOPE + MLA_V)
    wk = jnp.pad(wkv[..., :MLA_NOPE], ((0, 0), (0, 0), (0, MLA_HEAD_PAD - MLA_NOPE)))
    wv = wkv[..., MLA_NOPE:]
    hw = MLA_HEADS * MLA_HEAD_PAD
    lane = np.arange(LANES)
    col = np.arange(hw)
    is_rope = (lane >= MLA_NOPE) & (lane < MLA_NOPE + MLA_ROPE)
    e = (is_rope[:, None] & (col[None, :] % MLA_HEAD_PAD == lane[:, None])).astype(np.float32)
    return (wq_p.reshape(r, hw).astype(BF16), wq_rot.reshape(r, hw).astype(BF16),
            wk.reshape(c, hw).astype(BF16), jnp.asarray(e, BF16), wv.reshape(c, MLA_WIDTH).astype(BF16))


def _rope_tables(seq):
    pos = jnp.arange(seq)
    rows = (pos // GRID_W).astype(F32)
    cols = (pos % GRID_W).astype(F32)
    half = MLA_ROPE // 2
    inv = ROPE_THETA ** (-jnp.arange(0, half, 2, dtype=F32) / half)
    ar = rows[:, None] * inv
    ac = cols[:, None] * inv
    ang = jnp.concatenate([ar, ar, ac, ac], axis=-1)
    ang = jnp.concatenate([jnp.zeros((CTX_LEN, MLA_ROPE), F32), ang], axis=0)
    t = ang.shape[0]
    ones = jnp.ones((t, MLA_NOPE), F32)
    zeros = jnp.zeros((t, LANES - MLA_NOPE - MLA_ROPE), F32)
    cos_t = jnp.concatenate([ones, jnp.cos(ang), zeros], axis=1)
    sin_t = jnp.concatenate([0.0 * ones, jnp.sin(ang), zeros], axis=1)
    return cos_t, sin_t


def _prep_gla_gate(w_f, b_f, w_b, b_b):
    wg = jnp.zeros((LANES, 2 * GLA_K_WIDTH), F32)
    wg = wg.at[:GLA_GATE_RANK, :GLA_K_WIDTH].set(w_f)
    wg = wg.at[GLA_GATE_RANK:2 * GLA_GATE_RANK, GLA_K_WIDTH:].set(w_b)
    return wg, jnp.concatenate([b_f, b_b]).reshape(1, -1)


def _prep_peer_keys(sub_keys):
    half = PEER_QUERY_DIM // 2
    kp = jnp.zeros((2 * PEER_N_KEYS, PEER_QUERY_DIM), F32)
    kp = kp.at[:PEER_N_KEYS, :half].set(sub_keys[0])
    kp = kp.at[PEER_N_KEYS:, half:].set(sub_keys[1])
    return kp.astype(BF16)


def kernel(x, c, ctx, c_ctx, w_ada, b_ada, norm1, w_in, mla_q_norm, mla_w_uq, mla_kv_norm, mla_w_ukv, na_rpb, gla_w_gk_fwd, gla_b_gk_fwd, gla_w_gk_bwd, gla_b_gk_bwd, gla_norm, w_o_mla, w_o_na, w_o_gla, w_out, norm2, peer_w_q, peer_sub_keys, peer_u, peer_v, final_norm):
    nb, seq, d = x.shape
    assert ctx.shape[1] == CTX_LEN == TOK_TILE and seq % (NA_ROWS_PER_TILE * GRID_W) == 0
    assert NA_ROWS_PER_TILE * GRID_W == TOK_TILE and nb < 8
    rows = seq // GRID_W
    depth = w_ada.shape[0]

    xs = jnp.concatenate([ctx, x], axis=1)
    t = xs.shape[1]
    cc = jnp.zeros((8, d), F32).at[:nb].set(c).at[nb].set(c_ctx)
    mods = _modulation(cc, w_ada, b_ada).reshape(depth, 8, N_MOD, d)
    cos_t, sin_t = _rope_tables(seq)
    u_all = peer_u.astype(BF16)
    vt_all = jnp.swapaxes(peer_v, 1, 2).astype(BF16)

    for l in range(depth):
        mod = mods[l]
        p_mla, p_na, p_gla, p_gate = _in_projection(xs, mod, norm1[l], _prep_w_in(w_in[l]))
        q, k, v = _mla_prep(p_mla, cos_t, sin_t, mla_q_norm[l], mla_kv_norm[l], *_prep_mla(mla_w_uq[l], mla_w_ukv[l]))
        o_a = _mla_attention(q, k, v)
        o_b = _na_attention(p_na, _na_bias_table(na_rpb[l], rows), rows)
        gla_f, gla_b = _gla_scan(p_gla, *_prep_gla_gate(gla_w_gk_fwd[l], gla_b_gk_fwd[l], gla_w_gk_bwd[l], gla_b_gk_bwd[l]))
        last = l == depth - 1
        xs, h2 = _merge(xs, mod, o_a, o_b, gla_f, gla_b, p_gla, p_gate, gla_norm[l], norm2[l],
                        w_o_mla[l].astype(BF16), w_o_na[l].astype(BF16), w_o_gla[l].astype(BF16), w_out[l].astype(BF16),
                        keep_ctx=not last)
        h_t, n0, e0, r1, e1 = _peer_select(h2.reshape(-1, d), peer_w_q[l].T.astype(BF16),
                                           _prep_peer_keys(peer_sub_keys[l]))
        out_t = _peer_dense(h_t, u_all, vt_all, l, n0, e0, r1, e1)
        xs = _peer_residual(xs, mod, out_t, final_norm, final=last)
    return xs
```

```python
import functools

import numpy as np
import jax
import jax.numpy as jnp
from jax import lax
from jax.experimental import pallas as pl
from jax.experimental.pallas import tpu as pltpu

F32 = jnp.float32
BF16 = jnp.bfloat16
HIGHEST = lax.Precision.HIGHEST

D_MODEL = 1024
DEPTH = 2
CTX_LEN = 256
GRID_W = 64
EPS = 1e-6
N_MOD = 6
NEG_INF = -1e30

MLA_HEADS = 8
MLA_Q_LORA = 384
MLA_KV_LORA = 256
MLA_NOPE = 64
MLA_ROPE = 32
MLA_V = 64
ROPE_THETA = 10000.0
MLA_HEAD_PAD = 128
MLA_PAIRS_PER_STEP = 4

NA_HEADS = 8
NA_HEAD_DIM = 64
NA_KH = 8
NA_KW = 16
NA_ROWS_PER_TILE = 4
NA_SPAN = NA_ROWS_PER_TILE + NA_KH
NA_PAIRS_PER_STEP = 2

GLA_HEADS = 4
GLA_DK = 64
GLA_DV = 128
GLA_GATE_RANK = 16
GLA_GATE_NORM = 16.0
GLA_CHUNK = 64

PEER_HEADS = 8
PEER_N_KEYS = 128
PEER_N_EXPERTS = PEER_N_KEYS * PEER_N_KEYS
PEER_QUERY_DIM = 128
PEER_TOPK = 16
PEER_TOK_TILE = 512
PEER_EXP_TILE = 1024
PEER_PIECE = 256

MLA_WIDTH = MLA_HEADS * MLA_V
NA_WIDTH = NA_HEADS * NA_HEAD_DIM
GLA_K_WIDTH = GLA_HEADS * GLA_DK
GLA_V_WIDTH = GLA_HEADS * GLA_DV

LANES = 128
SUBLANES = 8
TOK_TILE = 256

MLA_COLS = MLA_Q_LORA + MLA_KV_LORA + 2 * LANES
NA_COLS = 3 * NA_WIDTH
GLA_COLS = 2 * GLA_K_WIDTH + 2 * GLA_V_WIDTH + LANES
GATE_COLS = 3 * D_MODEL
IN_COLS_PAD = MLA_COLS + NA_COLS + GLA_COLS + GATE_COLS

VMEM_LIMIT = 56 * 1024 * 1024


def _cparams(*sem, flags=None):
    return pltpu.CompilerParams(dimension_semantics=sem, vmem_limit_bytes=VMEM_LIMIT, flags=flags)


def _rms(x, w):
    return x * lax.rsqrt(jnp.mean(x * x, axis=-1, keepdims=True) + EPS) * w


def _dot(a, b):
    return jnp.dot(a, b, preferred_element_type=F32)


def _dot_nt(a, b):
    return lax.dot_general(a, b, (((1,), (1,)), ((), ())), preferred_element_type=F32)


def _dot_tn(a, b):
    return lax.dot_general(a, b, (((0,), (0,)), ((), ())), preferred_element_type=F32)


def _dot_exact(a, b):
    return jnp.dot(a, b, preferred_element_type=F32, precision=HIGHEST)


def _mod_kernel(c_ref, w_ref, b_ref, o_ref):
    c = c_ref[...]
    o_ref[0] = _dot_exact(c * jax.nn.sigmoid(c), w_ref[0]) + b_ref[0]


def _modulation(cc, w_ada, b_ada):
    depth, d, n = w_ada.shape
    tn = 1536
    return pl.pallas_call(
        _mod_kernel,
        grid=(depth, n // tn),
        in_specs=[
            pl.BlockSpec((8, d), lambda l, j: (0, 0)),
            pl.BlockSpec((1, d, tn), lambda l, j: (l, 0, j)),
            pl.BlockSpec((1, 1, tn), lambda l, j: (l, 0, j)),
        ],
        out_specs=pl.BlockSpec((1, 8, tn), lambda l, j: (l, 0, j)),
        out_shape=jax.ShapeDtypeStruct((depth, 8, n), F32),
        compiler_params=_cparams("parallel", "parallel"),
        name="adaln_mod",
    )(cc, w_ada, b_ada.reshape(depth, 1, n))


def _mod_row(n_batch, has_ctx=True):
    if not has_ctx:
        return lambda b, i: (b, 0, 0)
    return lambda b, i: (jnp.where(i == 0, n_batch, b), 0, 0)


def _inproj_kernel(x_ref, mod_ref, nw_ref, w_ref, mla_ref, na_ref, gla_ref, gate_ref):
    h = _rms(x_ref[0], nw_ref[...]) * (1.0 + mod_ref[0, 1:2, :]) + mod_ref[0, 0:1, :]
    h = h.astype(BF16)
    o = 0
    mla_ref[0] = _dot(h, w_ref[:, o:o + MLA_COLS])
    o += MLA_COLS
    na_ref[0] = _dot(h, w_ref[:, o:o + NA_COLS]).astype(BF16)
    o += NA_COLS
    gla_ref[0] = _dot(h, w_ref[:, o:o + GLA_COLS])
    o += GLA_COLS
    gate_ref[0] = _dot(h, w_ref[:, o:o + GATE_COLS])


def _in_projection(xs, mod, norm_w, w_in_p):
    nb, t, d = xs.shape
    tok = lambda b, i: (b, i, 0)
    return pl.pallas_call(
        _inproj_kernel,
        grid=(nb, t // TOK_TILE),
        in_specs=[
            pl.BlockSpec((1, TOK_TILE, d), tok),
            pl.BlockSpec((1, N_MOD, d), _mod_row(nb)),
            pl.BlockSpec((1, d), lambda b, i: (0, 0)),
            pl.BlockSpec((d, IN_COLS_PAD), lambda b, i: (0, 0), pipeline_mode=pl.Buffered(1)),
        ],
        out_specs=[
            pl.BlockSpec((1, TOK_TILE, MLA_COLS), tok),
            pl.BlockSpec((1, TOK_TILE, NA_COLS), tok),
            pl.BlockSpec((1, TOK_TILE, GLA_COLS), tok),
            pl.BlockSpec((1, TOK_TILE, GATE_COLS), tok),
        ],
        out_shape=[
            jax.ShapeDtypeStruct((nb, t, MLA_COLS), F32),
            jax.ShapeDtypeStruct((nb, t, NA_COLS), BF16),
            jax.ShapeDtypeStruct((nb, t, GLA_COLS), F32),
            jax.ShapeDtypeStruct((nb, t, GATE_COLS), F32),
        ],
        compiler_params=_cparams("parallel", "parallel"),
        name="in_proj",
    )(xs, mod, norm_w.reshape(1, d), w_in_p)


def _mla_prep_kernel(p_ref, cos_ref, sin_ref, qn_ref, kvn_ref, wq_ref, wqr_ref, wk_ref, e_ref, wv_ref,
                     q_ref, k_ref, v_ref):
    scale = (MLA_NOPE + MLA_ROPE) ** -0.5
    cos = cos_ref[...]
    sin = sin_ref[...]
    o = MLA_Q_LORA + MLA_KV_LORA
    nq = _rms(p_ref[0, :, :MLA_Q_LORA], qn_ref[...]).astype(BF16)
    nkv = _rms(p_ref[0, :, MLA_Q_LORA:o], kvn_ref[...]).astype(BF16)
    q = _dot(nq, wq_ref[...])
    q_rot = _dot(nq, wqr_ref[...])
    for h in range(MLA_HEADS):
        sl = slice(h * MLA_HEAD_PAD, (h + 1) * MLA_HEAD_PAD)
        q_ref[0, :, sl] = ((q[:, sl] * cos + q_rot[:, sl] * sin) * scale).astype(BF16)
    k_rope = (p_ref[0, :, o:o + LANES] * cos + p_ref[0, :, o + LANES:o + 2 * LANES] * sin).astype(BF16)
    k_ref[0] = (_dot(nkv, wk_ref[...]) + _dot(k_rope, e_ref[...])).astype(BF16)
    v_ref[0] = _dot(nkv, wv_ref[...]).astype(BF16)


def _mla_prep(p_mla, cos_t, sin_t, q_norm, kv_norm, wq, wq_rot, wk, e_rope, wv):
    nb, t, _ = p_mla.shape
    hw = MLA_HEADS * MLA_HEAD_PAD
    tok = lambda b, i: (b, i, 0)
    const = lambda b, i: (0, 0)
    return pl.pallas_call(
        _mla_prep_kernel,
        grid=(nb, t // TOK_TILE),
        in_specs=[
            pl.BlockSpec((1, TOK_TILE, MLA_COLS), tok),
            pl.BlockSpec((TOK_TILE, LANES), lambda b, i: (i, 0)),
            pl.BlockSpec((TOK_TILE, LANES), lambda b, i: (i, 0)),
            pl.BlockSpec((1, MLA_Q_LORA), const),
            pl.BlockSpec((1, MLA_KV_LORA), const),
            pl.BlockSpec((MLA_Q_LORA, hw), const),
            pl.BlockSpec((MLA_Q_LORA, hw), const),
            pl.BlockSpec((MLA_KV_LORA, hw), const),
            pl.BlockSpec((LANES, hw), const),
            pl.BlockSpec((MLA_KV_LORA, MLA_WIDTH), const),
        ],
        out_specs=[
            pl.BlockSpec((1, TOK_TILE, hw), tok),
            pl.BlockSpec((1, TOK_TILE, hw), tok),
            pl.BlockSpec((1, TOK_TILE, MLA_WIDTH), tok),
        ],
        out_shape=[
            jax.ShapeDtypeStruct((nb, t, hw), BF16),
            jax.ShapeDtypeStruct((nb, t, hw), BF16),
            jax.ShapeDtypeStruct((nb, t, MLA_WIDTH), BF16),
        ],
        compiler_params=_cparams("parallel", "parallel"),
        name="mla_prep",
    )(p_mla, cos_t, sin_t, q_norm.reshape(1, -1), kv_norm.reshape(1, -1), wq, wq_rot, wk, e_rope, wv)


def _softmax_pv(s_list, v_list):
    m = s_list[0].max(axis=-1, keepdims=True)
    for s in s_list[1:]:
        m = jnp.maximum(m, s.max(axis=-1, keepdims=True))
    acc = None
    den = None
    for s, v in zip(s_list, v_list):
        p = jnp.exp(s - m)
        l = p.sum(axis=-1, keepdims=True)
        o = _dot(p.astype(BF16), v)
        acc = o if acc is None else acc + o
        den = l if den is None else den + l
    return acc / den


def _pair_select(o0, o1, half):
    lane = lax.broadcasted_iota(jnp.int32, o0.shape, 1)
    return jnp.where(lane < half, o0, o1)


def _mla_attn_kernel(q_ref, k_ref, v_ref, o_ref):
    def attend(nk):
        for pr in range(MLA_PAIRS_PER_STEP):
            vs = slice(pr * 2 * MLA_V, (pr + 1) * 2 * MLA_V)
            outs = []
            for j in range(2):
                h = 2 * pr + j
                sl = slice(h * MLA_HEAD_PAD, (h + 1) * MLA_HEAD_PAD)
                s = _dot_nt(q_ref[0, :, sl], k_ref[0, :nk, sl])
                outs.append(_softmax_pv([s], [v_ref[0, :nk, vs]]))
            o_ref[0, :, vs] = _pair_select(outs[0], outs[1], MLA_V).astype(BF16)

    is_ctx = pl.program_id(2) == 0

    @pl.when(is_ctx)
    def _():
        attend(CTX_LEN)

    @pl.when(jnp.logical_not(is_ctx))
    def _():
        attend(k_ref.shape[1])


def _mla_attention(q, k, v):
    nb, t, _ = q.shape
    heads = 2 * MLA_PAIRS_PER_STEP
    pw = heads * MLA_HEAD_PAD
    return pl.pallas_call(
        _mla_attn_kernel,
        grid=(nb, MLA_HEADS // heads, t // TOK_TILE),
        in_specs=[
            pl.BlockSpec((1, TOK_TILE, pw), lambda b, p, i: (b, i, p)),
            pl.BlockSpec((1, t, pw), lambda b, p, i: (b, 0, p)),
            pl.BlockSpec((1, t, heads * MLA_V), lambda b, p, i: (b, 0, p)),
        ],
        out_specs=pl.BlockSpec((1, TOK_TILE, heads * MLA_V), lambda b, p, i: (b, i, p)),
        out_shape=jax.ShapeDtypeStruct((nb, t, MLA_WIDTH), BF16),
        compiler_params=_cparams("parallel", "parallel", "arbitrary"),
        name="mla_attn",
    )(q, k, v)


def _na_tile_geometry(rows):
    n_tiles = rows // NA_ROWS_PER_TILE
    r0 = np.arange(n_tiles) * NA_ROWS_PER_TILE
    kb = np.clip(r0 - NA_KH // 2, 0, rows - NA_SPAN)
    pat = np.where(r0 == 0, 0, np.where(r0 == rows - NA_ROWS_PER_TILE, 2, 1))
    return kb, pat


def _na_bias_table(rpb, rows):
    nh, nd, _ = rpb.shape
    w = GRID_W
    kh = min(NA_KH, rows)
    edge = w - NA_KW
    r_ext = jnp.concatenate([jnp.broadcast_to(rpb[..., :1], (nh, nd, edge)), rpb,
                             jnp.broadcast_to(rpb[..., -1:], (nh, nd, edge)), jnp.zeros((nh, nd, 1), rpb.dtype)], axis=-1)
    toe = jnp.tile(r_ext, (1, 1, w))[..., :w * (2 * w - 1)].reshape(nh, nd, w, 2 * w - 1)[..., w - 1:]
    col = np.arange(w)
    col_start = np.clip(col - NA_KW // 2, 0, w - NA_KW)
    col_in = (col[None, :] >= col_start[:, None]) & (col[None, :] < col_start[:, None] + NA_KW)
    toe = jnp.where(col_in[None, None], toe, NEG_INF).astype(F32)
    masked = jnp.full((nh, w, w), NEG_INF, F32)
    kb, _ = _na_tile_geometry(rows)
    n_tiles = rows // NA_ROWS_PER_TILE
    pats = []
    for tile in (0, 1, n_tiles - 1):
        q_rows = []
        for dq in range(NA_ROWS_PER_TILE):
            r = tile * NA_ROWS_PER_TILE + dq
            start = int(np.clip(r - kh // 2, 0, rows - kh))
            blocks = []
            for dk in range(NA_SPAN):
                kr = int(kb[tile]) + dk
                blocks.append(toe[:, kr - r + NA_KH - 1] if start <= kr < start + kh else masked)
            q_rows.append(jnp.concatenate(blocks, axis=-1))
        pats.append(jnp.concatenate(q_rows, axis=1))
    return jnp.stack(pats, axis=1)


def _na_attn_kernel(kb_ref, pat_ref, q_ref, k_ref, v_ref, bias_ref, o_ref):
    i = pl.program_id(2)
    scale = NA_HEAD_DIM ** -0.5
    pw = 2 * NA_HEAD_DIM
    lane = lax.broadcasted_iota(jnp.int32, (TOK_TILE, pw), 1)

    def head_q(pr, j):
        q = q_ref[0, :, pr * pw:(pr + 1) * pw].astype(F32)
        return jnp.where((lane < NA_HEAD_DIM) if j == 0 else (lane >= NA_HEAD_DIM), q, 0.0).astype(BF16)

    @pl.when(i == 0)
    def _():
        for pr in range(NA_PAIRS_PER_STEP):
            ps = slice(pr * pw, (pr + 1) * pw)
            outs = []
            for j in range(2):
                s = _dot_nt(head_q(pr, j), k_ref[0, :CTX_LEN, ps]) * scale
                outs.append(_softmax_pv([s], [v_ref[0, :CTX_LEN, ps]]))
            o_ref[0, :, ps] = _pair_select(outs[0], outs[1], NA_HEAD_DIM).astype(BF16)

    @pl.when(i > 0)
    def _():
        start = pl.multiple_of(CTX_LEN + kb_ref[i - 1] * GRID_W, GRID_W)
        nk = NA_SPAN * GRID_W
        pat = pat_ref[i - 1]
        for pr in range(NA_PAIRS_PER_STEP):
            ps = slice(pr * pw, (pr + 1) * pw)
            k_lat = k_ref[0, pl.ds(start, nk), ps]
            v_lat = v_ref[0, pl.ds(start, nk), ps]
            outs = []
            for j in range(2):
                qj = head_q(pr, j)
                s_lat = _dot_nt(qj, k_lat) * scale + bias_ref[2 * pr + j, pat]
                s_ctx = _dot_nt(qj, k_ref[0, :CTX_LEN, ps]) * scale
                outs.append(_softmax_pv([s_lat, s_ctx], [v_lat, v_ref[0, :CTX_LEN, ps]]))
            o_ref[0, :, ps] = _pair_select(outs[0], outs[1], NA_HEAD_DIM).astype(BF16)


def _na_attention(p_na, bias, rows):
    nb, t, _ = p_na.shape
    heads = 2 * NA_PAIRS_PER_STEP
    pw = heads * NA_HEAD_DIM
    n_groups = NA_HEADS // heads
    kb, pat = _na_tile_geometry(rows)
    nkeys = NA_SPAN * GRID_W
    grid_spec = pltpu.PrefetchScalarGridSpec(
        num_scalar_prefetch=2,
        grid=(n_groups, nb, t // TOK_TILE),
        in_specs=[
            pl.BlockSpec((1, TOK_TILE, pw), lambda p, b, i, *_: (b, i, p)),
            pl.BlockSpec((1, t, pw), lambda p, b, i, *_: (b, 0, n_groups + p)),
            pl.BlockSpec((1, t, pw), lambda p, b, i, *_: (b, 0, 2 * n_groups + p)),
            pl.BlockSpec((heads, 3, TOK_TILE, nkeys), lambda p, b, i, *_: (p, 0, 0, 0)),
        ],
        out_specs=pl.BlockSpec((1, TOK_TILE, pw), lambda p, b, i, *_: (b, i, p)),
    )
    return pl.pallas_call(
        _na_attn_kernel,
        grid_spec=grid_spec,
        out_shape=jax.ShapeDtypeStruct((nb, t, NA_WIDTH), BF16),
        compiler_params=_cparams("parallel", "parallel", "arbitrary"),
        name="na_attn",
    )(jnp.asarray(kb, jnp.int32), jnp.asarray(pat, jnp.int32), p_na, p_na, p_na, bias)


def _gla_direction(slab_ref, wg_ref, bg_ref, st_ref, o_ref, d):
    n = TOK_TILE
    n_chunks = n // GLA_CHUNK
    pw = 2 * GLA_DK
    row = lax.broadcasted_iota(jnp.int32, (n, n), 0)
    col = lax.broadcasted_iota(jnp.int32, (n, n), 1)
    shift = GLA_CHUNK.bit_length() - 1
    same_chunk = jnp.right_shift(row, shift) == jnp.right_shift(col, shift)
    causal = same_chunk & ((col <= row) if d == 0 else (col >= row))
    tri = jnp.where(causal, 1.0, 0.0).astype(F32)

    o_low = 2 * GLA_K_WIDTH + 2 * GLA_V_WIDTH
    gw = GLA_K_WIDTH
    g_raw = _dot_exact(slab_ref[0, :, o_low:o_low + LANES], wg_ref[:, d * gw:(d + 1) * gw]) + bg_ref[:, d * gw:(d + 1) * gw]
    g = jax.nn.log_sigmoid(g_raw) / GLA_GATE_NORM
    b = _dot_exact(tri, g)
    last = GLA_CHUNK - 1 if d == 0 else 0
    tot = jnp.concatenate([jnp.broadcast_to(b[c * GLA_CHUNK + last:c * GLA_CHUNK + last + 1], (GLA_CHUNK, b.shape[1]))
                           for c in range(n_chunks)], axis=0)
    e_b = jnp.exp(b)
    e_nb = jnp.exp(-b)
    e_end = jnp.exp(tot - b)
    decay = jnp.exp(tot)
    lane = lax.broadcasted_iota(jnp.int32, (n, pw), 1)
    chunk_order = range(n_chunks) if d == 0 else range(n_chunks - 1, -1, -1)

    for p in range(GLA_HEADS // 2):
        sl = slice(p * pw, (p + 1) * pw)
        q_in = slab_ref[0, :, sl] * e_b[:, sl] * (GLA_DK ** -0.5)
        k_p = slab_ref[0, :, GLA_K_WIDTH + p * pw:GLA_K_WIDTH + (p + 1) * pw]
        k_in = (k_p * e_nb[:, sl]).astype(BF16)
        k_end = (k_p * e_end[:, sl]).astype(BF16)
        for j in range(2):
            h = 2 * p + j
            q_h = jnp.where((lane < GLA_DK) if j == 0 else (lane >= GLA_DK), q_in, 0.0).astype(BF16)
            v_h = slab_ref[0, :, 2 * GLA_K_WIDTH + h * GLA_DV:2 * GLA_K_WIDTH + (h + 1) * GLA_DV].astype(BF16)
            a = jnp.where(causal, _dot_nt(q_h, k_in), 0.0)
            o_intra = _dot(a.astype(BF16), v_h)
            for c in chunk_order:
                rs = slice(c * GLA_CHUNK, (c + 1) * GLA_CHUNK)
                st = st_ref[d, h]
                o_ref[0, rs, h * GLA_DV:(h + 1) * GLA_DV] = o_intra[rs] + _dot_nt(q_h[rs], st.astype(BF16))
                st_ref[d, h] = st * decay[c * GLA_CHUNK:c * GLA_CHUNK + 1, sl] + _dot_tn(v_h[rs], k_end[rs])


def _gla_kernel(f_ref, b_ref, wg_ref, bg_ref, of_ref, ob_ref, st_ref):
    @pl.when(pl.program_id(1) == 0)
    def _():
        st_ref[...] = jnp.zeros_like(st_ref)

    _gla_direction(f_ref, wg_ref, bg_ref, st_ref, of_ref, 0)
    _gla_direction(b_ref, wg_ref, bg_ref, st_ref, ob_ref, 1)


def _gla_scan(p_gla, wg, bg):
    nb, t, _ = p_gla.shape
    n_tiles = t // TOK_TILE
    fwd = lambda b, s: (b, s, 0)
    bwd = lambda b, s: (b, jnp.where(s == 0, 0, n_tiles - s), 0)
    return pl.pallas_call(
        _gla_kernel,
        grid=(nb, n_tiles),
        in_specs=[
            pl.BlockSpec((1, TOK_TILE, GLA_COLS), fwd),
            pl.BlockSpec((1, TOK_TILE, GLA_COLS), bwd),
            pl.BlockSpec((LANES, 2 * GLA_K_WIDTH), lambda b, s: (0, 0)),
            pl.BlockSpec((1, 2 * GLA_K_WIDTH), lambda b, s: (0, 0)),
        ],
        out_specs=[
            pl.BlockSpec((1, TOK_TILE, GLA_V_WIDTH), fwd),
            pl.BlockSpec((1, TOK_TILE, GLA_V_WIDTH), bwd),
        ],
        out_shape=[jax.ShapeDtypeStruct((nb, t, GLA_V_WIDTH), F32)] * 2,
        scratch_shapes=[pltpu.VMEM((2, GLA_HEADS, GLA_DV, 2 * GLA_DK), F32)],
        compiler_params=_cparams("parallel", "arbitrary"),
        name="gla_scan",
    )(p_gla, p_gla, wg, bg)


def _merge_kernel(x_ref, mod_ref, oa_ref, ob_ref, gf_ref, gb_ref, og_ref, gate_ref, gn_ref, n2_ref,
                  wa_ref, wb_ref, wg_ref, wo_ref, x_out_ref, h_out_ref):
    og = og_ref[0]
    o_raw = gf_ref[0] + gb_ref[0]
    parts = []
    for h in range(GLA_HEADS):
        sl = slice(h * GLA_DV, (h + 1) * GLA_DV)
        parts.append(_rms(o_raw[:, sl], gn_ref[...]) * jax.nn.silu(og[:, sl]))
    o_g = jnp.concatenate(parts, axis=-1).astype(BF16)
    d = D_MODEL
    y = (jax.nn.sigmoid(gate_ref[0, :, 0:d]) * _dot(oa_ref[0], wa_ref[...])
         + jax.nn.sigmoid(gate_ref[0, :, d:2 * d]) * _dot(ob_ref[0], wb_ref[...])
         + jax.nn.sigmoid(gate_ref[0, :, 2 * d:3 * d]) * _dot(o_g, wg_ref[...]))
    x = x_ref[0] + mod_ref[0, 2:3, :] * _dot(y.astype(BF16), wo_ref[...])
    x_out_ref[0] = x
    h2 = _rms(x, n2_ref[...]) * (1.0 + mod_ref[0, 4:5, :]) + mod_ref[0, 3:4, :]
    h_out_ref[0] = h2.astype(BF16)


def _merge(xs, mod, o_a, o_b, gla_f, gla_b, p_gla, p_gate, gla_norm, norm2, wa, wb, wg, wo, keep_ctx):
    nb, t, d = xs.shape
    skip = 0 if keep_ctx else 1
    t_out = t - skip * TOK_TILE
    src = lambda b, i: (b, i + skip, 0)
    tok = lambda b, i: (b, i, 0)
    const = lambda b, i: (0, 0)
    og_block = (2 * GLA_K_WIDTH) // GLA_V_WIDTH + 1
    return pl.pallas_call(
        _merge_kernel,
        grid=(nb, t_out // TOK_TILE),
        in_specs=[
            pl.BlockSpec((1, TOK_TILE, d), src),
            pl.BlockSpec((1, N_MOD, d), _mod_row(nb, keep_ctx)),
            pl.BlockSpec((1, TOK_TILE, MLA_WIDTH), src),
            pl.BlockSpec((1, TOK_TILE, NA_WIDTH), src),
            pl.BlockSpec((1, TOK_TILE, GLA_V_WIDTH), src),
            pl.BlockSpec((1, TOK_TILE, GLA_V_WIDTH), src),
            pl.BlockSpec((1, TOK_TILE, GLA_V_WIDTH), lambda b, i: (b, i + skip, og_block)),
            pl.BlockSpec((1, TOK_TILE, GATE_COLS), src),
            pl.BlockSpec((1, GLA_DV), const),
            pl.BlockSpec((1, d), const),
            pl.BlockSpec((MLA_WIDTH, d), const),
            pl.BlockSpec((NA_WIDTH, d), const),
            pl.BlockSpec((GLA_V_WIDTH, d), const),
            pl.BlockSpec((d, d), const),
        ],
        out_specs=[pl.BlockSpec((1, TOK_TILE, d), tok), pl.BlockSpec((1, TOK_TILE, d), tok)],
        out_shape=[jax.ShapeDtypeStruct((nb, t_out, d), F32), jax.ShapeDtypeStruct((nb, t_out, d), BF16)],
        compiler_params=_cparams("parallel", "parallel"),
        name="merge",
    )(xs, mod, o_a, o_b, gla_f, gla_b, p_gla, p_gate, gla_norm.reshape(1, -1), norm2.reshape(1, -1),
      wa, wb, wg, wo)


def _top_ranked(x, k, want_rank=True):
    row = lax.broadcasted_iota(jnp.int32, x.shape, 0).astype(F32)
    out_row = lax.broadcasted_iota(jnp.int32, (k, x.shape[1]), 0)
    vals = jnp.zeros((k, x.shape[1]), F32)
    rank = jnp.full(x.shape, float(k), F32)
    for r in range(k):
        m = jnp.max(x, axis=0, keepdims=True)
        vals = jnp.where(out_row == r, m, vals)
        hit = row == jnp.min(jnp.where(x == m, row, float(x.shape[0])), axis=0, keepdims=True)
        if want_rank:
            rank = jnp.where(hit, float(r), rank)
        x = jnp.where(hit, -jnp.inf, x)
    return vals, rank


def _count(mask):
    return jnp.sum(jnp.where(mask, 1.0, 0.0), axis=0, keepdims=True)


def _pack_rows(x):
    return pltpu.bitcast(x.astype(BF16), jnp.uint32)


def _peer_select_chunk(s0, s1):
    k = PEER_TOPK
    sub = lax.broadcasted_iota(jnp.int32, (SUBLANES, LANES), 0)
    v0, rank0 = _top_ranked(s0, k)
    v1, rank1 = _top_ranked(s1, k)
    cand = [v0[0:1] + v1]
    for a in range(1, k):
        cand.append(jnp.where(sub < k // (a + 1), v0[a:a + 1] + v1[:SUBLANES], -jnp.inf))
    top, _ = _top_ranked(jnp.concatenate(cand, axis=0), k, want_rank=False)
    thr = top[k - 1:k]
    z = jnp.sum(jnp.exp(top - top[0:1]), axis=0, keepdims=True)
    above = [_count(ca > thr) for ca in cand]
    equal = [_count(ca == thr) for ca in cand]
    spare = float(k) - functools.reduce(jnp.add, above)
    n0 = jnp.zeros_like(s0)
    for a in range(k):
        take = jnp.clip(spare, 0.0, equal[a])
        spare = spare - equal[a]
        n0 = jnp.where(rank0 == float(a), above[a] + take, n0)
    return n0, jnp.exp(s0 - v0[0:1]) / z, _pack_rows(rank1), _pack_rows(jnp.exp(s1 - v1[0:1]))


def _sorting_network(n):
    pairs = []
    p = 1
    while p < n:
        k = p
        while k >= 1:
            for j in range(k % p, n - k, 2 * k):
                for i in range(min(k, n - j - k)):
                    if (i + j) // (2 * p) == (i + j + k) // (2 * p):
                        pairs.append((i + j, i + j + k))
            k //= 2
        p *= 2
    return pairs


def _sublane_sum(x):
    for shift in (4, 2, 1):
        x = x + pltpu.roll(x, shift, axis=0)
    return x


def _top_sorted(blocks, k):
    def exchange(xs, i, j):
        xs[i], xs[j] = jnp.maximum(xs[i], xs[j]), jnp.minimum(xs[i], xs[j])

    xs = list(blocks[:k])
    for i, j in _sorting_network(k):
        exchange(xs, i, j)
    for extra in blocks[k:]:
        for r in range(k):
            xs[r], extra = jnp.maximum(xs[r], extra), jnp.minimum(xs[r], extra)
    for shift in (4, 2, 1):
        ys = [pltpu.roll(x, shift, axis=0) for x in xs]
        xs = [jnp.maximum(xs[r], ys[k - 1 - r]) for r in range(k)]
        stride = k // 2
        while stride >= 1:
            for r in range(k):
                if r & stride == 0:
                    exchange(xs, r, r + stride)
            stride //= 2
    return xs


def _peer_select_chunk_distinct(s0, s1):
    k = PEER_TOPK
    nb = s0.shape[0] // SUBLANES
    sub = lax.broadcasted_iota(jnp.int32, (SUBLANES, LANES), 0)
    blocks0 = [s0[SUBLANES * i:SUBLANES * (i + 1)] for i in range(nb)]
    blocks1 = [s1[SUBLANES * i:SUBLANES * (i + 1)] for i in range(nb)]
    v0 = _top_sorted(blocks0, k)
    v1 = _top_sorted(blocks1, k)

    def ties(v, blocks):
        same = functools.reduce(jnp.add, [jnp.where(v[r] == v[r + 1], 1.0, 0.0) for r in range(k - 1)])
        reach = _sublane_sum(functools.reduce(jnp.add, [jnp.where(b >= v[k - 1], 1.0, 0.0) for b in blocks]))
        return same + jnp.abs(reach - float(k))

    v1_lo, v1_hi = v1[SUBLANES - 1], v1[2 * SUBLANES - 1]
    for b in range(SUBLANES - 2, -1, -1):
        v1_lo = jnp.where(sub == b, v1[b], v1_lo)
        v1_hi = jnp.where(sub == b, v1[SUBLANES + b], v1_hi)
    cand = [[v0[0] + v1_lo, v0[0] + v1_hi]]
    for a in range(1, k):
        cand.append([jnp.where(sub < k // (a + 1), v0[a] + v1_lo, -jnp.inf)])
    cand_blocks = [c for ca in cand for c in ca]
    top = _top_sorted(cand_blocks[1:] + cand_blocks[:1], k)
    thr = top[k - 1]
    z = functools.reduce(jnp.add, [jnp.exp(t - top[0]) for t in top])
    n_sel = [_sublane_sum(functools.reduce(jnp.add, [jnp.where(c >= thr, 1.0, 0.0) for c in ca])) for ca in cand]
    redo = ties(v0, blocks0) + ties(v1, blocks1) + ties(top, cand_blocks)

    n0, e0, rank1, e1 = [], [], [], []
    for x0, x1 in zip(blocks0, blocks1):
        n = jnp.zeros_like(x0)
        for a in range(k):
            n = jnp.where(x0 == v0[a], n_sel[a], n)
        n0.append(n)
        e0.append(jnp.exp(x0 - v0[0]) / z)
        rank1.append(functools.reduce(jnp.add, [jnp.where(v > x1, 1.0, 0.0) for v in v1]))
        e1.append(jnp.exp(x1 - v1[0]))
    cat = lambda parts: jnp.concatenate(parts, axis=0)
    return (cat(n0), cat(e0), _pack_rows(cat(rank1)), _pack_rows(cat(e1))), redo[0:1]


def _peer_select_kernel(h_ref, wqt_ref, kp_ref, ht_ref, n0_ref, e0_ref, r1_ref, e1_ref, s_ref):
    h = h_ref[...]
    ht_ref[...] = h.T
    q_t = _dot_nt(wqt_ref[...], h)
    nk = PEER_N_KEYS
    for hd in range(PEER_HEADS):
        s_ref[hd] = _dot(kp_ref[...], q_t[hd * PEER_QUERY_DIM:(hd + 1) * PEER_QUERY_DIM].astype(BF16))

    def run(exact):
        redo = jnp.zeros((1, LANES), F32)
        for hd in range(PEER_HEADS):
            for c in range(h.shape[0] // LANES):
                cs = slice(c * LANES, (c + 1) * LANES)
                s0, s1 = s_ref[hd, :nk, cs], s_ref[hd, nk:, cs]
                if exact:
                    outs = _peer_select_chunk(s0, s1)
                else:
                    outs, flag = _peer_select_chunk_distinct(s0, s1)
                    redo = jnp.maximum(redo, flag)
                for ref, val in zip((n0_ref, e0_ref, r1_ref, e1_ref), outs):
                    ref[hd, :, cs] = val
        return redo

    redo = run(exact=False)

    @pl.when(jnp.max(redo) > 0.0)
    def _():
        run(exact=True)


def _peer_select(h2, wq_t, key_pair):
    n, d = h2.shape
    tm = TOK_TILE
    fac = lambda dt: jax.ShapeDtypeStruct((PEER_HEADS, PEER_N_KEYS // (1 if dt == F32 else 2), n), dt)
    fac_spec = pl.BlockSpec((PEER_HEADS, PEER_N_KEYS, tm), lambda i: (0, 0, i))
    packed_spec = pl.BlockSpec((PEER_HEADS, PEER_N_KEYS // 2, tm), lambda i: (0, 0, i))
    return pl.pallas_call(
        _peer_select_kernel,
        grid=(n // tm,),
        in_specs=[
            pl.BlockSpec((tm, d), lambda i: (i, 0)),
            pl.BlockSpec(wq_t.shape, lambda i: (0, 0)),
            pl.BlockSpec(key_pair.shape, lambda i: (0, 0)),
        ],
        out_specs=[pl.BlockSpec((d, tm), lambda i: (0, i)), fac_spec, fac_spec, packed_spec, packed_spec],
        out_shape=[jax.ShapeDtypeStruct((d, n), BF16), fac(F32), fac(F32), fac(jnp.uint32), fac(jnp.uint32)],
        scratch_shapes=[pltpu.VMEM((PEER_HEADS, 2 * PEER_N_KEYS, tm), F32)],
        compiler_params=_cparams("parallel"),
        name="peer_select",
    )(h2, wq_t, key_pair)


def _peer_dense_kernel(ht_ref, u_ref, vt_ref, n0_ref, e0_ref, r1_ref, e1_ref, o_ref, g_ref, act_ref):
    nk = PEER_N_KEYS
    piece = PEER_PIECE
    n_pieces = PEER_EXP_TILE // piece
    tm = ht_ref.shape[1]

    @pl.when(pl.program_id(1) == 0)
    def _():
        o_ref[...] = jnp.zeros_like(o_ref)

    def gate_piece(p):
        rs_n = nk // 2
        zero = jnp.zeros((rs_n, LANES), BF16)
        for sub in range(piece // nk):
            ii = p * (piece // nk) + sub
            for cs in range(0, tm, LANES):
                for rs in range(0, nk, rs_n):
                    w = zero
                    for hd in range(PEER_HEADS):
                        n_sel = jnp.broadcast_to(n0_ref[hd, ii:ii + 1, cs:cs + LANES], (rs_n, LANES)).astype(BF16)
                        e0 = jnp.broadcast_to(e0_ref[hd, ii:ii + 1, cs:cs + LANES], (rs_n, LANES)).astype(BF16)
                        r1 = pltpu.bitcast(r1_ref[hd, rs // 2:(rs + rs_n) // 2, cs:cs + LANES], BF16)
                        e1 = pltpu.bitcast(e1_ref[hd, rs // 2:(rs + rs_n) // 2, cs:cs + LANES], BF16)
                        w = w + jnp.where(r1 < n_sel, e1 * e0, zero)
                    rows = slice(sub * nk + rs, sub * nk + rs + rs_n)
                    act = act_ref[p, rows, cs:cs + LANES]
                    g_ref[p, rows, cs:cs + LANES] = w * jax.nn.gelu(act).astype(BF16)

    for p in range(n_pieces):
        act_ref[p] = _dot(u_ref[0, p * piece:(p + 1) * piece, :], ht_ref[...])
    for p in range(n_pieces):
        gate_piece(p)
        o_ref[...] += _dot(vt_ref[0, :, p * piece:(p + 1) * piece], g_ref[p])


def _peer_dense(h_t, u, v_t, layer, n0, e0, r1, e1):
    d, n = h_t.shape
    tm, te = PEER_TOK_TILE, PEER_EXP_TILE
    sub = te // PEER_N_KEYS
    return pl.pallas_call(
        _peer_dense_kernel,
        grid=(n // tm, PEER_N_EXPERTS // te),
        in_specs=[
            pl.BlockSpec((d, tm), lambda i, e: (0, i)),
            pl.BlockSpec((1, te, d), lambda i, e: (layer, e, 0)),
            pl.BlockSpec((1, d, te), lambda i, e: (layer, 0, e)),
            pl.BlockSpec((PEER_HEADS, sub, tm), lambda i, e: (0, e, i)),
            pl.BlockSpec((PEER_HEADS, sub, tm), lambda i, e: (0, e, i)),
            pl.BlockSpec((PEER_HEADS, PEER_N_KEYS // 2, tm), lambda i, e: (0, 0, i)),
            pl.BlockSpec((PEER_HEADS, PEER_N_KEYS // 2, tm), lambda i, e: (0, 0, i)),
        ],
        out_specs=pl.BlockSpec((d, tm), lambda i, e: (0, i)),
        out_shape=jax.ShapeDtypeStruct((d, n), F32),
        scratch_shapes=[pltpu.VMEM((te // PEER_PIECE, PEER_PIECE, tm), BF16),
                        pltpu.VMEM((te // PEER_PIECE, PEER_PIECE, tm), F32)],
        compiler_params=_cparams("parallel", "arbitrary"),
        name="peer_dense",
    )(h_t, u, v_t, n0, e0, r1, e1)


def _peer_residual_kernel(x_ref, mod_ref, ot_ref, fn_ref, o_ref, *, final):
    x = x_ref[0] + mod_ref[0, 5:6, :] * ot_ref[...].T
    o_ref[0] = _rms(x, fn_ref[...]) if final else x


def _peer_residual(xs, mod, out_t, final_norm, final):
    nb, t, d = xs.shape
    n_tiles = t // TOK_TILE
    tok = lambda b, i: (b, i, 0)
    return pl.pallas_call(
        functools.partial(_peer_residual_kernel, final=final),
        grid=(nb, n_tiles),
        in_specs=[
            pl.BlockSpec((1, TOK_TILE, d), tok),
            pl.BlockSpec((1, N_MOD, d), _mod_row(nb, has_ctx=not final)),
            pl.BlockSpec((d, TOK_TILE), lambda b, i: (0, b * n_tiles + i)),
            pl.BlockSpec((1, d), lambda b, i: (0, 0)),
        ],
        out_specs=pl.BlockSpec((1, TOK_TILE, d), tok),
        out_shape=jax.ShapeDtypeStruct((nb, t, d), F32),
        compiler_params=_cparams("parallel", "parallel"),
        name="peer_residual",
    )(xs, mod, out_t, final_norm.reshape(1, d))


def _rot_cols(w):
    q = MLA_ROPE // 4
    return jnp.concatenate([-w[..., q:2 * q], w[..., :q], -w[..., 3 * q:], w[..., 2 * q:3 * q]], axis=-1)


def _rope_slab(w):
    return jnp.pad(w, ((0, 0), (MLA_NOPE, LANES - MLA_NOPE - MLA_ROPE)))


def _prep_w_in(w_in):
    o = MLA_Q_LORA + MLA_KV_LORA
    w_kr = w_in[:, o:o + MLA_ROPE]
    o2 = o + MLA_ROPE
    n_mid = NA_COLS + 2 * GLA_K_WIDTH + 2 * GLA_V_WIDTH
    low = w_in[:, o2 + n_mid:o2 + n_mid + 2 * GLA_GATE_RANK]
    parts = [w_in[:, :o], _rope_slab(w_kr), _rope_slab(_rot_cols(w_kr)), w_in[:, o2:o2 + n_mid],
             jnp.pad(low, ((0, 0), (0, LANES - 2 * GLA_GATE_RANK))), w_in[:, o2 + n_mid + 2 * GLA_GATE_RANK:]]
    return jnp.concatenate(parts, axis=1).astype(BF16)


def _prep_mla(w_uq, w_ukv):
    r = w_uq.shape[0]
    wq = w_uq.reshape(r, MLA_HEADS, MLA_NOPE + MLA_ROPE)
    pad = MLA_HEAD_PAD - MLA_NOPE - MLA_ROPE
    wq_p = jnp.pad(wq, ((0, 0), (0, 0), (0, pad)))
    wq_rot = jnp.pad(_rot_cols(wq[..., MLA_NOPE:]), ((0, 0), (0, 0), (MLA_NOPE, pad)))
    c = w_ukv.shape[0]
    wkv = w_ukv.reshape(c, MLA_HEADS, MLA_NOPE + MLA_V)
    wk = jnp.pad(wkv[..., :MLA_NOPE], ((0, 0), (0, 0), (0, MLA_HEAD_PAD - MLA_NOPE)))
    wv = wkv[..., MLA_NOPE:]
    hw = MLA_HEADS * MLA_HEAD_PAD
    lane = np.arange(LANES)
    col = np.arange(hw)
    is_rope = (lane >= MLA_NOPE) & (lane < MLA_NOPE + MLA_ROPE)
    e = (is_rope[:, None] & (col[None, :] % MLA_HEAD_PAD == lane[:, None])).astype(np.float32)
    return (wq_p.reshape(r, hw).astype(BF16), wq_rot.reshape(r, hw).astype(BF16),
            wk.reshape(c, hw).astype(BF16), jnp.asarray(e, BF16), wv.reshape(c, MLA_WIDTH).astype(BF16))


def _rope_tables(seq):
    pos = jnp.arange(seq)
    rows = (pos // GRID_W).astype(F32)
    cols = (pos % GRID_W).astype(F32)
    half = MLA_ROPE // 2
    inv = ROPE_THETA ** (-jnp.arange(0, half, 2, dtype=F32) / half)
    ar = rows[:, None] * inv
    ac = cols[:, None] * inv
    ang = jnp.concatenate([ar, ar, ac, ac], axis=-1)
    ang = jnp.concatenate([jnp.zeros((CTX_LEN, MLA_ROPE), F32), ang], axis=0)
    t = ang.shape[0]
    ones = jnp.ones((t, MLA_NOPE), F32)
    zeros = jnp.zeros((t, LANES - MLA_NOPE - MLA_ROPE), F32)
    cos_t = jnp.concatenate([ones, jnp.cos(ang), zeros], axis=1)
    sin_t = jnp.concatenate([0.0 * ones, jnp.sin(ang), zeros], axis=1)
    return cos_t, sin_t


def _prep_gla_gate(w_f, b_f, w_b, b_b):
    wg = jnp.zeros((LANES, 2 * GLA_K_WIDTH), F32)
    wg = wg.at[:GLA_GATE_RANK, :GLA_K_WIDTH].set(w_f)
    wg = wg.at[GLA_GATE_RANK:2 * GLA_GATE_RANK, GLA_K_WIDTH:].set(w_b)
    return wg, jnp.concatenate([b_f, b_b]).reshape(1, -1)


def _prep_peer_keys(sub_keys):
    half = PEER_QUERY_DIM // 2
    kp = jnp.zeros((2 * PEER_N_KEYS, PEER_QUERY_DIM), F32)
    kp = kp.at[:PEER_N_KEYS, :half].set(sub_keys[0])
    kp = kp.at[PEER_N_KEYS:, half:].set(sub_keys[1])
    return kp.astype(BF16)


def kernel(x, c, ctx, c_ctx, w_ada, b_ada, norm1, w_in, mla_q_norm, mla_w_uq, mla_kv_norm, mla_w_ukv, na_rpb, gla_w_gk_fwd, gla_b_gk_fwd, gla_w_gk_bwd, gla_b_gk_bwd, gla_norm, w_o_mla, w_o_na, w_o_gla, w_out, norm2, peer_w_q, peer_sub_keys, peer_u, peer_v, final_norm):
    nb, seq, d = x.shape
    assert ctx.shape[1] == CTX_LEN == TOK_TILE and seq % (NA_ROWS_PER_TILE * GRID_W) == 0
    assert NA_ROWS_PER_TILE * GRID_W == TOK_TILE and nb < 8
    rows = seq // GRID_W
    depth = w_ada.shape[0]

    xs = jnp.concatenate([ctx, x], axis=1)
    t = xs.shape[1]
    cc = jnp.zeros((8, d), F32).at[:nb].set(c).at[nb].set(c_ctx)
    mods = _modulation(cc, w_ada, b_ada).reshape(depth, 8, N_MOD, d)
    cos_t, sin_t = _rope_tables(seq)
    u_all = peer_u.astype(BF16)
    vt_all = jnp.swapaxes(peer_v, 1, 2).astype(BF16)

    for l in range(depth):
        mod = mods[l]
        p_mla, p_na, p_gla, p_gate = _in_projection(xs, mod, norm1[l], _prep_w_in(w_in[l]))
        q, k, v = _mla_prep(p_mla, cos_t, sin_t, mla_q_norm[l], mla_kv_norm[l], *_prep_mla(mla_w_uq[l], mla_w_ukv[l]))
        o_a = _mla_attention(q, k, v)
        o_b = _na_attention(p_na, _na_bias_table(na_rpb[l], rows), rows)
        gla_f, gla_b = _gla_scan(p_gla, *_prep_gla_gate(gla_w_gk_fwd[l], gla_b_gk_fwd[l], gla_w_gk_bwd[l], gla_b_gk_bwd[l]))
        last = l == depth - 1
        xs, h2 = _merge(xs, mod, o_a, o_b, gla_f, gla_b, p_gla, p_gate, gla_norm[l], norm2[l],
                        w_o_mla[l].astype(BF16), w_o_na[l].astype(BF16), w_o_gla[l].astype(BF16), w_out[l].astype(BF16),
                        keep_ctx=not last)
        h_t, n0, e0, r1, e1 = _peer_select(h2.reshape(-1, d), peer_w_q[l].T.astype(BF16),
                                           _prep_peer_keys(peer_sub_keys[l]))
        out_t = _peer_dense(h_t, u_all, vt_all, l, n0, e0, r1, e1)
        xs = _peer_residual(xs, mod, out_t, final_norm, final=last)
    return xs
```

```python
import functools

import numpy as np
import jax
import jax.numpy as jnp
from jax import lax
from jax.experimental import pallas as pl
from jax.experimental.pallas import tpu as pltpu

F32 = jnp.float32
BF16 = jnp.bfloat16
HIGHEST = lax.Precision.HIGHEST

D_MODEL = 1024
DEPTH = 2
CTX_LEN = 256
GRID_W = 64
EPS = 1e-6
N_MOD = 6
NEG_INF = -1e30

MLA_HEADS = 8
MLA_Q_LORA = 384
MLA_KV_LORA = 256
MLA_NOPE = 64
MLA_ROPE = 32
MLA_V = 64
ROPE_THETA = 10000.0
MLA_HEAD_PAD = 128
MLA_PAIRS_PER_STEP = 4

NA_HEADS = 8
NA_HEAD_DIM = 64
NA_KH = 8
NA_KW = 16
NA_ROWS_PER_TILE = 4
NA_SPAN = NA_ROWS_PER_TILE + NA_KH
NA_PAIRS_PER_STEP = 4

GLA_HEADS = 4
GLA_DK = 64
GLA_DV = 128
GLA_GATE_RANK = 16
GLA_GATE_NORM = 16.0
GLA_CHUNK = 64
GLA_BATCH_PER_STEP = 1

PEER_HEADS = 8
PEER_N_KEYS = 128
PEER_N_EXPERTS = PEER_N_KEYS * PEER_N_KEYS
PEER_QUERY_DIM = 128
PEER_TOPK = 16
PEER_TOK_TILE = 512
PEER_EXP_TILE = 1024
PEER_PIECE = 256

MLA_WIDTH = MLA_HEADS * MLA_V
NA_WIDTH = NA_HEADS * NA_HEAD_DIM
GLA_K_WIDTH = GLA_HEADS * GLA_DK
GLA_V_WIDTH = GLA_HEADS * GLA_DV

LANES = 128
SUBLANES = 8
TOK_TILE = 256

MLA_COLS = MLA_Q_LORA + MLA_KV_LORA + 2 * LANES
NA_COLS = 3 * NA_WIDTH
GLA_COLS = 2 * GLA_K_WIDTH + 2 * GLA_V_WIDTH + LANES
GATE_COLS = 3 * D_MODEL
IN_COLS_PAD = MLA_COLS + NA_COLS + GLA_COLS + GATE_COLS

VMEM_LIMIT = 56 * 1024 * 1024


def _cparams(*sem, flags=None):
    return pltpu.CompilerParams(dimension_semantics=sem, vmem_limit_bytes=VMEM_LIMIT, flags=flags)


def _rms(x, w):
    return x * lax.rsqrt(jnp.mean(x * x, axis=-1, keepdims=True) + EPS) * w


def _dot(a, b):
    return jnp.dot(a, b, preferred_element_type=F32)


def _dot_nt(a, b):
    return lax.dot_general(a, b, (((1,), (1,)), ((), ())), preferred_element_type=F32)


def _dot_tn(a, b):
    return lax.dot_general(a, b, (((0,), (0,)), ((), ())), preferred_element_type=F32)


def _dot_exact(a, b):
    return jnp.dot(a, b, preferred_element_type=F32, precision=HIGHEST)


def _mod_kernel(c_ref, w_ref, b_ref, o_ref):
    c = c_ref[...]
    o_ref[0] = _dot_exact(c * jax.nn.sigmoid(c), w_ref[0]) + b_ref[0]


def _modulation(cc, w_ada, b_ada):
    depth, d, n = w_ada.shape
    tn = 1536
    return pl.pallas_call(
        _mod_kernel,
        grid=(depth, n // tn),
        in_specs=[
            pl.BlockSpec((8, d), lambda l, j: (0, 0)),
            pl.BlockSpec((1, d, tn), lambda l, j: (l, 0, j)),
            pl.BlockSpec((1, 1, tn), lambda l, j: (l, 0, j)),
        ],
        out_specs=pl.BlockSpec((1, 8, tn), lambda l, j: (l, 0, j)),
        out_shape=jax.ShapeDtypeStruct((depth, 8, n), F32),
        compiler_params=_cparams("parallel", "parallel"),
        name="adaln_mod",
    )(cc, w_ada, b_ada.reshape(depth, 1, n))


def _mod_row(n_batch, has_ctx=True):
    if not has_ctx:
        return lambda b, i: (b, 0, 0)
    return lambda b, i: (jnp.where(i == 0, n_batch, b), 0, 0)


def _inproj_kernel(x_ref, mod_ref, nw_ref, w_ref, mla_ref, na_ref, gla_ref, gate_ref):
    h = _rms(x_ref[0], nw_ref[...]) * (1.0 + mod_ref[0, 1:2, :]) + mod_ref[0, 0:1, :]
    h = h.astype(BF16)
    o = 0
    mla_ref[0] = _dot(h, w_ref[:, o:o + MLA_COLS])
    o += MLA_COLS
    na_ref[0] = _dot(h, w_ref[:, o:o + NA_COLS]).astype(BF16)
    o += NA_COLS
    gla_ref[0] = _dot(h, w_ref[:, o:o + GLA_COLS])
    o += GLA_COLS
    gate_ref[0] = _dot(h, w_ref[:, o:o + GATE_COLS])


def _in_projection(xs, mod, norm_w, w_in_p):
    nb, t, d = xs.shape
    tok = lambda b, i: (b, i, 0)
    return pl.pallas_call(
        _inproj_kernel,
        grid=(nb, t // TOK_TILE),
        in_specs=[
            pl.BlockSpec((1, TOK_TILE, d), tok),
            pl.BlockSpec((1, N_MOD, d), _mod_row(nb)),
            pl.BlockSpec((1, d), lambda b, i: (0, 0)),
            pl.BlockSpec((d, IN_COLS_PAD), lambda b, i: (0, 0), pipeline_mode=pl.Buffered(1)),
        ],
        out_specs=[
            pl.BlockSpec((1, TOK_TILE, MLA_COLS), tok),
            pl.BlockSpec((1, TOK_TILE, NA_COLS), tok),
            pl.BlockSpec((1, TOK_TILE, GLA_COLS), tok),
            pl.BlockSpec((1, TOK_TILE, GATE_COLS), tok),
        ],
        out_shape=[
            jax.ShapeDtypeStruct((nb, t, MLA_COLS), F32),
            jax.ShapeDtypeStruct((nb, t, NA_COLS), BF16),
            jax.ShapeDtypeStruct((nb, t, GLA_COLS), F32),
            jax.ShapeDtypeStruct((nb, t, GATE_COLS), F32),
        ],
        compiler_params=_cparams("parallel", "parallel"),
        name="in_proj",
    )(xs, mod, norm_w.reshape(1, d), w_in_p)


def _mla_prep_kernel(p_ref, cos_ref, sin_ref, qn_ref, kvn_ref, wq_ref, wqr_ref, wk_ref, e_ref, wv_ref,
                     q_ref, k_ref, v_ref):
    scale = (MLA_NOPE + MLA_ROPE) ** -0.5
    cos = cos_ref[...]
    sin = sin_ref[...]
    o = MLA_Q_LORA + MLA_KV_LORA
    nq = _rms(p_ref[0, :, :MLA_Q_LORA], qn_ref[...]).astype(BF16)
    nkv = _rms(p_ref[0, :, MLA_Q_LORA:o], kvn_ref[...]).astype(BF16)
    q = _dot(nq, wq_ref[...])
    q_rot = _dot(nq, wqr_ref[...])
    for h in range(MLA_HEADS):
        sl = slice(h * MLA_HEAD_PAD, (h + 1) * MLA_HEAD_PAD)
        q_ref[0, :, sl] = ((q[:, sl] * cos + q_rot[:, sl] * sin) * scale).astype(BF16)
    k_rope = (p_ref[0, :, o:o + LANES] * cos + p_ref[0, :, o + LANES:o + 2 * LANES] * sin).astype(BF16)
    k_ref[0] = (_dot(nkv, wk_ref[...]) + _dot(k_rope, e_ref[...])).astype(BF16)
    v_ref[0] = _dot(nkv, wv_ref[...]).astype(BF16)


def _mla_prep(p_mla, cos_t, sin_t, q_norm, kv_norm, wq, wq_rot, wk, e_rope, wv):
    nb, t, _ = p_mla.shape
    hw = MLA_HEADS * MLA_HEAD_PAD
    tok = lambda b, i: (b, i, 0)
    const = lambda b, i: (0, 0)
    return pl.pallas_call(
        _mla_prep_kernel,
        grid=(nb, t // TOK_TILE),
        in_specs=[
            pl.BlockSpec((1, TOK_TILE, MLA_COLS), tok),
            pl.BlockSpec((TOK_TILE, LANES), lambda b, i: (i, 0)),
            pl.BlockSpec((TOK_TILE, LANES), lambda b, i: (i, 0)),
            pl.BlockSpec((1, MLA_Q_LORA), const),
            pl.BlockSpec((1, MLA_KV_LORA), const),
            pl.BlockSpec((MLA_Q_LORA, hw), const),
            pl.BlockSpec((MLA_Q_LORA, hw), const),
            pl.BlockSpec((MLA_KV_LORA, hw), const),
            pl.BlockSpec((LANES, hw), const),
            pl.BlockSpec((MLA_KV_LORA, MLA_WIDTH), const),
        ],
        out_specs=[
            pl.BlockSpec((1, TOK_TILE, hw), tok),
            pl.BlockSpec((1, TOK_TILE, hw), tok),
            pl.BlockSpec((1, TOK_TILE, MLA_WIDTH), tok),
        ],
        out_shape=[
            jax.ShapeDtypeStruct((nb, t, hw), BF16),
            jax.ShapeDtypeStruct((nb, t, hw), BF16),
            jax.ShapeDtypeStruct((nb, t, MLA_WIDTH), BF16),
        ],
        compiler_params=_cparams("parallel", "parallel"),
        name="mla_prep",
    )(p_mla, cos_t, sin_t, q_norm.reshape(1, -1), kv_norm.reshape(1, -1), wq, wq_rot, wk, e_rope, wv)


def _softmax_pv(s_list, v_list):
    m = s_list[0].max(axis=-1, keepdims=True)
    for s in s_list[1:]:
        m = jnp.maximum(m, s.max(axis=-1, keepdims=True))
    acc = None
    den = None
    for s, v in zip(s_list, v_list):
        p = jnp.exp(s - m)
        l = p.sum(axis=-1, keepdims=True)
        o = _dot(p.astype(BF16), v)
        acc = o if acc is None else acc + o
        den = l if den is None else den + l
    return acc / den


def _pair_select(o0, o1, half):
    lane = lax.broadcasted_iota(jnp.int32, o0.shape, 1)
    return jnp.where(lane < half, o0, o1)


def _mla_attn_kernel(q_ref, k_ref, v_ref, o_ref):
    def attend(nk):
        for pr in range(MLA_PAIRS_PER_STEP):
            vs = slice(pr * 2 * MLA_V, (pr + 1) * 2 * MLA_V)
            outs = []
            for j in range(2):
                h = 2 * pr + j
                sl = slice(h * MLA_HEAD_PAD, (h + 1) * MLA_HEAD_PAD)
                s = _dot_nt(q_ref[0, :, sl], k_ref[0, :nk, sl])
                outs.append(_softmax_pv([s], [v_ref[0, :nk, vs]]))
            o_ref[0, :, vs] = _pair_select(outs[0], outs[1], MLA_V).astype(BF16)

    is_ctx = pl.program_id(2) == 0

    @pl.when(is_ctx)
    def _():
        attend(CTX_LEN)

    @pl.when(jnp.logical_not(is_ctx))
    def _():
        attend(k_ref.shape[1])


def _mla_attention(q, k, v):
    nb, t, _ = q.shape
    heads = 2 * MLA_PAIRS_PER_STEP
    pw = heads * MLA_HEAD_PAD
    return pl.pallas_call(
        _mla_attn_kernel,
        grid=(nb, MLA_HEADS // heads, t // TOK_TILE),
        in_specs=[
            pl.BlockSpec((1, TOK_TILE, pw), lambda b, p, i: (b, i, p)),
            pl.BlockSpec((1, t, pw), lambda b, p, i: (b, 0, p)),
            pl.BlockSpec((1, t, heads * MLA_V), lambda b, p, i: (b, 0, p)),
        ],
        out_specs=pl.BlockSpec((1, TOK_TILE, heads * MLA_V), lambda b, p, i: (b, i, p)),
        out_shape=jax.ShapeDtypeStruct((nb, t, MLA_WIDTH), BF16),
        compiler_params=_cparams("parallel", "parallel", "arbitrary"),
        name="mla_attn",
    )(q, k, v)


def _na_tile_geometry(rows):
    n_tiles = rows // NA_ROWS_PER_TILE
    r0 = np.arange(n_tiles) * NA_ROWS_PER_TILE
    kb = np.clip(r0 - NA_KH // 2, 0, rows - NA_SPAN)
    pat = np.where(r0 == 0, 0, np.where(r0 == rows - NA_ROWS_PER_TILE, 2, 1))
    return kb, pat


def _na_bias_table(rpb, rows):
    nh, nd, _ = rpb.shape
    w = GRID_W
    kh = min(NA_KH, rows)
    edge = w - NA_KW
    r_ext = jnp.concatenate([jnp.broadcast_to(rpb[..., :1], (nh, nd, edge)), rpb,
                             jnp.broadcast_to(rpb[..., -1:], (nh, nd, edge)), jnp.zeros((nh, nd, 1), rpb.dtype)], axis=-1)
    toe = jnp.tile(r_ext, (1, 1, w))[..., :w * (2 * w - 1)].reshape(nh, nd, w, 2 * w - 1)[..., w - 1:]
    col = np.arange(w)
    col_start = np.clip(col - NA_KW // 2, 0, w - NA_KW)
    col_in = (col[None, :] >= col_start[:, None]) & (col[None, :] < col_start[:, None] + NA_KW)
    toe = jnp.where(col_in[None, None], toe, NEG_INF).astype(F32)
    masked = jnp.full((nh, w, w), NEG_INF, F32)
    kb, _ = _na_tile_geometry(rows)
    n_tiles = rows // NA_ROWS_PER_TILE
    pats = []
    for tile in (0, 1, n_tiles - 1):
        q_rows = []
        for dq in range(NA_ROWS_PER_TILE):
            r = tile * NA_ROWS_PER_TILE + dq
            start = int(np.clip(r - kh // 2, 0, rows - kh))
            blocks = []
            for dk in range(NA_SPAN):
                kr = int(kb[tile]) + dk
                blocks.append(toe[:, kr - r + NA_KH - 1] if start <= kr < start + kh else masked)
            q_rows.append(jnp.concatenate(blocks, axis=-1))
        pats.append(jnp.concatenate(q_rows, axis=1))
    return jnp.stack(pats, axis=1)


def _na_attn_kernel(kb_ref, pat_ref, q_ref, k_ref, v_ref, bias_ref, o_ref):
    i = pl.program_id(2)
    scale = NA_HEAD_DIM ** -0.5
    pw = 2 * NA_HEAD_DIM
    lane = lax.broadcasted_iota(jnp.int32, (TOK_TILE, pw), 1)

    def head_q(pr, j):
        q = q_ref[0, :, pr * pw:(pr + 1) * pw].astype(F32)
        return jnp.where((lane < NA_HEAD_DIM) if j == 0 else (lane >= NA_HEAD_DIM), q, 0.0).astype(BF16)

    @pl.when(i == 0)
    def _():
        for pr in range(NA_PAIRS_PER_STEP):
            ps = slice(pr * pw, (pr + 1) * pw)
            outs = []
            for j in range(2):
                s = _dot_nt(head_q(pr, j), k_ref[0, :CTX_LEN, ps]) * scale
                outs.append(_softmax_pv([s], [v_ref[0, :CTX_LEN, ps]]))
            o_ref[0, :, ps] = _pair_select(outs[0], outs[1], NA_HEAD_DIM).astype(BF16)

    @pl.when(i > 0)
    def _():
        start = pl.multiple_of(CTX_LEN + kb_ref[i - 1] * GRID_W, GRID_W)
        nk = NA_SPAN * GRID_W
        pat = pat_ref[i - 1]
        for pr in range(NA_PAIRS_PER_STEP):
            ps = slice(pr * pw, (pr + 1) * pw)
            k_lat = k_ref[0, pl.ds(start, nk), ps]
            v_lat = v_ref[0, pl.ds(start, nk), ps]
            outs = []
            for j in range(2):
                qj = head_q(pr, j)
                s_lat = _dot_nt(qj, k_lat) * scale + bias_ref[2 * pr + j, pat]
                s_ctx = _dot_nt(qj, k_ref[0, :CTX_LEN, ps]) * scale
                outs.append(_softmax_pv([s_lat, s_ctx], [v_lat, v_ref[0, :CTX_LEN, ps]]))
            o_ref[0, :, ps] = _pair_select(outs[0], outs[1], NA_HEAD_DIM).astype(BF16)


def _na_attention(p_na, bias, rows):
    nb, t, _ = p_na.shape
    heads = 2 * NA_PAIRS_PER_STEP
    pw = heads * NA_HEAD_DIM
    n_groups = NA_HEADS // heads
    kb, pat = _na_tile_geometry(rows)
    nkeys = NA_SPAN * GRID_W
    grid_spec = pltpu.PrefetchScalarGridSpec(
        num_scalar_prefetch=2,
        grid=(n_groups, nb, t // TOK_TILE),
        in_specs=[
            pl.BlockSpec((1, TOK_TILE, pw), lambda p, b, i, *_: (b, i, p)),
            pl.BlockSpec((1, t, pw), lambda p, b, i, *_: (b, 0, n_groups + p)),
            pl.BlockSpec((1, t, pw), lambda p, b, i, *_: (b, 0, 2 * n_groups + p)),
            pl.BlockSpec((heads, 3, TOK_TILE, nkeys), lambda p, b, i, *_: (p, 0, 0, 0),
                         pipeline_mode=pl.Buffered(1 if n_groups == 1 else 2)),
        ],
        out_specs=pl.BlockSpec((1, TOK_TILE, pw), lambda p, b, i, *_: (b, i, p)),
    )
    return pl.pallas_call(
        _na_attn_kernel,
        grid_spec=grid_spec,
        out_shape=jax.ShapeDtypeStruct((nb, t, NA_WIDTH), BF16),
        compiler_params=_cparams("parallel", "parallel", "arbitrary"),
        name="na_attn",
    )(jnp.asarray(kb, jnp.int32), jnp.asarray(pat, jnp.int32), p_na, p_na, p_na, bias)


def _gla_direction(slab_ref, wg_ref, bg_ref, st_ref, o_ref, d, bi):
    n = TOK_TILE
    n_chunks = n // GLA_CHUNK
    pw = 2 * GLA_DK
    row = lax.broadcasted_iota(jnp.int32, (n, n), 0)
    col = lax.broadcasted_iota(jnp.int32, (n, n), 1)
    shift = GLA_CHUNK.bit_length() - 1
    same_chunk = jnp.right_shift(row, shift) == jnp.right_shift(col, shift)
    causal = same_chunk & ((col <= row) if d == 0 else (col >= row))
    tri = jnp.where(causal, 1.0, 0.0).astype(F32)

    o_low = 2 * GLA_K_WIDTH + 2 * GLA_V_WIDTH
    gw = GLA_K_WIDTH
    g_raw = _dot_exact(slab_ref[bi, :,o_low:o_low + LANES], wg_ref[:, d * gw:(d + 1) * gw]) + bg_ref[:, d * gw:(d + 1) * gw]
    g = jax.nn.log_sigmoid(g_raw) / GLA_GATE_NORM
    b = _dot_exact(tri, g)
    last = GLA_CHUNK - 1 if d == 0 else 0
    tot = jnp.concatenate([jnp.broadcast_to(b[c * GLA_CHUNK + last:c * GLA_CHUNK + last + 1], (GLA_CHUNK, b.shape[1]))
                           for c in range(n_chunks)], axis=0)
    e_b = jnp.exp(b)
    e_nb = jnp.exp(-b)
    e_end = jnp.exp(tot - b)
    decay = jnp.exp(tot)
    lane = lax.broadcasted_iota(jnp.int32, (n, pw), 1)
    chunk_order = range(n_chunks) if d == 0 else range(n_chunks - 1, -1, -1)

    for p in range(GLA_HEADS // 2):
        sl = slice(p * pw, (p + 1) * pw)
        q_in = slab_ref[bi, :,sl] * e_b[:, sl] * (GLA_DK ** -0.5)
        k_p = slab_ref[bi, :,GLA_K_WIDTH + p * pw:GLA_K_WIDTH + (p + 1) * pw]
        k_in = (k_p * e_nb[:, sl]).astype(BF16)
        k_end = (k_p * e_end[:, sl]).astype(BF16)
        for j in range(2):
            h = 2 * p + j
            q_h = jnp.where((lane < GLA_DK) if j == 0 else (lane >= GLA_DK), q_in, 0.0).astype(BF16)
            v_h = slab_ref[bi, :,2 * GLA_K_WIDTH + h * GLA_DV:2 * GLA_K_WIDTH + (h + 1) * GLA_DV].astype(BF16)
            a = jnp.where(causal, _dot_nt(q_h, k_in), 0.0)
            o_intra = _dot(a.astype(BF16), v_h)
            for c in chunk_order:
                rs = slice(c * GLA_CHUNK, (c + 1) * GLA_CHUNK)
                st = st_ref[d, bi, h]
                o_ref[bi, rs, h * GLA_DV:(h + 1) * GLA_DV] = o_intra[rs] + _dot_nt(q_h[rs], st.astype(BF16))
                st_ref[d, bi, h] = st * decay[c * GLA_CHUNK:c * GLA_CHUNK + 1, sl] + _dot_tn(v_h[rs], k_end[rs])


def _gla_kernel(f_ref, b_ref, wg_ref, bg_ref, of_ref, ob_ref, st_ref):
    @pl.when(pl.program_id(1) == 0)
    def _():
        st_ref[...] = jnp.zeros_like(st_ref)

    for bi in range(f_ref.shape[0]):
        _gla_direction(f_ref, wg_ref, bg_ref, st_ref, of_ref, 0, bi)
        _gla_direction(b_ref, wg_ref, bg_ref, st_ref, ob_ref, 1, bi)


def _gla_scan(p_gla, wg, bg):
    nb, t, _ = p_gla.shape
    n_tiles = t // TOK_TILE
    fwd = lambda b, s: (b, s, 0)
    bwd = lambda b, s: (b, jnp.where(s == 0, 0, n_tiles - s), 0)
    per_step = GLA_BATCH_PER_STEP if nb % GLA_BATCH_PER_STEP == 0 else 1
    return pl.pallas_call(
        _gla_kernel,
        grid=(nb // per_step, n_tiles),
        in_specs=[
            pl.BlockSpec((per_step, TOK_TILE, GLA_COLS), fwd),
            pl.BlockSpec((per_step, TOK_TILE, GLA_COLS), bwd),
            pl.BlockSpec((LANES, 2 * GLA_K_WIDTH), lambda b, s: (0, 0)),
            pl.BlockSpec((1, 2 * GLA_K_WIDTH), lambda b, s: (0, 0)),
        ],
        out_specs=[
            pl.BlockSpec((per_step, TOK_TILE, GLA_V_WIDTH), fwd),
            pl.BlockSpec((per_step, TOK_TILE, GLA_V_WIDTH), bwd),
        ],
        out_shape=[jax.ShapeDtypeStruct((nb, t, GLA_V_WIDTH), F32)] * 2,
        scratch_shapes=[pltpu.VMEM((2, per_step, GLA_HEADS, GLA_DV, 2 * GLA_DK), F32)],
        compiler_params=_cparams("parallel", "arbitrary"),
        name="gla_scan",
    )(p_gla, p_gla, wg, bg)


def _merge_kernel(x_ref, mod_ref, oa_ref, ob_ref, gf_ref, gb_ref, og_ref, gate_ref, gn_ref, n2_ref,
                  wa_ref, wb_ref, wg_ref, wo_ref, x_out_ref, h_out_ref):
    og = og_ref[0]
    o_raw = gf_ref[0] + gb_ref[0]
    parts = []
    for h in range(GLA_HEADS):
        sl = slice(h * GLA_DV, (h + 1) * GLA_DV)
        parts.append(_rms(o_raw[:, sl], gn_ref[...]) * jax.nn.silu(og[:, sl]))
    o_g = jnp.concatenate(parts, axis=-1).astype(BF16)
    d = D_MODEL
    y = (jax.nn.sigmoid(gate_ref[0, :, 0:d]) * _dot(oa_ref[0], wa_ref[...])
         + jax.nn.sigmoid(gate_ref[0, :, d:2 * d]) * _dot(ob_ref[0], wb_ref[...])
         + jax.nn.sigmoid(gate_ref[0, :, 2 * d:3 * d]) * _dot(o_g, wg_ref[...]))
    x = x_ref[0] + mod_ref[0, 2:3, :] * _dot(y.astype(BF16), wo_ref[...])
    x_out_ref[0] = x
    h2 = _rms(x, n2_ref[...]) * (1.0 + mod_ref[0, 4:5, :]) + mod_ref[0, 3:4, :]
    h_out_ref[0] = h2.astype(BF16)


def _merge(xs, mod, o_a, o_b, gla_f, gla_b, p_gla, p_gate, gla_norm, norm2, wa, wb, wg, wo, keep_ctx):
    nb, t, d = xs.shape
    skip = 0 if keep_ctx else 1
    t_out = t - skip * TOK_TILE
    src = lambda b, i: (b, i + skip, 0)
    tok = lambda b, i: (b, i, 0)
    const = lambda b, i: (0, 0)
    og_block = (2 * GLA_K_WIDTH) // GLA_V_WIDTH + 1
    return pl.pallas_call(
        _merge_kernel,
        grid=(nb, t_out // TOK_TILE),
        in_specs=[
            pl.BlockSpec((1, TOK_TILE, d), src),
            pl.BlockSpec((1, N_MOD, d), _mod_row(nb, keep_ctx)),
            pl.BlockSpec((1, TOK_TILE, MLA_WIDTH), src),
            pl.BlockSpec((1, TOK_TILE, NA_WIDTH), src),
            pl.BlockSpec((1, TOK_TILE, GLA_V_WIDTH), src),
            pl.BlockSpec((1, TOK_TILE, GLA_V_WIDTH), src),
            pl.BlockSpec((1, TOK_TILE, GLA_V_WIDTH), lambda b, i: (b, i + skip, og_block)),
            pl.BlockSpec((1, TOK_TILE, GATE_COLS), src),
            pl.BlockSpec((1, GLA_DV), const),
            pl.BlockSpec((1, d), const),
            pl.BlockSpec((MLA_WIDTH, d), const),
            pl.BlockSpec((NA_WIDTH, d), const),
            pl.BlockSpec((GLA_V_WIDTH, d), const),
            pl.BlockSpec((d, d), const),
        ],
        out_specs=[pl.BlockSpec((1, TOK_TILE, d), tok), pl.BlockSpec((1, TOK_TILE, d), tok)],
        out_shape=[jax.ShapeDtypeStruct((nb, t_out, d), F32), jax.ShapeDtypeStruct((nb, t_out, d), BF16)],
        compiler_params=_cparams("parallel", "parallel"),
        name="merge",
    )(xs, mod, o_a, o_b, gla_f, gla_b, p_gla, p_gate, gla_norm.reshape(1, -1), norm2.reshape(1, -1),
      wa, wb, wg, wo)


def _top_ranked(x, k, want_rank=True):
    row = lax.broadcasted_iota(jnp.int32, x.shape, 0).astype(F32)
    out_row = lax.broadcasted_iota(jnp.int32, (k, x.shape[1]), 0)
    vals = jnp.zeros((k, x.shape[1]), F32)
    rank = jnp.full(x.shape, float(k), F32)
    for r in range(k):
        m = jnp.max(x, axis=0, keepdims=True)
        vals = jnp.where(out_row == r, m, vals)
        hit = row == jnp.min(jnp.where(x == m, row, float(x.shape[0])), axis=0, keepdims=True)
        if want_rank:
            rank = jnp.where(hit, float(r), rank)
        x = jnp.where(hit, -jnp.inf, x)
    return vals, rank


def _count(mask):
    return jnp.sum(jnp.where(mask, 1.0, 0.0), axis=0, keepdims=True)


def _pack_rows(x):
    return pltpu.bitcast(x.astype(BF16), jnp.uint32)


def _peer_select_chunk(s0, s1):
    k = PEER_TOPK
    sub = lax.broadcasted_iota(jnp.int32, (SUBLANES, LANES), 0)
    v0, rank0 = _top_ranked(s0, k)
    v1, rank1 = _top_ranked(s1, k)
    cand = [v0[0:1] + v1]
    for a in range(1, k):
        cand.append(jnp.where(sub < k // (a + 1), v0[a:a + 1] + v1[:SUBLANES], -jnp.inf))
    top, _ = _top_ranked(jnp.concatenate(cand, axis=0), k, want_rank=False)
    thr = top[k - 1:k]
    z = jnp.sum(jnp.exp(top - top[0:1]), axis=0, keepdims=True)
    above = [_count(ca > thr) for ca in cand]
    equal = [_count(ca == thr) for ca in cand]
    spare = float(k) - functools.reduce(jnp.add, above)
    n0 = jnp.zeros_like(s0)
    for a in range(k):
        take = jnp.clip(spare, 0.0, equal[a])
        spare = spare - equal[a]
        n0 = jnp.where(rank0 == float(a), above[a] + take, n0)
    return n0, jnp.exp(s0 - v0[0:1]) / z, _pack_rows(rank1), _pack_rows(jnp.exp(s1 - v1[0:1]))


def _sorting_network(n):
    pairs = []
    p = 1
    while p < n:
        k = p
        while k >= 1:
            for j in range(k % p, n - k, 2 * k):
                for i in range(min(k, n - j - k)):
                    if (i + j) // (2 * p) == (i + j + k) // (2 * p):
                        pairs.append((i + j, i + j + k))
            k //= 2
        p *= 2
    return pairs


def _sublane_sum(x):
    for shift in (4, 2, 1):
        x = x + pltpu.roll(x, shift, axis=0)
    return x


def _top_sorted(blocks, k):
    def exchange(xs, i, j):
        xs[i], xs[j] = jnp.maximum(xs[i], xs[j]), jnp.minimum(xs[i], xs[j])

    xs = list(blocks[:k])
    for i, j in _sorting_network(k):
        exchange(xs, i, j)
    for extra in blocks[k:]:
        for r in range(k):
            xs[r], extra = jnp.maximum(xs[r], extra), jnp.minimum(xs[r], extra)
    for shift in (4, 2, 1):
        ys = [pltpu.roll(x, shift, axis=0) for x in xs]
        xs = [jnp.maximum(xs[r], ys[k - 1 - r]) for r in range(k)]
        stride = k // 2
        while stride >= 1:
            for r in range(k):
                if r & stride == 0:
                    exchange(xs, r, r + stride)
            stride //= 2
    return xs


def _peer_select_chunk_distinct(s0, s1):
    k = PEER_TOPK
    nb = s0.shape[0] // SUBLANES
    sub = lax.broadcasted_iota(jnp.int32, (SUBLANES, LANES), 0)
    blocks0 = [s0[SUBLANES * i:SUBLANES * (i + 1)] for i in range(nb)]
    blocks1 = [s1[SUBLANES * i:SUBLANES * (i + 1)] for i in range(nb)]
    v0 = _top_sorted(blocks0, k)
    v1 = _top_sorted(blocks1, k)

    def ties(v, blocks):
        same = functools.reduce(jnp.add, [jnp.where(v[r] == v[r + 1], 1.0, 0.0) for r in range(k - 1)])
        reach = _sublane_sum(functools.reduce(jnp.add, [jnp.where(b >= v[k - 1], 1.0, 0.0) for b in blocks]))
        return same + jnp.abs(reach - float(k))

    v1_lo, v1_hi = v1[SUBLANES - 1], v1[2 * SUBLANES - 1]
    for b in range(SUBLANES - 2, -1, -1):
        v1_lo = jnp.where(sub == b, v1[b], v1_lo)
        v1_hi = jnp.where(sub == b, v1[SUBLANES + b], v1_hi)
    cand = [[v0[0] + v1_lo, v0[0] + v1_hi]]
    for a in range(1, k):
        cand.append([jnp.where(sub < k // (a + 1), v0[a] + v1_lo, -jnp.inf)])
    cand_blocks = [c for ca in cand for c in ca]
    top = _top_sorted(cand_blocks[1:] + cand_blocks[:1], k)
    thr = top[k - 1]
    z = functools.reduce(jnp.add, [jnp.exp(t - top[0]) for t in top])
    n_sel = [_sublane_sum(functools.reduce(jnp.add, [jnp.where(c >= thr, 1.0, 0.0) for c in ca])) for ca in cand]
    redo = ties(v0, blocks0) + ties(v1, blocks1) + ties(top, cand_blocks)

    n0, e0, rank1, e1 = [], [], [], []
    for x0, x1 in zip(blocks0, blocks1):
        n = jnp.zeros_like(x0)
        for a in range(k):
            n = jnp.where(x0 == v0[a], n_sel[a], n)
        n0.append(n)
        e0.append(jnp.exp(x0 - v0[0]) / z)
        rank1.append(functools.reduce(jnp.add, [jnp.where(v > x1, 1.0, 0.0) for v in v1]))
        e1.append(jnp.exp(x1 - v1[0]))
    cat = lambda parts: jnp.concatenate(parts, axis=0)
    return (cat(n0), cat(e0), _pack_rows(cat(rank1)), _pack_rows(cat(e1))), redo[0:1]


def _peer_select_kernel(h_ref, wqt_ref, kp_ref, ht_ref, n0_ref, e0_ref, r1_ref, e1_ref, s_ref):
    h = h_ref[...]
    ht_ref[...] = h.T
    q_t = _dot_nt(wqt_ref[...], h)
    nk = PEER_N_KEYS
    for hd in range(PEER_HEADS):
        s_ref[hd] = _dot(kp_ref[...], q_t[hd * PEER_QUERY_DIM:(hd + 1) * PEER_QUERY_DIM].astype(BF16))

    def run(exact):
        redo = jnp.zeros((1, LANES), F32)
        for hd in range(PEER_HEADS):
            for c in range(h.shape[0] // LANES):
                cs = slice(c * LANES, (c + 1) * LANES)
                s0, s1 = s_ref[hd, :nk, cs], s_ref[hd, nk:, cs]
                if exact:
                    outs = _peer_select_chunk(s0, s1)
                else:
                    outs, flag = _peer_select_chunk_distinct(s0, s1)
                    redo = jnp.maximum(redo, flag)
                for ref, val in zip((n0_ref, e0_ref, r1_ref, e1_ref), outs):
                    ref[hd, :, cs] = val
        return redo

    redo = run(exact=False)

    @pl.when(jnp.max(redo) > 0.0)
    def _():
        run(exact=True)


def _peer_select(h2, wq_t, key_pair):
    n, d = h2.shape
    tm = TOK_TILE
    fac = lambda dt: jax.ShapeDtypeStruct((PEER_HEADS, PEER_N_KEYS // (1 if dt == F32 else 2), n), dt)
    fac_spec = pl.BlockSpec((PEER_HEADS, PEER_N_KEYS, tm), lambda i: (0, 0, i))
    packed_spec = pl.BlockSpec((PEER_HEADS, PEER_N_KEYS // 2, tm), lambda i: (0, 0, i))
    return pl.pallas_call(
        _peer_select_kernel,
        grid=(n // tm,),
        in_specs=[
            pl.BlockSpec((tm, d), lambda i: (i, 0)),
            pl.BlockSpec(wq_t.shape, lambda i: (0, 0)),
            pl.BlockSpec(key_pair.shape, lambda i: (0, 0)),
        ],
        out_specs=[pl.BlockSpec((d, tm), lambda i: (0, i)), fac_spec, fac_spec, packed_spec, packed_spec],
        out_shape=[jax.ShapeDtypeStruct((d, n), BF16), fac(F32), fac(F32), fac(jnp.uint32), fac(jnp.uint32)],
        scratch_shapes=[pltpu.VMEM((PEER_HEADS, 2 * PEER_N_KEYS, tm), F32)],
        compiler_params=_cparams("parallel"),
        name="peer_select",
    )(h2, wq_t, key_pair)


def _peer_dense_kernel(ht_ref, u_ref, vt_ref, n0_ref, e0_ref, r1_ref, e1_ref, o_ref, g_ref, act_ref):
    nk = PEER_N_KEYS
    piece = PEER_PIECE
    n_pieces = PEER_EXP_TILE // piece
    tm = ht_ref.shape[1]

    @pl.when(pl.program_id(1) == 0)
    def _():
        o_ref[...] = jnp.zeros_like(o_ref)

    def gate_piece(p):
        rs_n = nk // 2
        zero = jnp.zeros((rs_n, LANES), BF16)
        for sub in range(piece // nk):
            ii = p * (piece // nk) + sub
            for cs in range(0, tm, LANES):
                for rs in range(0, nk, rs_n):
                    w = zero
                    for hd in range(PEER_HEADS):
                        n_sel = jnp.broadcast_to(n0_ref[hd, ii:ii + 1, cs:cs + LANES], (rs_n, LANES)).astype(BF16)
                        e0 = jnp.broadcast_to(e0_ref[hd, ii:ii + 1, cs:cs + LANES], (rs_n, LANES)).astype(BF16)
                        r1 = pltpu.bitcast(r1_ref[hd, rs // 2:(rs + rs_n) // 2, cs:cs + LANES], BF16)
                        e1 = pltpu.bitcast(e1_ref[hd, rs // 2:(rs + rs_n) // 2, cs:cs + LANES], BF16)
                        w = w + jnp.where(r1 < n_sel, e1 * e0, zero)
                    rows = slice(sub * nk + rs, sub * nk + rs + rs_n)
                    act = act_ref[p, rows, cs:cs + LANES]
                    g_ref[p, rows, cs:cs + LANES] = w * jax.nn.gelu(act).astype(BF16)

    for p in range(n_pieces):
        act_ref[p] = _dot(u_ref[0, p * piece:(p + 1) * piece, :], ht_ref[...])
    for p in range(n_pieces):
        gate_piece(p)
        o_ref[...] += _dot(vt_ref[0, :, p * piece:(p + 1) * piece], g_ref[p])


def _peer_dense(h_t, u, v_t, layer, n0, e0, r1, e1):
    d, n = h_t.shape
    tm, te = PEER_TOK_TILE, PEER_EXP_TILE
    sub = te // PEER_N_KEYS
    return pl.pallas_call(
        _peer_dense_kernel,
        grid=(n // tm, PEER_N_EXPERTS // te),
        in_specs=[
            pl.BlockSpec((d, tm), lambda i, e: (0, i)),
            pl.BlockSpec((1, te, d), lambda i, e: (layer, e, 0)),
            pl.BlockSpec((1, d, te), lambda i, e: (layer, 0, e)),
            pl.BlockSpec((PEER_HEADS, sub, tm), lambda i, e: (0, e, i)),
            pl.BlockSpec((PEER_HEADS, sub, tm), lambda i, e: (0, e, i)),
            pl.BlockSpec((PEER_HEADS, PEER_N_KEYS // 2, tm), lambda i, e: (0, 0, i)),
            pl.BlockSpec((PEER_HEADS, PEER_N_KEYS // 2, tm), lambda i, e: (0, 0, i)),
        ],
        out_specs=pl.BlockSpec((d, tm), lambda i, e: (0, i)),
        out_shape=jax.ShapeDtypeStruct((d, n), F32),
        scratch_shapes=[pltpu.VMEM((te // PEER_PIECE, PEER_PIECE, tm), BF16),
                        pltpu.VMEM((te // PEER_PIECE, PEER_PIECE, tm), F32)],
        compiler_params=_cparams("parallel", "arbitrary"),
        name="peer_dense",
    )(h_t, u, v_t, n0, e0, r1, e1)


def _peer_residual_kernel(x_ref, mod_ref, ot_ref, fn_ref, o_ref, *, final):
    x = x_ref[0] + mod_ref[0, 5:6, :] * ot_ref[...].T
    o_ref[0] = _rms(x, fn_ref[...]) if final else x


def _peer_residual(xs, mod, out_t, final_norm, final):
    nb, t, d = xs.shape
    n_tiles = t // TOK_TILE
    tok = lambda b, i: (b, i, 0)
    return pl.pallas_call(
        functools.partial(_peer_residual_kernel, final=final),
        grid=(nb, n_tiles),
        in_specs=[
            pl.BlockSpec((1, TOK_TILE, d), tok),
            pl.BlockSpec((1, N_MOD, d), _mod_row(nb, has_ctx=not final)),
            pl.BlockSpec((d, TOK_TILE), lambda b, i: (0, b * n_tiles + i)),
            pl.BlockSpec((1, d), lambda b, i: (0, 0)),
        ],
        out_specs=pl.BlockSpec((1, TOK_TILE, d), tok),
        out_shape=jax.ShapeDtypeStruct((nb, t, d), F32),
        compiler_params=_cparams("parallel", "parallel"),
        name="peer_residual",
    )(xs, mod, out_t, final_norm.reshape(1, d))


def _rot_cols(w):
    q = MLA_ROPE // 4
    return jnp.concatenate([-w[..., q:2 * q], w[..., :q], -w[..., 3 * q:], w[..., 2 * q:3 * q]], axis=-1)


def _rope_slab(w):
    return jnp.pad(w, ((0, 0), (MLA_NOPE, LANES - MLA_NOPE - MLA_ROPE)))


def _prep_w_in(w_in):
    o = MLA_Q_LORA + MLA_KV_LORA
    w_kr = w_in[:, o:o + MLA_ROPE]
    o2 = o + MLA_ROPE
    n_mid = NA_COLS + 2 * GLA_K_WIDTH + 2 * GLA_V_WIDTH
    low = w_in[:, o2 + n_mid:o2 + n_mid + 2 * GLA_GATE_RANK]
    parts = [w_in[:, :o], _rope_slab(w_kr), _rope_slab(_rot_cols(w_kr)), w_in[:, o2:o2 + n_mid],
             jnp.pad(low, ((0, 0), (0, LANES - 2 * GLA_GATE_RANK))), w_in[:, o2 + n_mid + 2 * GLA_GATE_RANK:]]
    return jnp.concatenate(parts, axis=1).astype(BF16)


def _prep_mla(w_uq, w_ukv):
    r = w_uq.shape[0]
    wq = w_uq.reshape(r, MLA_HEADS, MLA_NOPE + MLA_ROPE)
    pad = MLA_HEAD_PAD - MLA_NOPE - MLA_ROPE
    wq_p = jnp.pad(wq, ((0, 0), (0, 0), (0, pad)))
    wq_rot = jnp.pad(_rot_cols(wq[..., MLA_NOPE:]), ((0, 0), (0, 0), (MLA_NOPE, pad)))
    c = w_ukv.shape[0]
    wkv = w_ukv.reshape(c, MLA_HEADS, MLA_NOPE + MLA_V)
    wk = jnp.pad(wkv[..., :MLA_NOPE], ((0, 0), (0, 0), (0, MLA_HEAD_PAD - MLA_NOPE)))
    wv = wkv[..., MLA_NOPE:]
    hw = MLA_HEADS * MLA_HEAD_PAD
    lane = np.arange(LANES)
    col = np.arange(hw)
    is_rope = (lane >= MLA_NOPE) & (lane < MLA_NOPE + MLA_ROPE)
    e = (is_rope[:, None] & (col[None, :] % MLA_HEAD_PAD == lane[:, None])).astype(np.float32)
    return (wq_p.reshape(r, hw).astype(BF16), wq_rot.reshape(r, hw).astype(BF16),
            wk.reshape(c, hw).astype(BF16), jnp.asarray(e, BF16), wv.reshape(c, MLA_WIDTH).astype(BF16))


def _rope_tables(seq):
    pos = jnp.arange(seq)
    rows = (pos // GRID_W).astype(F32)
    cols = (pos % GRID_W).astype(F32)
    half = MLA_ROPE // 2
    inv = ROPE_THETA ** (-jnp.arange(0, half, 2, dtype=F32) / half)
    ar = rows[:, None] * inv
    ac = cols[:, None] * inv
    ang = jnp.concatenate([ar, ar, ac, ac], axis=-1)
    ang = jnp.concatenate([jnp.zeros((CTX_LEN, MLA_ROPE), F32), ang], axis=0)
    t = ang.shape[0]
    ones = jnp.ones((t, MLA_NOPE), F32)
    zeros = jnp.zeros((t, LANES - MLA_NOPE - MLA_ROPE), F32)
    cos_t = jnp.concatenate([ones, jnp.cos(ang), zeros], axis=1)
    sin_t = jnp.concatenate([0.0 * ones, jnp.sin(ang), zeros], axis=1)
    return cos_t, sin_t


def _prep_gla_gate(w_f, b_f, w_b, b_b):
    wg = jnp.zeros((LANES, 2 * GLA_K_WIDTH), F32)
    wg = wg.at[:GLA_GATE_RANK, :GLA_K_WIDTH].set(w_f)
    wg = wg.at[GLA_GATE_RANK:2 * GLA_GATE_RANK, GLA_K_WIDTH:].set(w_b)
    return wg, jnp.concatenate([b_f, b_b]).reshape(1, -1)


def _prep_peer_keys(sub_keys):
    half = PEER_QUERY_DIM // 2
    kp = jnp.zeros((2 * PEER_N_KEYS, PEER_QUERY_DIM), F32)
    kp = kp.at[:PEER_N_KEYS, :half].set(sub_keys[0])
    kp = kp.at[PEER_N_KEYS:, half:].set(sub_keys[1])
    return kp.astype(BF16)


def kernel(x, c, ctx, c_ctx, w_ada, b_ada, norm1, w_in, mla_q_norm, mla_w_uq, mla_kv_norm, mla_w_ukv, na_rpb, gla_w_gk_fwd, gla_b_gk_fwd, gla_w_gk_bwd, gla_b_gk_bwd, gla_norm, w_o_mla, w_o_na, w_o_gla, w_out, norm2, peer_w_q, peer_sub_keys, peer_u, peer_v, final_norm):
    nb, seq, d = x.shape
    assert ctx.shape[1] == CTX_LEN == TOK_TILE and seq % (NA_ROWS_PER_TILE * GRID_W) == 0
    assert NA_ROWS_PER_TILE * GRID_W == TOK_TILE and nb < 8
    rows = seq // GRID_W
    depth = w_ada.shape[0]

    xs = jnp.concatenate([ctx, x], axis=1)
    t = xs.shape[1]
    cc = jnp.zeros((8, d), F32).at[:nb].set(c).at[nb].set(c_ctx)
    mods = _modulation(cc, w_ada, b_ada).reshape(depth, 8, N_MOD, d)
    cos_t, sin_t = _rope_tables(seq)
    u_all = peer_u.astype(BF16)
    vt_all = jnp.swapaxes(peer_v, 1, 2).astype(BF16)

    for l in range(depth):
        mod = mods[l]
        p_mla, p_na, p_gla, p_gate = _in_projection(xs, mod, norm1[l], _prep_w_in(w_in[l]))
        q, k, v = _mla_prep(p_mla, cos_t, sin_t, mla_q_norm[l], mla_kv_norm[l], *_prep_mla(mla_w_uq[l], mla_w_ukv[l]))
        o_a = _mla_attention(q, k, v)
        o_b = _na_attention(p_na, _na_bias_table(na_rpb[l], rows), rows)
        gla_f, gla_b = _gla_scan(p_gla, *_prep_gla_gate(gla_w_gk_fwd[l], gla_b_gk_fwd[l], gla_w_gk_bwd[l], gla_b_gk_bwd[l]))
        last = l == depth - 1
        xs, h2 = _merge(xs, mod, o_a, o_b, gla_f, gla_b, p_gla, p_gate, gla_norm[l], norm2[l],
                        w_o_mla[l].astype(BF16), w_o_na[l].astype(BF16), w_o_gla[l].astype(BF16), w_out[l].astype(BF16),
                        keep_ctx=not last)
        h_t, n0, e0, r1, e1 = _peer_select(h2.reshape(-1, d), peer_w_q[l].T.astype(BF16),
                                           _prep_peer_keys(peer_sub_keys[l]))
        out_t = _peer_dense(h_t, u_all, vt_all, l, n0, e0, r1, e1)
        xs = _peer_residual(xs, mod, out_t, final_norm, final=last)
    return xs
```

```python
import functools

import numpy as np
import jax
import jax.numpy as jnp
from jax import lax
from jax.experimental import pallas as pl
from jax.experimental.pallas import tpu as pltpu

F32 = jnp.float32
BF16 = jnp.bfloat16
HIGHEST = lax.Precision.HIGHEST

D_MODEL = 1024
DEPTH = 2
CTX_LEN = 256
GRID_W = 64
EPS = 1e-6
N_MOD = 6
NEG_INF = -1e30

MLA_HEADS = 8
MLA_Q_LORA = 384
MLA_KV_LORA = 256
MLA_NOPE = 64
MLA_ROPE = 32
MLA_V = 64
ROPE_THETA = 10000.0
MLA_HEAD_PAD = 128
MLA_PAIRS_PER_STEP = 4

NA_HEADS = 8
NA_HEAD_DIM = 64
NA_KH = 8
NA_KW = 16
NA_ROWS_PER_TILE = 4
NA_SPAN = NA_ROWS_PER_TILE + NA_KH
NA_PAIRS_PER_STEP = 4

GLA_HEADS = 4
GLA_DK = 64
GLA_DV = 128
GLA_GATE_RANK = 16
GLA_GATE_NORM = 16.0
GLA_CHUNK = 64
GLA_BATCH_PER_STEP = 1

PEER_HEADS = 8
PEER_N_KEYS = 128
PEER_N_EXPERTS = PEER_N_KEYS * PEER_N_KEYS
PEER_QUERY_DIM = 128
PEER_TOPK = 16
PEER_TOK_TILE = 512
PEER_EXP_TILE = 1024
PEER_PIECE = 256

MLA_WIDTH = MLA_HEADS * MLA_V
NA_WIDTH = NA_HEADS * NA_HEAD_DIM
GLA_K_WIDTH = GLA_HEADS * GLA_DK
GLA_V_WIDTH = GLA_HEADS * GLA_DV

LANES = 128
SUBLANES = 8
TOK_TILE = 256

MLA_COLS = MLA_Q_LORA + MLA_KV_LORA + 2 * LANES
NA_COLS = 3 * NA_WIDTH
GLA_COLS = 2 * GLA_K_WIDTH + 2 * GLA_V_WIDTH + LANES
GATE_COLS = 3 * D_MODEL
IN_COLS_PAD = MLA_COLS + NA_COLS + GLA_COLS + GATE_COLS

VMEM_LIMIT = 56 * 1024 * 1024


def _cparams(*sem, flags=None):
    return pltpu.CompilerParams(dimension_semantics=sem, vmem_limit_bytes=VMEM_LIMIT, flags=flags)


def _rms(x, w):
    return x * lax.rsqrt(jnp.mean(x * x, axis=-1, keepdims=True) + EPS) * w


def _dot(a, b):
    return jnp.dot(a, b, preferred_element_type=F32)


def _dot_nt(a, b):
    return lax.dot_general(a, b, (((1,), (1,)), ((), ())), preferred_element_type=F32)


def _dot_tn(a, b):
    return lax.dot_general(a, b, (((0,), (0,)), ((), ())), preferred_element_type=F32)


def _dot_exact(a, b):
    return jnp.dot(a, b, preferred_element_type=F32, precision=HIGHEST)


def _dot_mask_exact(mask, x):
    m = mask.astype(BF16)
    hi = x.astype(BF16)
    rest = x - hi.astype(F32)
    mid = rest.astype(BF16)
    lo = (rest - mid.astype(F32)).astype(BF16)
    return _dot(m, hi) + _dot(m, mid) + _dot(m, lo)


def _mod_kernel(c_ref, w_ref, b_ref, o_ref):
    c = c_ref[...]
    o_ref[0] = _dot_exact(c * jax.nn.sigmoid(c), w_ref[0]) + b_ref[0]


def _modulation(cc, w_ada, b_ada):
    depth, d, n = w_ada.shape
    tn = 1536
    return pl.pallas_call(
        _mod_kernel,
        grid=(depth, n // tn),
        in_specs=[
            pl.BlockSpec((8, d), lambda l, j: (0, 0)),
            pl.BlockSpec((1, d, tn), lambda l, j: (l, 0, j)),
            pl.BlockSpec((1, 1, tn), lambda l, j: (l, 0, j)),
        ],
        out_specs=pl.BlockSpec((1, 8, tn), lambda l, j: (l, 0, j)),
        out_shape=jax.ShapeDtypeStruct((depth, 8, n), F32),
        compiler_params=_cparams("parallel", "parallel"),
        name="adaln_mod",
    )(cc, w_ada, b_ada.reshape(depth, 1, n))


def _mod_row(n_batch, has_ctx=True):
    if not has_ctx:
        return lambda b, i: (b, 0, 0)
    return lambda b, i: (jnp.where(i == 0, n_batch, b), 0, 0)


def _inproj_kernel(x_ref, mod_ref, nw_ref, w_ref, mla_ref, na_ref, gla_ref, gate_ref):
    h = _rms(x_ref[0], nw_ref[...]) * (1.0 + mod_ref[0, 1:2, :]) + mod_ref[0, 0:1, :]
    h = h.astype(BF16)
    o = 0
    mla_ref[0] = _dot(h, w_ref[:, o:o + MLA_COLS])
    o += MLA_COLS
    na_ref[0] = _dot(h, w_ref[:, o:o + NA_COLS]).astype(BF16)
    o += NA_COLS
    gla_ref[0] = _dot(h, w_ref[:, o:o + GLA_COLS])
    o += GLA_COLS
    gate_ref[0] = _dot(h, w_ref[:, o:o + GATE_COLS])


def _in_projection(xs, mod, norm_w, w_in_p):
    nb, t, d = xs.shape
    tok = lambda b, i: (b, i, 0)
    return pl.pallas_call(
        _inproj_kernel,
        grid=(nb, t // TOK_TILE),
        in_specs=[
            pl.BlockSpec((1, TOK_TILE, d), tok),
            pl.BlockSpec((1, N_MOD, d), _mod_row(nb)),
            pl.BlockSpec((1, d), lambda b, i: (0, 0)),
            pl.BlockSpec((d, IN_COLS_PAD), lambda b, i: (0, 0), pipeline_mode=pl.Buffered(1)),
        ],
        out_specs=[
            pl.BlockSpec((1, TOK_TILE, MLA_COLS), tok),
            pl.BlockSpec((1, TOK_TILE, NA_COLS), tok),
            pl.BlockSpec((1, TOK_TILE, GLA_COLS), tok),
            pl.BlockSpec((1, TOK_TILE, GATE_COLS), tok),
        ],
        out_shape=[
            jax.ShapeDtypeStruct((nb, t, MLA_COLS), F32),
            jax.ShapeDtypeStruct((nb, t, NA_COLS), BF16),
            jax.ShapeDtypeStruct((nb, t, GLA_COLS), F32),
            jax.ShapeDtypeStruct((nb, t, GATE_COLS), F32),
        ],
        compiler_params=_cparams("parallel", "parallel"),
        name="in_proj",
    )(xs, mod, norm_w.reshape(1, d), w_in_p)


def _mla_prep_kernel(p_ref, cos_ref, sin_ref, qn_ref, kvn_ref, wq_ref, wqr_ref, wk_ref, e_ref, wv_ref,
                     q_ref, k_ref, v_ref):
    scale = (MLA_NOPE + MLA_ROPE) ** -0.5
    cos = cos_ref[...]
    sin = sin_ref[...]
    o = MLA_Q_LORA + MLA_KV_LORA
    nq = _rms(p_ref[0, :, :MLA_Q_LORA], qn_ref[...]).astype(BF16)
    nkv = _rms(p_ref[0, :, MLA_Q_LORA:o], kvn_ref[...]).astype(BF16)
    q = _dot(nq, wq_ref[...])
    q_rot = _dot(nq, wqr_ref[...])
    for h in range(MLA_HEADS):
        sl = slice(h * MLA_HEAD_PAD, (h + 1) * MLA_HEAD_PAD)
        q_ref[0, :, sl] = ((q[:, sl] * cos + q_rot[:, sl] * sin) * scale).astype(BF16)
    k_rope = (p_ref[0, :, o:o + LANES] * cos + p_ref[0, :, o + LANES:o + 2 * LANES] * sin).astype(BF16)
    k_ref[0] = (_dot(nkv, wk_ref[...]) + _dot(k_rope, e_ref[...])).astype(BF16)
    v_ref[0] = _dot(nkv, wv_ref[...]).astype(BF16)


def _mla_prep(p_mla, cos_t, sin_t, q_norm, kv_norm, wq, wq_rot, wk, e_rope, wv):
    nb, t, _ = p_mla.shape
    hw = MLA_HEADS * MLA_HEAD_PAD
    tok = lambda b, i: (b, i, 0)
    const = lambda b, i: (0, 0)
    return pl.pallas_call(
        _mla_prep_kernel,
        grid=(nb, t // TOK_TILE),
        in_specs=[
            pl.BlockSpec((1, TOK_TILE, MLA_COLS), tok),
            pl.BlockSpec((TOK_TILE, LANES), lambda b, i: (i, 0)),
            pl.BlockSpec((TOK_TILE, LANES), lambda b, i: (i, 0)),
            pl.BlockSpec((1, MLA_Q_LORA), const),
            pl.BlockSpec((1, MLA_KV_LORA), const),
            pl.BlockSpec((MLA_Q_LORA, hw), const),
            pl.BlockSpec((MLA_Q_LORA, hw), const),
            pl.BlockSpec((MLA_KV_LORA, hw), const),
            pl.BlockSpec((LANES, hw), const),
            pl.BlockSpec((MLA_KV_LORA, MLA_WIDTH), const),
        ],
        out_specs=[
            pl.BlockSpec((1, TOK_TILE, hw), tok),
            pl.BlockSpec((1, TOK_TILE, hw), tok),
            pl.BlockSpec((1, TOK_TILE, MLA_WIDTH), tok),
        ],
        out_shape=[
            jax.ShapeDtypeStruct((nb, t, hw), BF16),
            jax.ShapeDtypeStruct((nb, t, hw), BF16),
            jax.ShapeDtypeStruct((nb, t, MLA_WIDTH), BF16),
        ],
        compiler_params=_cparams("parallel", "parallel"),
        name="mla_prep",
    )(p_mla, cos_t, sin_t, q_norm.reshape(1, -1), kv_norm.reshape(1, -1), wq, wq_rot, wk, e_rope, wv)


def _softmax_pv(s_list, v_list):
    m = s_list[0].max(axis=-1, keepdims=True)
    for s in s_list[1:]:
        m = jnp.maximum(m, s.max(axis=-1, keepdims=True))
    acc = None
    den = None
    for s, v in zip(s_list, v_list):
        p = jnp.exp(s - m)
        l = p.sum(axis=-1, keepdims=True)
        o = _dot(p.astype(BF16), v)
        acc = o if acc is None else acc + o
        den = l if den is None else den + l
    return acc / den


def _pair_select(o0, o1, half):
    lane = lax.broadcasted_iota(jnp.int32, o0.shape, 1)
    return jnp.where(lane < half, o0, o1)


def _mla_attn_kernel(q_ref, k_ref, v_ref, o_ref, *, ctx_queries):
    def attend(nk):
        for pr in range(MLA_PAIRS_PER_STEP):
            vs = slice(pr * 2 * MLA_V, (pr + 1) * 2 * MLA_V)
            outs = []
            for j in range(2):
                h = 2 * pr + j
                sl = slice(h * MLA_HEAD_PAD, (h + 1) * MLA_HEAD_PAD)
                s = _dot_nt(q_ref[0, :, sl], k_ref[0, :nk, sl])
                outs.append(_softmax_pv([s], [v_ref[0, :nk, vs]]))
            o_ref[0, :, vs] = _pair_select(outs[0], outs[1], MLA_V).astype(BF16)

    is_ctx = pl.program_id(2) == 0

    @pl.when(is_ctx)
    def _():
        if ctx_queries:
            attend(CTX_LEN)
        else:
            o_ref[...] = jnp.zeros_like(o_ref)

    @pl.when(jnp.logical_not(is_ctx))
    def _():
        attend(k_ref.shape[1])


def _mla_attention(q, k, v, ctx_queries):
    nb, t, _ = q.shape
    heads = 2 * MLA_PAIRS_PER_STEP
    pw = heads * MLA_HEAD_PAD
    return pl.pallas_call(
        functools.partial(_mla_attn_kernel, ctx_queries=ctx_queries),
        grid=(nb, MLA_HEADS // heads, t // TOK_TILE),
        in_specs=[
            pl.BlockSpec((1, TOK_TILE, pw), lambda b, p, i: (b, i, p)),
            pl.BlockSpec((1, t, pw), lambda b, p, i: (b, 0, p)),
            pl.BlockSpec((1, t, heads * MLA_V), lambda b, p, i: (b, 0, p)),
        ],
        out_specs=pl.BlockSpec((1, TOK_TILE, heads * MLA_V), lambda b, p, i: (b, i, p)),
        out_shape=jax.ShapeDtypeStruct((nb, t, MLA_WIDTH), BF16),
        compiler_params=_cparams("parallel", "parallel", "arbitrary"),
        name="mla_attn",
    )(q, k, v)


def _na_tile_geometry(rows):
    n_tiles = rows // NA_ROWS_PER_TILE
    r0 = np.arange(n_tiles) * NA_ROWS_PER_TILE
    kb = np.clip(r0 - NA_KH // 2, 0, rows - NA_SPAN)
    pat = np.where(r0 == 0, 0, np.where(r0 == rows - NA_ROWS_PER_TILE, 2, 1))
    return kb, pat


def _na_bias_table(rpb, rows):
    nh, nd, _ = rpb.shape
    w = GRID_W
    kh = min(NA_KH, rows)
    edge = w - NA_KW
    r_ext = jnp.concatenate([jnp.broadcast_to(rpb[..., :1], (nh, nd, edge)), rpb,
                             jnp.broadcast_to(rpb[..., -1:], (nh, nd, edge)), jnp.zeros((nh, nd, 1), rpb.dtype)], axis=-1)
    toe = jnp.tile(r_ext, (1, 1, w))[..., :w * (2 * w - 1)].reshape(nh, nd, w, 2 * w - 1)[..., w - 1:]
    col = np.arange(w)
    col_start = np.clip(col - NA_KW // 2, 0, w - NA_KW)
    col_in = (col[None, :] >= col_start[:, None]) & (col[None, :] < col_start[:, None] + NA_KW)
    toe = jnp.where(col_in[None, None], toe, NEG_INF).astype(F32)
    masked = jnp.full((nh, w, w), NEG_INF, F32)
    kb, _ = _na_tile_geometry(rows)
    n_tiles = rows // NA_ROWS_PER_TILE
    pats = []
    for tile in (0, 1, n_tiles - 1):
        q_rows = []
        for dq in range(NA_ROWS_PER_TILE):
            r = tile * NA_ROWS_PER_TILE + dq
            start = int(np.clip(r - kh // 2, 0, rows - kh))
            blocks = []
            for dk in range(NA_SPAN):
                kr = int(kb[tile]) + dk
                blocks.append(toe[:, kr - r + NA_KH - 1] if start <= kr < start + kh else masked)
            q_rows.append(jnp.concatenate(blocks, axis=-1))
        pats.append(jnp.concatenate(q_rows, axis=1))
    return jnp.stack(pats, axis=1)


def _na_attn_kernel(kb_ref, pat_ref, q_ref, k_ref, v_ref, bias_ref, o_ref, *, ctx_queries):
    i = pl.program_id(2)
    scale = NA_HEAD_DIM ** -0.5
    pw = 2 * NA_HEAD_DIM
    lane = lax.broadcasted_iota(jnp.int32, (TOK_TILE, pw), 1)

    def head_q(pr, j):
        q = q_ref[0, :, pr * pw:(pr + 1) * pw].astype(F32)
        return jnp.where((lane < NA_HEAD_DIM) if j == 0 else (lane >= NA_HEAD_DIM), q, 0.0).astype(BF16)

    @pl.when(i == 0)
    def _():
        if not ctx_queries:
            o_ref[...] = jnp.zeros_like(o_ref)
            return
        for pr in range(NA_PAIRS_PER_STEP):
            ps = slice(pr * pw, (pr + 1) * pw)
            outs = []
            for j in range(2):
                s = _dot_nt(head_q(pr, j), k_ref[0, :CTX_LEN, ps]) * scale
                outs.append(_softmax_pv([s], [v_ref[0, :CTX_LEN, ps]]))
            o_ref[0, :, ps] = _pair_select(outs[0], outs[1], NA_HEAD_DIM).astype(BF16)

    @pl.when(i > 0)
    def _():
        start = pl.multiple_of(CTX_LEN + kb_ref[i - 1] * GRID_W, GRID_W)
        nk = NA_SPAN * GRID_W
        pat = pat_ref[i - 1]
        for pr in range(NA_PAIRS_PER_STEP):
            ps = slice(pr * pw, (pr + 1) * pw)
            k_lat = k_ref[0, pl.ds(start, nk), ps]
            v_lat = v_ref[0, pl.ds(start, nk), ps]
            outs = []
            for j in range(2):
                qj = head_q(pr, j)
                s_lat = _dot_nt(qj, k_lat) * scale + bias_ref[2 * pr + j, pat]
                s_ctx = _dot_nt(qj, k_ref[0, :CTX_LEN, ps]) * scale
                outs.append(_softmax_pv([s_lat, s_ctx], [v_lat, v_ref[0, :CTX_LEN, ps]]))
            o_ref[0, :, ps] = _pair_select(outs[0], outs[1], NA_HEAD_DIM).astype(BF16)


def _na_attention(p_na, bias, rows, ctx_queries):
    nb, t, _ = p_na.shape
    heads = 2 * NA_PAIRS_PER_STEP
    pw = heads * NA_HEAD_DIM
    n_groups = NA_HEADS // heads
    kb, pat = _na_tile_geometry(rows)
    nkeys = NA_SPAN * GRID_W
    grid_spec = pltpu.PrefetchScalarGridSpec(
        num_scalar_prefetch=2,
        grid=(n_groups, nb, t // TOK_TILE),
        in_specs=[
            pl.BlockSpec((1, TOK_TILE, pw), lambda p, b, i, *_: (b, i, p)),
            pl.BlockSpec((1, t, pw), lambda p, b, i, *_: (b, 0, n_groups + p)),
            pl.BlockSpec((1, t, pw), lambda p, b, i, *_: (b, 0, 2 * n_groups + p)),
            pl.BlockSpec((heads, 3, TOK_TILE, nkeys), lambda p, b, i, *_: (p, 0, 0, 0),
                         pipeline_mode=pl.Buffered(1 if n_groups == 1 else 2)),
        ],
        out_specs=pl.BlockSpec((1, TOK_TILE, pw), lambda p, b, i, *_: (b, i, p)),
    )
    return pl.pallas_call(
        functools.partial(_na_attn_kernel, ctx_queries=ctx_queries),
        grid_spec=grid_spec,
        out_shape=jax.ShapeDtypeStruct((nb, t, NA_WIDTH), BF16),
        compiler_params=_cparams("parallel", "parallel", "arbitrary"),
        name="na_attn",
    )(jnp.asarray(kb, jnp.int32), jnp.asarray(pat, jnp.int32), p_na, p_na, p_na, bias)


def _gla_direction(slab_ref, wg_ref, bg_ref, st_ref, o_ref, d, bi):
    n = TOK_TILE
    n_chunks = n // GLA_CHUNK
    pw = 2 * GLA_DK
    row = lax.broadcasted_iota(jnp.int32, (n, n), 0)
    col = lax.broadcasted_iota(jnp.int32, (n, n), 1)
    shift = GLA_CHUNK.bit_length() - 1
    same_chunk = jnp.right_shift(row, shift) == jnp.right_shift(col, shift)
    causal = same_chunk & ((col <= row) if d == 0 else (col >= row))
    tri = jnp.where(causal, 1.0, 0.0).astype(F32)

    o_low = 2 * GLA_K_WIDTH + 2 * GLA_V_WIDTH
    gw = GLA_K_WIDTH
    g_raw = _dot_exact(slab_ref[bi, :,o_low:o_low + LANES], wg_ref[:, d * gw:(d + 1) * gw]) + bg_ref[:, d * gw:(d + 1) * gw]
    g = jax.nn.log_sigmoid(g_raw) / GLA_GATE_NORM
    b = _dot_mask_exact(tri, g)
    last = GLA_CHUNK - 1 if d == 0 else 0
    tot = jnp.concatenate([jnp.broadcast_to(b[c * GLA_CHUNK + last:c * GLA_CHUNK + last + 1], (GLA_CHUNK, b.shape[1]))
                           for c in range(n_chunks)], axis=0)
    e_b = jnp.exp(b)
    e_nb = jnp.exp(-b)
    e_end = jnp.exp(tot - b)
    decay = jnp.exp(tot)
    lane = lax.broadcasted_iota(jnp.int32, (n, pw), 1)
    chunk_order = range(n_chunks) if d == 0 else range(n_chunks - 1, -1, -1)

    for p in range(GLA_HEADS // 2):
        sl = slice(p * pw, (p + 1) * pw)
        q_in = slab_ref[bi, :,sl] * e_b[:, sl] * (GLA_DK ** -0.5)
        k_p = slab_ref[bi, :,GLA_K_WIDTH + p * pw:GLA_K_WIDTH + (p + 1) * pw]
        k_in = (k_p * e_nb[:, sl]).astype(BF16)
        k_end = (k_p * e_end[:, sl]).astype(BF16)
        for j in range(2):
            h = 2 * p + j
            q_h = jnp.where((lane < GLA_DK) if j == 0 else (lane >= GLA_DK), q_in, 0.0).astype(BF16)
            v_h = slab_ref[bi, :,2 * GLA_K_WIDTH + h * GLA_DV:2 * GLA_K_WIDTH + (h + 1) * GLA_DV].astype(BF16)
            a = jnp.where(causal, _dot_nt(q_h, k_in), 0.0)
            o_intra = _dot(a.astype(BF16), v_h)
            for c in chunk_order:
                rs = slice(c * GLA_CHUNK, (c + 1) * GLA_CHUNK)
                st = st_ref[d, bi, h]
                o_ref[bi, rs, h * GLA_DV:(h + 1) * GLA_DV] = o_intra[rs] + _dot_nt(q_h[rs], st.astype(BF16))
                st_ref[d, bi, h] = st * decay[c * GLA_CHUNK:c * GLA_CHUNK + 1, sl] + _dot_tn(v_h[rs], k_end[rs])


def _gla_kernel(f_ref, b_ref, wg_ref, bg_ref, of_ref, ob_ref, st_ref):
    @pl.when(pl.program_id(1) == 0)
    def _():
        st_ref[...] = jnp.zeros_like(st_ref)

    for bi in range(f_ref.shape[0]):
        _gla_direction(f_ref, wg_ref, bg_ref, st_ref, of_ref, 0, bi)
        _gla_direction(b_ref, wg_ref, bg_ref, st_ref, ob_ref, 1, bi)


def _gla_scan(p_gla, wg, bg):
    nb, t, _ = p_gla.shape
    n_tiles = t // TOK_TILE
    fwd = lambda b, s: (b, s, 0)
    bwd = lambda b, s: (b, jnp.where(s == 0, 0, n_tiles - s), 0)
    per_step = GLA_BATCH_PER_STEP if nb % GLA_BATCH_PER_STEP == 0 else 1
    return pl.pallas_call(
        _gla_kernel,
        grid=(nb // per_step, n_tiles),
        in_specs=[
            pl.BlockSpec((per_step, TOK_TILE, GLA_COLS), fwd),
            pl.BlockSpec((per_step, TOK_TILE, GLA_COLS), bwd),
            pl.BlockSpec((LANES, 2 * GLA_K_WIDTH), lambda b, s: (0, 0)),
            pl.BlockSpec((1, 2 * GLA_K_WIDTH), lambda b, s: (0, 0)),
        ],
        out_specs=[
            pl.BlockSpec((per_step, TOK_TILE, GLA_V_WIDTH), fwd),
            pl.BlockSpec((per_step, TOK_TILE, GLA_V_WIDTH), bwd),
        ],
        out_shape=[jax.ShapeDtypeStruct((nb, t, GLA_V_WIDTH), F32)] * 2,
        scratch_shapes=[pltpu.VMEM((2, per_step, GLA_HEADS, GLA_DV, 2 * GLA_DK), F32)],
        compiler_params=_cparams("parallel", "arbitrary"),
        name="gla_scan",
    )(p_gla, p_gla, wg, bg)


def _merge_kernel(x_ref, mod_ref, oa_ref, ob_ref, gf_ref, gb_ref, og_ref, gate_ref, gn_ref, n2_ref,
                  wa_ref, wb_ref, wg_ref, wo_ref, x_out_ref, h_out_ref):
    og = og_ref[0]
    o_raw = gf_ref[0] + gb_ref[0]
    parts = []
    for h in range(GLA_HEADS):
        sl = slice(h * GLA_DV, (h + 1) * GLA_DV)
        parts.append(_rms(o_raw[:, sl], gn_ref[...]) * jax.nn.silu(og[:, sl]))
    o_g = jnp.concatenate(parts, axis=-1).astype(BF16)
    d = D_MODEL
    y = (jax.nn.sigmoid(gate_ref[0, :, 0:d]) * _dot(oa_ref[0], wa_ref[...])
         + jax.nn.sigmoid(gate_ref[0, :, d:2 * d]) * _dot(ob_ref[0], wb_ref[...])
         + jax.nn.sigmoid(gate_ref[0, :, 2 * d:3 * d]) * _dot(o_g, wg_ref[...]))
    x = x_ref[0] + mod_ref[0, 2:3, :] * _dot(y.astype(BF16), wo_ref[...])
    x_out_ref[0] = x
    h2 = _rms(x, n2_ref[...]) * (1.0 + mod_ref[0, 4:5, :]) + mod_ref[0, 3:4, :]
    h_out_ref[0] = h2.astype(BF16)


def _merge(xs, mod, o_a, o_b, gla_f, gla_b, p_gla, p_gate, gla_norm, norm2, wa, wb, wg, wo, keep_ctx):
    nb, t, d = xs.shape
    skip = 0 if keep_ctx else 1
    t_out = t - skip * TOK_TILE
    src = lambda b, i: (b, i + skip, 0)
    tok = lambda b, i: (b, i, 0)
    const = lambda b, i: (0, 0)
    og_block = (2 * GLA_K_WIDTH) // GLA_V_WIDTH + 1
    return pl.pallas_call(
        _merge_kernel,
        grid=(nb, t_out // TOK_TILE),
        in_specs=[
            pl.BlockSpec((1, TOK_TILE, d), src),
            pl.BlockSpec((1, N_MOD, d), _mod_row(nb, keep_ctx)),
            pl.BlockSpec((1, TOK_TILE, MLA_WIDTH), src),
            pl.BlockSpec((1, TOK_TILE, NA_WIDTH), src),
            pl.BlockSpec((1, TOK_TILE, GLA_V_WIDTH), src),
            pl.BlockSpec((1, TOK_TILE, GLA_V_WIDTH), src),
            pl.BlockSpec((1, TOK_TILE, GLA_V_WIDTH), lambda b, i: (b, i + skip, og_block)),
            pl.BlockSpec((1, TOK_TILE, GATE_COLS), src),
            pl.BlockSpec((1, GLA_DV), const),
            pl.BlockSpec((1, d), const),
            pl.BlockSpec((MLA_WIDTH, d), const),
            pl.BlockSpec((NA_WIDTH, d), const),
            pl.BlockSpec((GLA_V_WIDTH, d), const),
            pl.BlockSpec((d, d), const),
        ],
        out_specs=[pl.BlockSpec((1, TOK_TILE, d), tok), pl.BlockSpec((1, TOK_TILE, d), tok)],
        out_shape=[jax.ShapeDtypeStruct((nb, t_out, d), F32), jax.ShapeDtypeStruct((nb, t_out, d), BF16)],
        compiler_params=_cparams("parallel", "parallel"),
        name="merge",
    )(xs, mod, o_a, o_b, gla_f, gla_b, p_gla, p_gate, gla_norm.reshape(1, -1), norm2.reshape(1, -1),
      wa, wb, wg, wo)


def _top_ranked(x, k, want_rank=True):
    row = lax.broadcasted_iota(jnp.int32, x.shape, 0).astype(F32)
    out_row = lax.broadcasted_iota(jnp.int32, (k, x.shape[1]), 0)
    vals = jnp.zeros((k, x.shape[1]), F32)
    rank = jnp.full(x.shape, float(k), F32)
    for r in range(k):
        m = jnp.max(x, axis=0, keepdims=True)
        vals = jnp.where(out_row == r, m, vals)
        hit = row == jnp.min(jnp.where(x == m, row, float(x.shape[0])), axis=0, keepdims=True)
        if want_rank:
            rank = jnp.where(hit, float(r), rank)
        x = jnp.where(hit, -jnp.inf, x)
    return vals, rank


def _count(mask):
    return jnp.sum(jnp.where(mask, 1.0, 0.0), axis=0, keepdims=True)


def _pack_rows(x):
    return pltpu.bitcast(x.astype(BF16), jnp.uint32)


def _peer_select_chunk(s0, s1):
    k = PEER_TOPK
    sub = lax.broadcasted_iota(jnp.int32, (SUBLANES, LANES), 0)
    v0, rank0 = _top_ranked(s0, k)
    v1, rank1 = _top_ranked(s1, k)
    cand = [v0[0:1] + v1]
    for a in range(1, k):
        cand.append(jnp.where(sub < k // (a + 1), v0[a:a + 1] + v1[:SUBLANES], -jnp.inf))
    top, _ = _top_ranked(jnp.concatenate(cand, axis=0), k, want_rank=False)
    thr = top[k - 1:k]
    z = jnp.sum(jnp.exp(top - top[0:1]), axis=0, keepdims=True)
    above = [_count(ca > thr) for ca in cand]
    equal = [_count(ca == thr) for ca in cand]
    spare = float(k) - functools.reduce(jnp.add, above)
    n0 = jnp.zeros_like(s0)
    for a in range(k):
        take = jnp.clip(spare, 0.0, equal[a])
        spare = spare - equal[a]
        n0 = jnp.where(rank0 == float(a), above[a] + take, n0)
    return n0, jnp.exp(s0 - v0[0:1]) / z, _pack_rows(rank1), _pack_rows(jnp.exp(s1 - v1[0:1]))


def _sorting_network(n):
    pairs = []
    p = 1
    while p < n:
        k = p
        while k >= 1:
            for j in range(k % p, n - k, 2 * k):
                for i in range(min(k, n - j - k)):
                    if (i + j) // (2 * p) == (i + j + k) // (2 * p):
                        pairs.append((i + j, i + j + k))
            k //= 2
        p *= 2
    return pairs


def _sublane_sum(x):
    for shift in (4, 2, 1):
        x = x + pltpu.roll(x, shift, axis=0)
    return x


def _top_sorted(blocks, k):
    def exchange(xs, i, j):
        xs[i], xs[j] = jnp.maximum(xs[i], xs[j]), jnp.minimum(xs[i], xs[j])

    xs = list(blocks[:k])
    for i, j in _sorting_network(k):
        exchange(xs, i, j)
    for extra in blocks[k:]:
        for r in range(k):
            xs[r], extra = jnp.maximum(xs[r], extra), jnp.minimum(xs[r], extra)
    for shift in (4, 2, 1):
        ys = [pltpu.roll(x, shift, axis=0) for x in xs]
        xs = [jnp.maximum(xs[r], ys[k - 1 - r]) for r in range(k)]
        stride = k // 2
        while stride >= 1:
            for r in range(k):
                if r & stride == 0:
                    exchange(xs, r, r + stride)
            stride //= 2
    return xs


def _peer_select_chunk_distinct(s0, s1):
    k = PEER_TOPK
    nb = s0.shape[0] // SUBLANES
    sub = lax.broadcasted_iota(jnp.int32, (SUBLANES, LANES), 0)
    blocks0 = [s0[SUBLANES * i:SUBLANES * (i + 1)] for i in range(nb)]
    blocks1 = [s1[SUBLANES * i:SUBLANES * (i + 1)] for i in range(nb)]
    v0 = _top_sorted(blocks0, k)
    v1 = _top_sorted(blocks1, k)

    def ties(v, blocks):
        same = functools.reduce(jnp.add, [jnp.where(v[r] == v[r + 1], 1.0, 0.0) for r in range(k - 1)])
        reach = _sublane_sum(functools.reduce(jnp.add, [jnp.where(b >= v[k - 1], 1.0, 0.0) for b in blocks]))
        return same + jnp.abs(reach - float(k))

    v1_lo, v1_hi = v1[SUBLANES - 1], v1[2 * SUBLANES - 1]
    for b in range(SUBLANES - 2, -1, -1):
        v1_lo = jnp.where(sub == b, v1[b], v1_lo)
        v1_hi = jnp.where(sub == b, v1[SUBLANES + b], v1_hi)
    cand = [[v0[0] + v1_lo, v0[0] + v1_hi]]
    for a in range(1, k):
        cand.append([jnp.where(sub < k // (a + 1), v0[a] + v1_lo, -jnp.inf)])
    cand_blocks = [c for ca in cand for c in ca]
    top = _top_sorted(cand_blocks[1:] + cand_blocks[:1], k)
    thr = top[k - 1]
    z = functools.reduce(jnp.add, [jnp.exp(t - top[0]) for t in top])
    n_sel = [_sublane_sum(functools.reduce(jnp.add, [jnp.where(c >= thr, 1.0, 0.0) for c in ca])) for ca in cand]
    redo = ties(v0, blocks0) + ties(v1, blocks1) + ties(top, cand_blocks)

    n0, e0, rank1, e1 = [], [], [], []
    for x0, x1 in zip(blocks0, blocks1):
        n = jnp.zeros_like(x0)
        for a in range(k):
            n = jnp.where(x0 == v0[a], n_sel[a], n)
        n0.append(n)
        e0.append(jnp.exp(x0 - v0[0]) / z)
        rank1.append(functools.reduce(jnp.add, [jnp.where(v > x1, 1.0, 0.0) for v in v1]))
        e1.append(jnp.exp(x1 - v1[0]))
    cat = lambda parts: jnp.concatenate(parts, axis=0)
    return (cat(n0), cat(e0), _pack_rows(cat(rank1)), _pack_rows(cat(e1))), redo[0:1]


def _peer_select_kernel(h_ref, wqt_ref, kp_ref, ht_ref, n0_ref, e0_ref, r1_ref, e1_ref, s_ref):
    h = h_ref[...]
    ht_ref[...] = h.T
    q_t = _dot_nt(wqt_ref[...], h)
    nk = PEER_N_KEYS
    for hd in range(PEER_HEADS):
        s_ref[hd] = _dot(kp_ref[...], q_t[hd * PEER_QUERY_DIM:(hd + 1) * PEER_QUERY_DIM].astype(BF16))

    def run(exact):
        redo = jnp.zeros((1, LANES), F32)
        for hd in range(PEER_HEADS):
            for c in range(h.shape[0] // LANES):
                cs = slice(c * LANES, (c + 1) * LANES)
                s0, s1 = s_ref[hd, :nk, cs], s_ref[hd, nk:, cs]
                if exact:
                    outs = _peer_select_chunk(s0, s1)
                else:
                    outs, flag = _peer_select_chunk_distinct(s0, s1)
                    redo = jnp.maximum(redo, flag)
                for ref, val in zip((n0_ref, e0_ref, r1_ref, e1_ref), outs):
                    ref[hd, :, cs] = val
        return redo

    redo = run(exact=False)

    @pl.when(jnp.max(redo) > 0.0)
    def _():
        run(exact=True)


def _peer_select(h2, wq_t, key_pair):
    n, d = h2.shape
    tm = TOK_TILE
    fac = lambda dt: jax.ShapeDtypeStruct((PEER_HEADS, PEER_N_KEYS // (1 if dt == F32 else 2), n), dt)
    fac_spec = pl.BlockSpec((PEER_HEADS, PEER_N_KEYS, tm), lambda i: (0, 0, i))
    packed_spec = pl.BlockSpec((PEER_HEADS, PEER_N_KEYS // 2, tm), lambda i: (0, 0, i))
    return pl.pallas_call(
        _peer_select_kernel,
        grid=(n // tm,),
        in_specs=[
            pl.BlockSpec((tm, d), lambda i: (i, 0)),
            pl.BlockSpec(wq_t.shape, lambda i: (0, 0)),
            pl.BlockSpec(key_pair.shape, lambda i: (0, 0)),
        ],
        out_specs=[pl.BlockSpec((d, tm), lambda i: (0, i)), fac_spec, fac_spec, packed_spec, packed_spec],
        out_shape=[jax.ShapeDtypeStruct((d, n), BF16), fac(F32), fac(F32), fac(jnp.uint32), fac(jnp.uint32)],
        scratch_shapes=[pltpu.VMEM((PEER_HEADS, 2 * PEER_N_KEYS, tm), F32)],
        compiler_params=_cparams("parallel"),
        name="peer_select",
    )(h2, wq_t, key_pair)


def _peer_dense_kernel(ht_ref, u_ref, vt_ref, n0_ref, e0_ref, r1_ref, e1_ref, o_ref, g_ref, act_ref):
    nk = PEER_N_KEYS
    piece = PEER_PIECE
    n_pieces = PEER_EXP_TILE // piece
    tm = ht_ref.shape[1]

    @pl.when(pl.program_id(1) == 0)
    def _():
        o_ref[...] = jnp.zeros_like(o_ref)

    def gate_piece(p):
        rs_n = nk // 2
        zero = jnp.zeros((rs_n, LANES), BF16)
        for sub in range(piece // nk):
            ii = p * (piece // nk) + sub
            for cs in range(0, tm, LANES):
                for rs in range(0, nk, rs_n):
                    w = zero
                    for hd in range(PEER_HEADS):
                        n_sel = jnp.broadcast_to(n0_ref[hd, ii:ii + 1, cs:cs + LANES], (rs_n, LANES)).astype(BF16)
                        e0 = jnp.broadcast_to(e0_ref[hd, ii:ii + 1, cs:cs + LANES], (rs_n, LANES)).astype(BF16)
                        r1 = pltpu.bitcast(r1_ref[hd, rs // 2:(rs + rs_n) // 2, cs:cs + LANES], BF16)
                        e1 = pltpu.bitcast(e1_ref[hd, rs // 2:(rs + rs_n) // 2, cs:cs + LANES], BF16)
                        w = w + jnp.where(r1 < n_sel, e1 * e0, zero)
                    rows = slice(sub * nk + rs, sub * nk + rs + rs_n)
                    act = act_ref[p, rows, cs:cs + LANES]
                    g_ref[p, rows, cs:cs + LANES] = w * jax.nn.gelu(act).astype(BF16)

    for p in range(n_pieces):
        act_ref[p] = _dot(u_ref[0, p * piece:(p + 1) * piece, :], ht_ref[...])
    for p in range(n_pieces):
        gate_piece(p)
        o_ref[...] += _dot(vt_ref[0, :, p * piece:(p + 1) * piece], g_ref[p])


def _peer_dense(h_t, u, v_t, layer, n0, e0, r1, e1):
    d, n = h_t.shape
    tm, te = PEER_TOK_TILE, PEER_EXP_TILE
    sub = te // PEER_N_KEYS
    return pl.pallas_call(
        _peer_dense_kernel,
        grid=(n // tm, PEER_N_EXPERTS // te),
        in_specs=[
            pl.BlockSpec((d, tm), lambda i, e: (0, i)),
            pl.BlockSpec((1, te, d), lambda i, e: (layer, e, 0)),
            pl.BlockSpec((1, d, te), lambda i, e: (layer, 0, e)),
            pl.BlockSpec((PEER_HEADS, sub, tm), lambda i, e: (0, e, i)),
            pl.BlockSpec((PEER_HEADS, sub, tm), lambda i, e: (0, e, i)),
            pl.BlockSpec((PEER_HEADS, PEER_N_KEYS // 2, tm), lambda i, e: (0, 0, i)),
            pl.BlockSpec((PEER_HEADS, PEER_N_KEYS // 2, tm), lambda i, e: (0, 0, i)),
        ],
        out_specs=pl.BlockSpec((d, tm), lambda i, e: (0, i)),
        out_shape=jax.ShapeDtypeStruct((d, n), F32),
        scratch_shapes=[pltpu.VMEM((te // PEER_PIECE, PEER_PIECE, tm), BF16),
                        pltpu.VMEM((te // PEER_PIECE, PEER_PIECE, tm), F32)],
        compiler_params=_cparams("parallel", "arbitrary"),
        name="peer_dense",
    )(h_t, u, v_t, n0, e0, r1, e1)


def _peer_residual_kernel(x_ref, mod_ref, ot_ref, fn_ref, o_ref, *, final):
    x = x_ref[0] + mod_ref[0, 5:6, :] * ot_ref[...].T
    o_ref[0] = _rms(x, fn_ref[...]) if final else x


def _peer_residual(xs, mod, out_t, final_norm, final):
    nb, t, d = xs.shape
    n_tiles = t // TOK_TILE
    tok = lambda b, i: (b, i, 0)
    return pl.pallas_call(
        functools.partial(_peer_residual_kernel, final=final),
        grid=(nb, n_tiles),
        in_specs=[
            pl.BlockSpec((1, TOK_TILE, d), tok),
            pl.BlockSpec((1, N_MOD, d), _mod_row(nb, has_ctx=not final)),
            pl.BlockSpec((d, TOK_TILE), lambda b, i: (0, b * n_tiles + i)),
            pl.BlockSpec((1, d), lambda b, i: (0, 0)),
        ],
        out_specs=pl.BlockSpec((1, TOK_TILE, d), tok),
        out_shape=jax.ShapeDtypeStruct((nb, t, d), F32),
        compiler_params=_cparams("parallel", "parallel"),
        name="peer_residual",
    )(xs, mod, out_t, final_norm.reshape(1, d))


def _rot_cols(w):
    q = MLA_ROPE // 4
    return jnp.concatenate([-w[..., q:2 * q], w[..., :q], -w[..., 3 * q:], w[..., 2 * q:3 * q]], axis=-1)


def _rope_slab(w):
    return jnp.pad(w, ((0, 0), (MLA_NOPE, LANES - MLA_NOPE - MLA_ROPE)))


def _prep_w_in(w_in):
    o = MLA_Q_LORA + MLA_KV_LORA
    w_kr = w_in[:, o:o + MLA_ROPE]
    o2 = o + MLA_ROPE
    n_mid = NA_COLS + 2 * GLA_K_WIDTH + 2 * GLA_V_WIDTH
    low = w_in[:, o2 + n_mid:o2 + n_mid + 2 * GLA_GATE_RANK]
    parts = [w_in[:, :o], _rope_slab(w_kr), _rope_slab(_rot_cols(w_kr)), w_in[:, o2:o2 + n_mid],
             jnp.pad(low, ((0, 0), (0, LANES - 2 * GLA_GATE_RANK))), w_in[:, o2 + n_mid + 2 * GLA_GATE_RANK:]]
    return jnp.concatenate(parts, axis=1).astype(BF16)


def _prep_mla(w_uq, w_ukv):
    r = w_uq.shape[0]
    wq = w_uq.reshape(r, MLA_HEADS, MLA_NOPE + MLA_ROPE)
    pad = MLA_HEAD_PAD - MLA_NOPE - MLA_ROPE
    wq_p = jnp.pad(wq, ((0, 0), (0, 0), (0, pad)))
    wq_rot = jnp.pad(_rot_cols(wq[..., MLA_NOPE:]), ((0, 0), (0, 0), (MLA_NOPE, pad)))
    c = w_ukv.shape[0]
    wkv = w_ukv.reshape(c, MLA_HEADS, MLA_NOPE + MLA_V)
    wk = jnp.pad(wkv[..., :MLA_NOPE], ((0, 0), (0, 0), (0, MLA_HEAD_PAD - MLA_NOPE)))
    wv = wkv[..., MLA_NOPE:]
    hw = MLA_HEADS * MLA_HEAD_PAD
    lane = np.arange(LANES)
    col = np.arange(hw)
    is_rope = (lane >= MLA_NOPE) & (lane < MLA_NOPE + MLA_ROPE)
    e = (is_rope[:, None] & (col[None, :] % MLA_HEAD_PAD == lane[:, None])).astype(np.float32)
    return (wq_p.reshape(r, hw).astype(BF16), wq_rot.reshape(r, hw).astype(BF16),
            wk.reshape(c, hw).astype(BF16), jnp.asarray(e, BF16), wv.reshape(c, MLA_WIDTH).astype(BF16))


def _rope_tables(seq):
    pos = jnp.arange(seq)
    rows = (pos // GRID_W).astype(F32)
    cols = (pos % GRID_W).astype(F32)
    half = MLA_ROPE // 2
    inv = ROPE_THETA ** (-jnp.arange(0, half, 2, dtype=F32) / half)
    ar = rows[:, None] * inv
    ac = cols[:, None] * inv
    ang = jnp.concatenate([ar, ar, ac, ac], axis=-1)
    ang = jnp.concatenate([jnp.zeros((CTX_LEN, MLA_ROPE), F32), ang], axis=0)
    t = ang.shape[0]
    ones = jnp.ones((t, MLA_NOPE), F32)
    zeros = jnp.zeros((t, LANES - MLA_NOPE - MLA_ROPE), F32)
    cos_t = jnp.concatenate([ones, jnp.cos(ang), zeros], axis=1)
    sin_t = jnp.concatenate([0.0 * ones, jnp.sin(ang), zeros], axis=1)
    return cos_t, sin_t


def _prep_gla_gate(w_f, b_f, w_b, b_b):
    wg = jnp.zeros((LANES, 2 * GLA_K_WIDTH), F32)
    wg = wg.at[:GLA_GATE_RANK, :GLA_K_WIDTH].set(w_f)
    wg = wg.at[GLA_GATE_RANK:2 * GLA_GATE_RANK, GLA_K_WIDTH:].set(w_b)
    return wg, jnp.concatenate([b_f, b_b]).reshape(1, -1)


def _prep_peer_keys(sub_keys):
    half = PEER_QUERY_DIM // 2
    kp = jnp.zeros((2 * PEER_N_KEYS, PEER_QUERY_DIM), F32)
    kp = kp.at[:PEER_N_KEYS, :half].set(sub_keys[0])
    kp = kp.at[PEER_N_KEYS:, half:].set(sub_keys[1])
    return kp.astype(BF16)


def kernel(x, c, ctx, c_ctx, w_ada, b_ada, norm1, w_in, mla_q_norm, mla_w_uq, mla_kv_norm, mla_w_ukv, na_rpb, gla_w_gk_fwd, gla_b_gk_fwd, gla_w_gk_bwd, gla_b_gk_bwd, gla_norm, w_o_mla, w_o_na, w_o_gla, w_out, norm2, peer_w_q, peer_sub_keys, peer_u, peer_v, final_norm):
    nb, seq, d = x.shape
    assert ctx.shape[1] == CTX_LEN == TOK_TILE and seq % (NA_ROWS_PER_TILE * GRID_W) == 0
    assert NA_ROWS_PER_TILE * GRID_W == TOK_TILE and nb < 8
    rows = seq // GRID_W
    depth = w_ada.shape[0]

    xs = jnp.concatenate([ctx, x], axis=1)
    t = xs.shape[1]
    cc = jnp.zeros((8, d), F32).at[:nb].set(c).at[nb].set(c_ctx)
    mods = _modulation(cc, w_ada, b_ada).reshape(depth, 8, N_MOD, d)
    cos_t, sin_t = _rope_tables(seq)
    u_all = peer_u.astype(BF16)
    vt_all = jnp.swapaxes(peer_v, 1, 2).astype(BF16)

    for l in range(depth):
        mod = mods[l]
        p_mla, p_na, p_gla, p_gate = _in_projection(xs, mod, norm1[l], _prep_w_in(w_in[l]))
        q, k, v = _mla_prep(p_mla, cos_t, sin_t, mla_q_norm[l], mla_kv_norm[l], *_prep_mla(mla_w_uq[l], mla_w_ukv[l]))
        last = l == depth - 1
        o_a = _mla_attention(q, k, v, ctx_queries=not last)
        o_b = _na_attention(p_na, _na_bias_table(na_rpb[l], rows), rows, ctx_queries=not last)
        gla_f, gla_b = _gla_scan(p_gla, *_prep_gla_gate(gla_w_gk_fwd[l], gla_b_gk_fwd[l], gla_w_gk_bwd[l], gla_b_gk_bwd[l]))
        xs, h2 = _merge(xs, mod, o_a, o_b, gla_f, gla_b, p_gla, p_gate, gla_norm[l], norm2[l],
                        w_o_mla[l].astype(BF16), w_o_na[l].astype(BF16), w_o_gla[l].astype(BF16), w_out[l].astype(BF16),
                        keep_ctx=not last)
        h_t, n0, e0, r1, e1 = _peer_select(h2.reshape(-1, d), peer_w_q[l].T.astype(BF16),
                                           _prep_peer_keys(peer_sub_keys[l]))
        out_t = _peer_dense(h_t, u_all, vt_all, l, n0, e0, r1, e1)
        xs = _peer_residual(xs, mod, out_t, final_norm, final=last)
    return xs
```
